```python
import math
import jax, jax.numpy as jnp
from jax import lax
import numpy as np

D_MODEL = 1024
BATCH = 8
SEQ = 2048
DEPTH = 1
DEC_BATCH = 128
DEC_SEQ = 4
PAST_LEN = 16384
PAGE_SIZE = 128

M_HEADS = 4
M_DK = 128
M_DV = 256
G_HEADS = 4
G_DK = 128
G_DV = 256
G_RANK = 16
G_TAU = 16.0
CHUNK = 64
N_MEM = 256
C_HEADS = 4
C_HD = D_MODEL // C_HEADS
D_FF = 2816
EPS = 1e-6

M_QK = M_HEADS * M_DK
M_V = M_HEADS * M_DV
G_QK = G_HEADS * G_DK
G_V = G_HEADS * G_DV
IN_SPLITS = (M_QK, M_QK, M_V, M_V, M_HEADS, M_HEADS, G_QK, G_QK, G_V, G_V, G_RANK, D_MODEL, D_MODEL)
IN_WIDTH = 2 * M_QK + 2 * M_V + 2 * M_HEADS + 2 * G_QK + 2 * G_V + G_RANK + 2 * D_MODEL

kernel_name = 'hybrid_mlstm_gla_macaron_memxattn_step'


def rmsnorm(x, g):
    xf = x.astype(jnp.float32)
    xf = xf * lax.rsqrt(jnp.mean(xf * xf, axis=-1, keepdims=True) + EPS)
    return (xf * g.astype(jnp.float32)).astype(x.dtype)


def head_rmsnorm(h):
    return h * lax.rsqrt(jnp.mean(h * h, axis=-1, keepdims=True) + EPS)


def swiglu(x, wg, wu, wd):
    return (jax.nn.silu(x @ wg) * (x @ wu)) @ wd


def chunk_len(T):
    return CHUNK if T % CHUNK == 0 else T


def to_chunks(a, L):
    B, T = a.shape[:2]
    return jnp.swapaxes(a.reshape((B, T // L, L) + a.shape[2:]), 0, 1)


def from_chunks(a):
    N, B, L = a.shape[:3]
    return jnp.swapaxes(a, 0, 1).reshape((B, N * L) + a.shape[3:])


def mlstm_chunk(carry, inp):
    C0, n0, m0 = carry
    q, k, v, ig, lf = inp
    L = q.shape[1]
    tri = jnp.tril(jnp.ones((L, L), dtype=bool))
    F = jnp.swapaxes(jnp.cumsum(lf, axis=1), 1, 2)
    igh = jnp.swapaxes(ig, 1, 2)
    D = F[..., :, None] - F[..., None, :] + igh[..., None, :]
    D = jnp.where(tri, D, -jnp.inf)
    m_inter = m0[..., None] + F
    m = jnp.maximum(m_inter, jnp.max(D, axis=-1))
    W = jnp.exp(D - m[..., None])
    a = jnp.exp(m_inter - m)
    S = jnp.einsum('bthd,bshd->bhts', q, k) * W
    num = a[..., None] * jnp.einsum('bhvd,bthd->bhtv', C0, q) + jnp.einsum('bhts,bshv->bhtv', S, v)
    den = a * jnp.einsum('bhd,bthd->bht', n0, q) + jnp.sum(S, axis=-1)
    h = num / jnp.maximum(jnp.abs(den), jnp.exp(-m))[..., None]
    w_end = W[:, :, -1, :]
    a_end = a[:, :, -1]
    C1 = a_end[..., None, None] * C0 + jnp.einsum('bhs,bshv,bshd->bhvd', w_end, v, k)
    n1 = a_end[..., None] * n0 + jnp.einsum('bhs,bshd->bhd', w_end, k)
    return (C1, n1, m[:, :, -1]), jnp.swapaxes(h, 1, 2)


def gla_chunk(S0, inp):
    q, k, v, la = inp
    L = q.shape[1]
    tri = jnp.tril(jnp.ones((L, L), dtype=bool))
    b = jnp.cumsum(la, axis=1)
    inter = jnp.einsum('bthd,bhdv->bthv', q * jnp.exp(b), S0)
    diff = b[:, :, None] - b[:, None, :]
    decay = jnp.exp(jnp.where(tri[None, :, :, None, None], diff, -jnp.inf))
    A = jnp.einsum('bthd,bshd,btshd->bhts', q, k, decay)
    o = inter + jnp.einsum('bhts,bshv->bthv', A, v)
    b_end = b[:, -1]
    S1 = jnp.exp(b_end)[..., None] * S0 + jnp.einsum('bshd,bshv->bhdv', k * jnp.exp(b_end[:, None] - b), v)
    return S1, o


def token_mix(h, C0, n0, m0, S0, P):
    B, T, _ = h.shape
    dt = h.dtype
    f32 = jnp.float32
    z = h @ P['w_in']
    idx = np.cumsum(np.array(IN_SPLITS))[:-1].tolist()
    (mq, mk, mv, mo, mi, mf, gq, gk, gv, gr, ga, gate_m, gate_g) = jnp.split(z, idx, axis=-1)
    L = chunk_len(T)
    q = mq.reshape(B, T, M_HEADS, M_DK).astype(f32)
    k = mk.reshape(B, T, M_HEADS, M_DK).astype(f32) * (M_DK ** -0.5)
    v = mv.reshape(B, T, M_HEADS, M_DV).astype(f32)
    b_if = P['b_if'].astype(f32)
    ig = mi.astype(f32) + b_if[:M_HEADS]
    lf = jax.nn.log_sigmoid(mf.astype(f32) + b_if[M_HEADS:])
    (C1, n1, m1), hm = lax.scan(
        mlstm_chunk, (C0.astype(f32), n0.astype(f32), m0.astype(f32)),
        (to_chunks(q, L), to_chunks(k, L), to_chunks(v, L), to_chunks(ig, L), to_chunks(lf, L)))
    hm = head_rmsnorm(from_chunks(hm)).reshape(B, T, M_V)
    hm = (hm * P['mlstm_norm'].astype(f32) * jax.nn.sigmoid(mo.astype(f32))).astype(dt)
    gq_ = gq.reshape(B, T, G_HEADS, G_DK).astype(f32) * (G_DK ** -0.5)
    gk_ = gk.reshape(B, T, G_HEADS, G_DK).astype(f32)
    gv_ = gv.reshape(B, T, G_HEADS, G_DV).astype(f32)
    la = jax.nn.log_sigmoid((ga @ P['gla_wa2'] + P['gla_ba']).astype(f32)) / G_TAU
    la = la.reshape(B, T, G_HEADS, G_DK)
    S1, hg = lax.scan(gla_chunk, S0.astype(f32),
                      (to_chunks(gq_, L), to_chunks(gk_, L), to_chunks(gv_, L), to_chunks(la, L)))
    hg = head_rmsnorm(from_chunks(hg)).reshape(B, T, G_V)
    hg = (hg * P['gla_norm'].astype(f32) * jax.nn.silu(gr.astype(f32))).astype(dt)
    y = jax.nn.sigmoid(gate_m) * (hm @ P['w_br_m']) + jax.nn.sigmoid(gate_g) * (hg @ P['w_br_g'])
    return y @ P['w_out'], (C1, n1, m1, S1)


def mem_kv(mem, g_mem, wk, wv):
    B = mem.shape[0]
    mn = rmsnorm(mem, g_mem)
    return (mn @ wk).reshape(B, N_MEM, C_HEADS, C_HD), (mn @ wv).reshape(B, N_MEM, C_HEADS, C_HD)


def cross_attn(h, mk, mv, wq, wo):
    B, T, _ = h.shape
    q = (h @ wq).reshape(B, T, C_HEADS, C_HD)
    s = jnp.einsum('bthd,bmhd->bhtm', q, mk).astype(jnp.float32) * (C_HD ** -0.5)
    p = jax.nn.softmax(s, axis=-1).astype(h.dtype)
    o = jnp.einsum('bhtm,bmhd->bthd', p, mv).reshape(B, T, D_MODEL)
    return o @ wo


def layer(x, mk, mv, C0, n0, m0, S0, P):
    x = x + 0.5 * swiglu(rmsnorm(x, P['ffn1_norm']), P['ffn1_wg'], P['ffn1_wu'], P['ffn1_wd'])
    mix, st = token_mix(rmsnorm(x, P['mix_norm']), C0, n0, m0, S0, P)
    x = x + mix
    x = x + cross_attn(rmsnorm(x, P['ca_norm']), mk, mv, P['ca_wq'], P['ca_wo'])
    x = x + 0.5 * swiglu(rmsnorm(x, P['ffn2_norm']), P['ffn2_wg'], P['ffn2_wu'], P['ffn2_wd'])
    return x, st


def setup_inputs(seed: int = 0) -> dict:
    key = jax.random.key(seed)
    ks = iter(jax.random.split(key, 48))
    f32 = jnp.float32

    def nrm(shape, scale):
        return jax.random.normal(next(ks), shape, f32) * scale

    def gain(shape):
        return 1.0 + nrm(shape, 0.02)

    Dp = DEPTH
    b_i = -1.0 + nrm((Dp, M_HEADS), 0.1)
    b_f = 3.0 + nrm((Dp, M_HEADS), 0.5)
    return {
        'x_prompt': nrm((BATCH, SEQ, D_MODEL), 1.0),
        'x_sample': nrm((DEC_BATCH, DEC_SEQ, D_MODEL), 1.0),
        'mem_prompt': nrm((BATCH, N_MEM, D_MODEL), 1.0),
        'state_mlstm_C': nrm((Dp, DEC_BATCH, M_HEADS, M_DV, M_DK), 0.1),
        'state_mlstm_n': nrm((Dp, DEC_BATCH, M_HEADS, M_DK), 0.1),
        'state_mlstm_m': nrm((Dp, DEC_BATCH, M_HEADS), 1.0),
        'state_gla_S': nrm((Dp, DEC_BATCH, G_HEADS, G_DK, G_DV), 0.1),
        'cache_mem_k': nrm((Dp, DEC_BATCH, N_MEM, C_HEADS, C_HD), 1.0),
        'cache_mem_v': nrm((Dp, DEC_BATCH, N_MEM, C_HEADS, C_HD), 1.0),
        'ffn1_norm': gain((Dp, D_MODEL)),
        'ffn1_wg': nrm((Dp, D_MODEL, D_FF), D_MODEL ** -0.5),
        'ffn1_wu': nrm((Dp, D_MODEL, D_FF), D_MODEL ** -0.5),
        'ffn1_wd': nrm((Dp, D_FF, D_MODEL), D_FF ** -0.5),
        'mix_norm': gain((Dp, D_MODEL)),
        'w_in': nrm((Dp, D_MODEL, IN_WIDTH), D_MODEL ** -0.5),
        'b_if': jnp.concatenate([b_i, b_f], axis=-1),
        'gla_wa2': nrm((Dp, G_RANK, G_QK), G_RANK ** -0.5),
        'gla_ba': nrm((Dp, G_QK), 0.1),
        'mlstm_norm': gain((Dp, M_V)),
        'gla_norm': gain((Dp, G_V)),
        'w_br_m': nrm((Dp, M_V, D_MODEL), M_V ** -0.5),
        'w_br_g': nrm((Dp, G_V, D_MODEL), G_V ** -0.5),
        'w_out': nrm((Dp, D_MODEL, D_MODEL), D_MODEL ** -0.5),
        'ca_norm': gain((Dp, D_MODEL)),
        'mem_norm': gain((Dp, D_MODEL)),
        'ca_wq': nrm((Dp, D_MODEL, D_MODEL), D_MODEL ** -0.5),
        'ca_wk': nrm((Dp, D_MODEL, D_MODEL), D_MODEL ** -0.5),
        'ca_wv': nrm((Dp, D_MODEL, D_MODEL), D_MODEL ** -0.5),
        'ca_wo': nrm((Dp, D_MODEL, D_MODEL), D_MODEL ** -0.5),
        'ffn2_norm': gain((Dp, D_MODEL)),
        'ffn2_wg': nrm((Dp, D_MODEL, D_FF), D_MODEL ** -0.5),
        'ffn2_wu': nrm((Dp, D_MODEL, D_FF), D_MODEL ** -0.5),
        'ffn2_wd': nrm((Dp, D_FF, D_MODEL), D_FF ** -0.5),
        'final_norm': gain((D_MODEL,)),
    }


def reference(x_prompt, x_sample, mem_prompt, state_mlstm_C, state_mlstm_n, state_mlstm_m, state_gla_S,
              cache_mem_k, cache_mem_v, ffn1_norm, ffn1_wg, ffn1_wu, ffn1_wd, mix_norm, w_in, b_if,
              gla_wa2, gla_ba, mlstm_norm, gla_norm, w_br_m, w_br_g, w_out, ca_norm, mem_norm,
              ca_wq, ca_wk, ca_wv, ca_wo, ffn2_norm, ffn2_wg, ffn2_wu, ffn2_wd, final_norm):
    f32 = jnp.float32
    Bp = x_prompt.shape[0]
    sdt = state_mlstm_C.dtype
    xp, xs = x_prompt, x_sample
    Cp_l, np_l, mp_l, Sp_l, mkp_l, mvp_l = [], [], [], [], [], []
    Cs_l, ns_l, ms_l, Ss_l = [], [], [], []
    for l in range(DEPTH):
        P = {
            'ffn1_norm': ffn1_norm[l], 'ffn1_wg': ffn1_wg[l], 'ffn1_wu': ffn1_wu[l], 'ffn1_wd': ffn1_wd[l],
            'mix_norm': mix_norm[l], 'w_in': w_in[l], 'b_if': b_if[l], 'gla_wa2': gla_wa2[l],
            'gla_ba': gla_ba[l], 'mlstm_norm': mlstm_norm[l], 'gla_norm': gla_norm[l],
            'w_br_m': w_br_m[l], 'w_br_g': w_br_g[l], 'w_out': w_out[l], 'ca_norm': ca_norm[l],
            'ca_wq': ca_wq[l], 'ca_wo': ca_wo[l], 'ffn2_norm': ffn2_norm[l], 'ffn2_wg': ffn2_wg[l],
            'ffn2_wu': ffn2_wu[l], 'ffn2_wd': ffn2_wd[l],
        }
        mk_p, mv_p = mem_kv(mem_prompt, mem_norm[l], ca_wk[l], ca_wv[l])
        C0 = jnp.zeros((Bp, M_HEADS, M_DV, M_DK), f32)
        n0 = jnp.zeros((Bp, M_HEADS, M_DK), f32)
        m0 = jnp.zeros((Bp, M_HEADS), f32)
        S0 = jnp.zeros((Bp, G_HEADS, G_DK, G_DV), f32)
        xp, (Cp, npr, mp, Sp) = layer(xp, mk_p, mv_p, C0, n0, m0, S0, P)
        xs, (Cs, ns, ms, Ss) = layer(xs, cache_mem_k[l], cache_mem_v[l], state_mlstm_C[l], state_mlstm_n[l],
                                     state_mlstm_m[l], state_gla_S[l], P)
        Cp_l.append(Cp.astype(sdt)); np_l.append(npr.astype(sdt)); mp_l.append(mp.astype(sdt))
        Sp_l.append(Sp.astype(sdt)); mkp_l.append(mk_p); mvp_l.append(mv_p)
        Cs_l.append(Cs.astype(sdt)); ns_l.append(ns.astype(sdt)); ms_l.append(ms.astype(sdt))
        Ss_l.append(Ss.astype(sdt))
    y_prompt = rmsnorm(xp, final_norm)
    y_sample = rmsnorm(xs, final_norm)
    return (y_prompt, y_sample,
            jnp.stack(Cp_l), jnp.stack(np_l), jnp.stack(mp_l), jnp.stack(Sp_l),
            jnp.stack(mkp_l), jnp.stack(mvp_l),
            jnp.stack(Cs_l), jnp.stack(ns_l), jnp.stack(ms_l), jnp.stack(Ss_l))
```

```python
import functools
import math

import jax
import jax.numpy as jnp
from jax import lax
from jax.experimental import pallas as pl
from jax.experimental.pallas import tpu as pltpu

F32 = jnp.float32
BF16 = jnp.bfloat16

D_MODEL = 1024
D_FF = 2816
HEADS = 4
DK = 128
DV = 256
QK = HEADS * DK
VW = HEADS * DV
G_RANK = 16
G_TAU = 16.0
N_MEM = 256
C_HD = D_MODEL // HEADS
EPS = 1e-6
LANES = 128
SUBLANES = 8

Z_SMALL = 2 * QK + 2 * VW + 2 * QK + 2 * VW + 2 * D_MODEL
Z_WIDTH = Z_SMALL + LANES
SM_I = 0
SM_F = HEADS
SM_A = 2 * HEADS

VMEM_LIMIT = 56 * 1024 * 1024


def _cparams(sem):
    return pltpu.CompilerParams(dimension_semantics=sem, vmem_limit_bytes=VMEM_LIMIT)


def _rms(x, g):
    return x * lax.rsqrt(jnp.mean(x * x, axis=-1, keepdims=True) + EPS) * g


def _sigmoid(x):
    return 1.0 / (1.0 + jnp.exp(-x))


def _log_sigmoid(x):
    return jnp.minimum(x, 0.0) - jnp.log(1.0 + jnp.exp(-jnp.abs(x)))


def _dot(a, b):
    return jnp.dot(a, b, preferred_element_type=F32)


def _dot_nt(a, b):
    return lax.dot_general(a, b, (((1,), (1,)), ((), ())), preferred_element_type=F32)


def _split3(x):
    hi = x.astype(BF16)
    r1 = x - hi.astype(F32)
    mid = r1.astype(BF16)
    lo = (r1 - mid.astype(F32)).astype(BF16)
    return hi, mid, lo


def _dot_exact_left(m_bf16, x):
    return sum(_dot(m_bf16, p) for p in _split3(x))


def _dot_exact_right(x, m_bf16):
    return sum(_dot(p, m_bf16) for p in _split3(x))


def _ffn_kernel(*refs, final):
    if final:
        x_ref, g_ref, wg_ref, wu_ref, wd_ref, fg_ref, o_ref, hn_ref, acc_ref = refs
    else:
        x_ref, g_ref, wg_ref, wu_ref, wd_ref, o_ref, hn_ref, acc_ref = refs
    j = pl.program_id(1)

    @pl.when(j == 0)
    def _():
        hn_ref[...] = _rms(x_ref[...], g_ref[...]).astype(BF16)
        acc_ref[...] = jnp.zeros_like(acc_ref)

    hn = hn_ref[...]
    a = _dot(hn, wg_ref[...])
    u = _dot(hn, wu_ref[...])
    act = (a * _sigmoid(a) * u).astype(BF16)
    acc_ref[...] += _dot(act, wd_ref[...])

    @pl.when(j == pl.num_programs(1) - 1)
    def _():
        y = x_ref[...] + 0.5 * acc_ref[...]
        if final:
            y = _rms(y, fg_ref[...])
        o_ref[...] = y


def _ffn(x, g, wg, wu, wd, final_g=None, tm=512, tf=1408):
    n = x.shape[0]
    tm = min(tm, n)
    assert n % tm == 0 and D_FF % tf == 0
    final = final_g is not None
    in_specs = [
        pl.BlockSpec((tm, D_MODEL), lambda i, j: (i, 0)),
        pl.BlockSpec((1, D_MODEL), lambda i, j: (0, 0)),
        pl.BlockSpec((D_MODEL, tf), lambda i, j: (0, j)),
        pl.BlockSpec((D_MODEL, tf), lambda i, j: (0, j)),
        pl.BlockSpec((tf, D_MODEL), lambda i, j: (j, 0)),
    ]
    args = [x, g, wg, wu, wd]
    if final:
        in_specs.append(pl.BlockSpec((1, D_MODEL), lambda i, j: (0, 0)))
        args.append(final_g)
    return pl.pallas_call(
        functools.partial(_ffn_kernel, final=final),
        grid=(n // tm, D_FF // tf),
        in_specs=in_specs,
        out_specs=pl.BlockSpec((tm, D_MODEL), lambda i, j: (i, 0)),
        out_shape=jax.ShapeDtypeStruct((n, D_MODEL), F32),
        scratch_shapes=[pltpu.VMEM((tm, D_MODEL), BF16), pltpu.VMEM((tm, D_MODEL), F32)],
        compiler_params=_cparams(("parallel", "arbitrary")),
        name="ffn_final" if final else "ffn",
    )(*args)


def _normproj_kernel(x_ref, g_ref, w_ref, o_ref, hn_ref):
    @pl.when(pl.program_id(1) == 0)
    def _():
        hn_ref[...] = _rms(x_ref[...], g_ref[...]).astype(BF16)

    o_ref[...] = _dot(hn_ref[...], w_ref[...]).astype(o_ref.dtype)


def _normproj(x, g, w, out_dtype, tn, tm=512, name="normproj"):
    n = x.shape[0]
    width = w.shape[1]
    tm = min(tm, n)
    assert n % tm == 0 and width % tn == 0
    return pl.pallas_call(
        _normproj_kernel,
        grid=(n // tm, width // tn),
        in_specs=[
            pl.BlockSpec((tm, D_MODEL), lambda i, j: (i, 0)),
            pl.BlockSpec((1, D_MODEL), lambda i, j: (0, 0)),
            pl.BlockSpec((D_MODEL, tn), lambda i, j: (0, j)),
        ],
        out_specs=pl.BlockSpec((tm, tn), lambda i, j: (i, j)),
        out_shape=jax.ShapeDtypeStruct((n, width), out_dtype),
        scratch_shapes=[pltpu.VMEM((tm, D_MODEL), BF16)],
        compiler_params=_cparams(("parallel", "arbitrary")),
        name=name,
    )(x, g, w)


def _projres_kernel(a_ref, w_ref, r_ref, o_ref):
    o_ref[...] = r_ref[...] + _dot(a_ref[...], w_ref[...])


def _projres(a, w, res, tm=512):
    n = a.shape[0]
    tm = min(tm, n)
    assert n % tm == 0
    return pl.pallas_call(
        _projres_kernel,
        grid=(n // tm,),
        in_specs=[
            pl.BlockSpec((tm, D_MODEL), lambda i: (i, 0)),
            pl.BlockSpec((D_MODEL, D_MODEL), lambda i: (0, 0)),
            pl.BlockSpec((tm, D_MODEL), lambda i: (i, 0)),
        ],
        out_specs=pl.BlockSpec((tm, D_MODEL), lambda i: (i, 0)),
        out_shape=jax.ShapeDtypeStruct((n, D_MODEL), F32),
        compiler_params=_cparams(("parallel",)),
        name="projres",
    )(a, w, res)


def _merge_kernel(gm_ref, gg_ref, hm_ref, hg_ref, x_ref, wm_ref, wg_ref, wo_ref, o_ref):
    ym = _dot(hm_ref[...], wm_ref[...])
    yg = _dot(hg_ref[...], wg_ref[...])
    y = _sigmoid(gm_ref[...]) * ym + _sigmoid(gg_ref[...]) * yg
    o_ref[...] = x_ref[...] + _dot(y.astype(BF16), wo_ref[...])


def _merge(z, hm, hg, x, wm, wg, wo, tm=512):
    n = x.shape[0]
    tm = min(tm, n)
    assert n % tm == 0
    gate_blk = (2 * QK + 2 * VW + 2 * QK + 2 * VW) // D_MODEL
    row = lambda i: (i, 0)
    full = lambda i: (0, 0)
    return pl.pallas_call(
        _merge_kernel,
        grid=(n // tm,),
        in_specs=[
            pl.BlockSpec((tm, D_MODEL), lambda i: (i, gate_blk)),
            pl.BlockSpec((tm, D_MODEL), lambda i: (i, gate_blk + 1)),
            pl.BlockSpec((tm, VW), row),
            pl.BlockSpec((tm, VW), row),
            pl.BlockSpec((tm, D_MODEL), row),
            pl.BlockSpec((VW, D_MODEL), full),
            pl.BlockSpec((VW, D_MODEL), full),
            pl.BlockSpec((D_MODEL, D_MODEL), full),
        ],
        out_specs=pl.BlockSpec((tm, D_MODEL), row),
        out_shape=jax.ShapeDtypeStruct((n, D_MODEL), F32),
        compiler_params=_cparams(("parallel",)),
        name="merge",
    )(z, z, hm, hg, x, wm, wg, wo)


def _attn_kernel(q_ref, k_ref, v_ref, o_ref):
    scale = C_HD ** -0.5
    for h in range(HEADS):
        cs = slice(h * C_HD, (h + 1) * C_HD)
        qh = q_ref[0, :, cs]
        kh = k_ref[0, :, cs].astype(BF16)
        vh = v_ref[0, :, cs].astype(BF16)
        s = _dot_nt(qh, kh) * scale
        s = s - jnp.max(s, axis=-1, keepdims=True)
        p = jnp.exp(s)
        p = p / jnp.sum(p, axis=-1, keepdims=True)
        o_ref[0, :, cs] = _dot(p.astype(BF16), vh).astype(o_ref.dtype)


def _attn(q, k, v, tq):
    b, t, _ = q.shape
    assert t % tq == 0
    return pl.pallas_call(
        _attn_kernel,
        grid=(b, t // tq),
        in_specs=[
            pl.BlockSpec((1, tq, D_MODEL), lambda i, j: (i, j, 0)),
            pl.BlockSpec((1, N_MEM, D_MODEL), lambda i, j: (i, 0, 0)),
            pl.BlockSpec((1, N_MEM, D_MODEL), lambda i, j: (i, 0, 0)),
        ],
        out_specs=pl.BlockSpec((1, tq, D_MODEL), lambda i, j: (i, j, 0)),
        out_shape=jax.ShapeDtypeStruct((b, t, D_MODEL), BF16),
        compiler_params=_cparams(("parallel", "arbitrary")),
        name="attn",
    )(q, k, v)


def _mlstm_kernel(q_ref, k_ref, v_ref, og_ref, sm_ref, bias_ref, gn_ref, c0_ref, n0_ref, m0_ref,
                  h_ref, c1_ref, n1_ref, m1_ref, c_scr, n_scr, m_scr, *, chunk, t_valid):
    L = chunk
    tb = q_ref.shape[1]
    tstep = pl.program_id(1)
    scale = DK ** -0.5

    @pl.when(tstep == 0)
    def _():
        c_scr[...] = c0_ref[0]
        n_scr[...] = n0_ref[0]
        m_scr[...] = m0_ref[0]

    row = lax.broadcasted_iota(jnp.int32, (L, L), 0)
    col = lax.broadcasted_iota(jnp.int32, (L, L), 1)
    tri = col <= row
    tri_lo = jnp.where(tri, 1.0, 0.0).astype(BF16)
    tri_up = jnp.where(row <= col, 1.0, 0.0).astype(BF16)

    def chunk_body(ci, carry):
        r = pl.multiple_of(ci * L, L)
        rows = pl.ds(r, L)
        g = sm_ref[0, rows, :] + bias_ref[...]
        lf_all = _log_sigmoid(g)
        ig_all = g
        if t_valid < L:
            valid = lax.broadcasted_iota(jnp.int32, (L, 1), 0) < t_valid
            lf_all = jnp.where(valid, lf_all, 0.0)
            ig_all = jnp.where(valid, g, -1e30)
        f_cols = _dot_exact_left(tri_lo, lf_all)
        lf_t = lf_all.T
        ig_t = ig_all.T
        f_rows = _dot_exact_right(lf_t[0:SUBLANES, :], tri_up)

        for h in range(HEADS):
            f_col = f_cols[:, SM_F + h:SM_F + h + 1]
            f_row = f_rows[SM_F + h:SM_F + h + 1, :]
            ig_row = ig_t[SM_I + h:SM_I + h + 1, :]
            ig_col = ig_all[:, SM_I + h:SM_I + h + 1]
            dmat = jnp.where(tri, f_col - f_row + ig_row, -jnp.inf)
            m0 = m_scr[:, h:h + 1]
            m_inter = m0 + f_col
            m = jnp.maximum(m_inter, jnp.max(dmat, axis=1, keepdims=True))
            w = jnp.exp(dmat - m)
            a = jnp.exp(m_inter - m)
            qf = q_ref[0, rows, h * DK:(h + 1) * DK]
            kf = k_ref[0, rows, h * DK:(h + 1) * DK]
            vf = v_ref[0, rows, h * DV:(h + 1) * DV]
            qb = qf.astype(BF16)
            kb = kf.astype(BF16)
            s = _dot_nt(qb, kb) * (w * scale)
            c0 = c_scr[h]
            n0 = n_scr[h:h + 1, :]
            num = a * _dot_nt(qb, c0.astype(BF16)) + _dot(s.astype(BF16), vf.astype(BF16))
            den = a * jnp.sum(qf * n0, axis=1, keepdims=True) + jnp.sum(s, axis=1, keepdims=True)
            hh = num / jnp.maximum(jnp.abs(den), jnp.exp(-m))
            m_end = m[L - 1:L, :]
            a_end = a[L - 1:L, :]
            w_col = jnp.exp(f_col[L - 1:L, :] - f_col + ig_col - m_end) * scale
            vw_t = (vf * w_col).T.astype(BF16)
            c_scr[h] = a_end * c0 + _dot(vw_t, kb)
            n_scr[h:h + 1, :] = a_end * n0 + jnp.sum(kf * w_col, axis=0, keepdims=True)
            m_scr[:, h:h + 1] = m_end
            hn = hh * lax.rsqrt(jnp.mean(hh * hh, axis=-1, keepdims=True) + EPS)
            og = og_ref[0, rows, h * DV:(h + 1) * DV]
            hn = hn * gn_ref[:, h * DV:(h + 1) * DV] * _sigmoid(og)
            h_ref[0, rows, h * DV:(h + 1) * DV] = hn.astype(h_ref.dtype)
        return carry

    lax.fori_loop(0, tb // L, chunk_body, 0)

    @pl.when(tstep == pl.num_programs(1) - 1)
    def _():
        c1_ref[0] = c_scr[...]
        n1_ref[0] = n_scr[...]
        m1_ref[0] = m_scr[...]


def _mlstm(z3, bias_row, gn, c0, n0, m0, chunk, tb, t_valid):
    b, t, _ = z3.shape
    assert t % tb == 0 and tb % chunk == 0
    assert t_valid == t or (tb == t and chunk == t)
    m0 = m0.reshape(b, 1, HEADS)
    bt = lambda col: (lambda i, j: (i, j, col))
    st4 = lambda i, j: (i, 0, 0, 0)
    st3 = lambda i, j: (i, 0, 0)
    h, c1, n1, m1 = pl.pallas_call(
        functools.partial(_mlstm_kernel, chunk=chunk, t_valid=t_valid),
        grid=(b, t // tb),
        in_specs=[
            pl.BlockSpec((1, tb, QK), bt(0)),
            pl.BlockSpec((1, tb, QK), bt(1)),
            pl.BlockSpec((1, tb, VW), bt(1)),
            pl.BlockSpec((1, tb, VW), bt(2)),
            pl.BlockSpec((1, tb, LANES), bt(Z_SMALL // LANES)),
            pl.BlockSpec((1, LANES), lambda i, j: (0, 0)),
            pl.BlockSpec((1, VW), lambda i, j: (0, 0)),
            pl.BlockSpec((1, HEADS, DV, DK), st4),
            pl.BlockSpec((1, HEADS, DK), st3),
            pl.BlockSpec((1, 1, HEADS), st3),
        ],
        out_specs=[
            pl.BlockSpec((1, tb, VW), lambda i, j: (i, j, 0)),
            pl.BlockSpec((1, HEADS, DV, DK), st4),
            pl.BlockSpec((1, HEADS, DK), st3),
            pl.BlockSpec((1, 1, HEADS), st3),
        ],
        out_shape=[
            jax.ShapeDtypeStruct((b, t, VW), BF16),
            jax.ShapeDtypeStruct((b, HEADS, DV, DK), F32),
            jax.ShapeDtypeStruct((b, HEADS, DK), F32),
            jax.ShapeDtypeStruct((b, 1, HEADS), F32),
        ],
        scratch_shapes=[
            pltpu.VMEM((HEADS, DV, DK), F32),
            pltpu.VMEM((HEADS, DK), F32),
            pltpu.VMEM((1, HEADS), F32),
        ],
        compiler_params=_cparams(("parallel", "arbitrary")),
        name="mlstm",
    )(z3, z3, z3, z3, z3, bias_row, gn, c0, n0, m0)
    return h, c1, n1, m1.reshape(b, HEADS)


def _gla_kernel(q_ref, k_ref, v_ref, r_ref, sm_ref, wa_ref, ba_ref, gn_ref, s0_ref,
                h_ref, s1_ref, s_scr, *, chunk, sub, t_valid):
    L = chunk
    nb = L // sub
    tb = q_ref.shape[1]
    tstep = pl.program_id(1)
    scale = DK ** -0.5

    @pl.when(tstep == 0)
    def _():
        s_scr[...] = s0_ref[0]

    row = lax.broadcasted_iota(jnp.int32, (L, L), 0)
    col = lax.broadcasted_iota(jnp.int32, (L, L), 1)
    tri_lo = jnp.where(col <= row, 1.0, 0.0).astype(BF16)
    srow = lax.broadcasted_iota(jnp.int32, (sub, sub, DK), 0)
    scol = lax.broadcasted_iota(jnp.int32, (sub, sub, DK), 1)
    tri_sub = scol <= srow
    eye = lax.broadcasted_iota(jnp.int32, (DK, DK), 0) == lax.broadcasted_iota(jnp.int32, (DK, DK), 1)
    ones_red = jnp.ones((DK, LANES), BF16)
    colb = lax.broadcasted_iota(jnp.int32, (sub, L), 1)
    wa_hi = wa_ref[...].astype(BF16)
    wa_lo = (wa_ref[...] - wa_hi.astype(F32)).astype(BF16)

    def chunk_body(ci, carry):
        r = pl.multiple_of(ci * L, L)
        rows = pl.ds(r, L)
        sm = sm_ref[0, rows, :]
        sm_hi = sm.astype(BF16)
        sm_lo = (sm - sm_hi.astype(F32)).astype(BF16)
        a_raw = _dot(sm_hi, wa_hi) + _dot(sm_lo, wa_hi) + _dot(sm_hi, wa_lo) + ba_ref[...]
        la = _log_sigmoid(a_raw) * (1.0 / G_TAU)
        if t_valid < L:
            valid = lax.broadcasted_iota(jnp.int32, (L, 1), 0) < t_valid
            la = jnp.where(valid, la, 0.0)
        b_all = _dot_exact_left(tri_lo, la)

        for h in range(HEADS):
            ks = slice(h * DK, (h + 1) * DK)
            vs = slice(h * DV, (h + 1) * DV)
            bh = b_all[:, ks]
            qf = q_ref[0, rows, ks] * scale
            kf = k_ref[0, rows, ks]
            vf = v_ref[0, rows, vs]
            vb = vf.astype(BF16)
            s0 = s_scr[h]
            o = _dot((qf * jnp.exp(bh)).astype(BF16), s0.astype(BF16))

            if nb > 1:
                blocks = [jnp.zeros((sub, L), F32)]
                for i in range(1, nb):
                    r0 = i * sub
                    b_ref = bh[r0:r0 + 1, :]
                    qt = qf[r0:r0 + sub, :] * jnp.exp(bh[r0:r0 + sub, :] - b_ref)
                    kt = kf * jnp.exp(jnp.minimum(b_ref - bh, 0.0))
                    p = _dot_nt(qt.astype(BF16), kt.astype(BF16))
                    blocks.append(jnp.where(colb < r0, p, 0.0))
                a_off = jnp.concatenate(blocks, axis=0)
                o = o + _dot(a_off.astype(BF16), vb)

            pds = []
            for i in range(nb):
                r0 = i * sub
                bs = bh[r0:r0 + sub, :]
                dd = bs[:, None, :] - bs[None, :, :]
                e = jnp.exp(jnp.where(tri_sub, dd, -jnp.inf))
                pd = e * qf[r0:r0 + sub, None, :] * kf[None, r0:r0 + sub, :]
                pds.append(pd.reshape(sub * sub, DK))
            pd_all = jnp.concatenate(pds, axis=0) if nb > 1 else pds[0]
            red = _dot(pd_all.astype(BF16), ones_red)
            diag = []
            for i in range(nb):
                r0 = i * sub
                a_blk = red[i * sub * sub:(i + 1) * sub * sub, :].reshape(sub, sub, LANES)
                a_blk = jnp.concatenate([a_blk] * (DV // LANES), axis=-1)
                diag.append(jnp.sum(a_blk * vf[None, r0:r0 + sub, :], axis=1))
            o = o + (jnp.concatenate(diag, axis=0) if nb > 1 else diag[0])

            b_end = bh[L - 1:L, :]
            e_col = jnp.sum(jnp.where(eye, jnp.exp(b_end), 0.0), axis=1, keepdims=True)
            ke_t = (kf * jnp.exp(b_end - bh)).T.astype(BF16)
            s_scr[h] = e_col * s0 + _dot(ke_t, vb)

            hn = o * lax.rsqrt(jnp.mean(o * o, axis=-1, keepdims=True) + EPS)
            rg = r_ref[0, rows, vs]
            hn = hn * gn_ref[:, vs] * (rg * _sigmoid(rg))
            h_ref[0, rows, vs] = hn.astype(h_ref.dtype)
        return carry

    lax.fori_loop(0, tb // L, chunk_body, 0)

    @pl.when(tstep == pl.num_programs(1) - 1)
    def _():
        s1_ref[0] = s_scr[...]


def _gla(z3, wa_pad, ba, gn, s0, chunk, sub, tb, t_valid):
    b, t, _ = z3.shape
    assert t % tb == 0 and tb % chunk == 0 and chunk % sub == 0
    assert t_valid == t or (tb == t and chunk == t)
    g0 = (2 * QK + 2 * VW)
    bt = lambda col: (lambda i, j: (i, j, col))
    st4 = lambda i, j: (i, 0, 0, 0)
    h, s1 = pl.pallas_call(
        functools.partial(_gla_kernel, chunk=chunk, sub=sub, t_valid=t_valid),
        grid=(b, t // tb),
        in_specs=[
            pl.BlockSpec((1, tb, QK), bt(g0 // QK)),
            pl.BlockSpec((1, tb, QK), bt(g0 // QK + 1)),
            pl.BlockSpec((1, tb, VW), bt((g0 + 2 * QK) // VW)),
            pl.BlockSpec((1, tb, VW), bt((g0 + 2 * QK) // VW + 1)),
            pl.BlockSpec((1, tb, LANES), bt(Z_SMALL // LANES)),
            pl.BlockSpec((LANES, QK), lambda i, j: (0, 0)),
            pl.BlockSpec((1, QK), lambda i, j: (0, 0)),
            pl.BlockSpec((1, VW), lambda i, j: (0, 0)),
            pl.BlockSpec((1, HEADS, DK, DV), st4),
        ],
        out_specs=[
            pl.BlockSpec((1, tb, VW), lambda i, j: (i, j, 0)),
            pl.BlockSpec((1, HEADS, DK, DV), st4),
        ],
        out_shape=[
            jax.ShapeDtypeStruct((b, t, VW), BF16),
            jax.ShapeDtypeStruct((b, HEADS, DK, DV), F32),
        ],
        scratch_shapes=[pltpu.VMEM((HEADS, DK, DV), F32)],
        compiler_params=_cparams(("parallel", "arbitrary")),
        name="gla",
    )(z3, z3, z3, z3, z3, wa_pad, ba, gn, s0)
    return h, s1


def _layer(x3, mem_k, mem_v, c0, n0, m0, s0, p, final_norm, t_valid, m_chunk, g_chunk, g_sub, tb, tq):
    b, t, _ = x3.shape
    n = b * t
    x = x3.reshape(n, D_MODEL)
    x = _ffn(x, p["ffn1_norm"], p["ffn1_wg"], p["ffn1_wu"], p["ffn1_wd"])
    z = _normproj(x, p["mix_norm"], p["w_in"], F32, tn=Z_WIDTH // 5, name="inproj")
    z3 = z.reshape(b, t, Z_WIDTH)
    hm, c1, n1, m1 = _mlstm(z3, p["bias_row"], p["mlstm_norm"], c0, n0, m0, m_chunk, tb, t_valid)
    hg, s1 = _gla(z3, p["wa_pad"], p["gla_ba"], p["gla_norm"], s0, g_chunk, g_sub, tb, t_valid)
    x = _merge(z, hm.reshape(n, VW), hg.reshape(n, VW), x, p["w_br_m"], p["w_br_g"], p["w_out"])
    q = _normproj(x, p["ca_norm"], p["ca_wq"], BF16, tn=D_MODEL, name="qproj")
    o = _attn(q.reshape(b, t, D_MODEL), mem_k, mem_v, tq)
    x = _projres(o.reshape(n, D_MODEL), p["ca_wo"], x)
    y = _ffn(x, p["ffn2_norm"], p["ffn2_wg"], p["ffn2_wu"], p["ffn2_wd"], final_g=final_norm)
    return y.reshape(b, t, D_MODEL), (c1, n1, m1, s1)


def _permute_w_in(w_in):
    sizes = (QK, QK, VW, VW, HEADS, HEADS, QK, QK, VW, VW, G_RANK, D_MODEL, D_MODEL)
    offs = [0]
    for s in sizes:
        offs.append(offs[-1] + s)
    part = lambda i: w_in[:, offs[i]:offs[i + 1]]
    (mq, mk, mv, mo, mi, mf, gq, gk, gv, gr, ga, gate_m, gate_g) = [part(i) for i in range(len(sizes))]
    pad = jnp.zeros((D_MODEL, LANES - 2 * HEADS - G_RANK), w_in.dtype)
    return jnp.concatenate([mq, mk, mv, mo, gq, gk, gv, gr, gate_m, gate_g, mi, mf, ga, pad], axis=1)


def kernel(x_prompt, x_sample, mem_prompt, state_mlstm_C, state_mlstm_n, state_mlstm_m, state_gla_S, cache_mem_k, cache_mem_v, ffn1_norm, ffn1_wg, ffn1_wu, ffn1_wd, mix_norm, w_in, b_if, gla_wa2, gla_ba, mlstm_norm, gla_norm, w_br_m, w_br_g, w_out, ca_norm, mem_norm, ca_wq, ca_wk, ca_wv, ca_wo, ffn2_norm, ffn2_wg, ffn2_wu, ffn2_wd, final_norm):
    depth = ffn1_norm.shape[0]
    assert depth == 1
    l = 0
    bp, tp, _ = x_prompt.shape
    bs, ts, _ = x_sample.shape
    row = lambda v: v.reshape(1, -1).astype(F32)
    bias_row = jnp.zeros((1, LANES), F32).at[0, :2 * HEADS].set(b_if[l])
    wa_pad = jnp.zeros((LANES, QK), F32).at[SM_A:SM_A + G_RANK, :].set(gla_wa2[l])
    p = {
        "ffn1_norm": row(ffn1_norm[l]), "ffn1_wg": ffn1_wg[l].astype(BF16), "ffn1_wu": ffn1_wu[l].astype(BF16),
        "ffn1_wd": ffn1_wd[l].astype(BF16),
        "mix_norm": row(mix_norm[l]), "w_in": _permute_w_in(w_in[l]).astype(BF16),
        "bias_row": bias_row, "wa_pad": wa_pad, "gla_ba": row(gla_ba[l]),
        "mlstm_norm": row(mlstm_norm[l]), "gla_norm": row(gla_norm[l]),
        "w_br_m": w_br_m[l].astype(BF16), "w_br_g": w_br_g[l].astype(BF16), "w_out": w_out[l].astype(BF16),
        "ca_norm": row(ca_norm[l]), "ca_wq": ca_wq[l].astype(BF16), "ca_wo": ca_wo[l].astype(BF16),
        "ffn2_norm": row(ffn2_norm[l]), "ffn2_wg": ffn2_wg[l].astype(BF16), "ffn2_wu": ffn2_wu[l].astype(BF16),
        "ffn2_wd": ffn2_wd[l].astype(BF16),
    }
    fin = row(final_norm)

    mem2 = mem_prompt.reshape(bp * N_MEM, D_MODEL)
    mk_p = _normproj(mem2, row(mem_norm[l]), ca_wk[l].astype(BF16), F32, tn=D_MODEL, name="memk")
    mv_p = _normproj(mem2, row(mem_norm[l]), ca_wv[l].astype(BF16), F32, tn=D_MODEL, name="memv")
    mk_p = mk_p.reshape(bp, N_MEM, D_MODEL)
    mv_p = mv_p.reshape(bp, N_MEM, D_MODEL)
    zc = jnp.zeros((bp, HEADS, DV, DK), F32)
    zn = jnp.zeros((bp, HEADS, DK), F32)
    zm = jnp.zeros((bp, HEADS), F32)
    zs = jnp.zeros((bp, HEADS, DK, DV), F32)
    yp, (cp, np_, mp, sp) = _layer(x_prompt, mk_p, mv_p, zc, zn, zm, zs, p, fin, t_valid=tp,
                                   m_chunk=256, g_chunk=64, g_sub=16, tb=512, tq=512)

    tpad = -(-ts // SUBLANES) * SUBLANES
    xs = jnp.pad(x_sample, ((0, 0), (0, tpad - ts), (0, 0)))
    ys, (cs, ns, ms, ss) = _layer(xs, cache_mem_k[l].reshape(bs, N_MEM, D_MODEL),
                                  cache_mem_v[l].reshape(bs, N_MEM, D_MODEL),
                                  state_mlstm_C[l], state_mlstm_n[l], state_mlstm_m[l], state_gla_S[l],
                                  p, fin, t_valid=ts, m_chunk=tpad, g_chunk=tpad, g_sub=tpad, tb=tpad, tq=tpad)
    ys = ys[:, :ts]

    st = lambda a: a[None]
    return (yp, ys, st(cp), st(np_), st(mp), st(sp),
            st(mk_p.reshape(bp, N_MEM, HEADS, C_HD)), st(mv_p.reshape(bp, N_MEM, HEADS, C_HD)),
            st(cs), st(ns), st(ms), st(ss))
```

```python
import functools
import math

import jax
import jax.numpy as jnp
from jax import lax
from jax.experimental import pallas as pl
from jax.experimental.pallas import tpu as pltpu

F32 = jnp.float32
BF16 = jnp.bfloat16

D_MODEL = 1024
D_FF = 2816
HEADS = 4
DK = 128
DV = 256
QK = HEADS * DK
VW = HEADS * DV
G_RANK = 16
G_TAU = 16.0
N_MEM = 256
C_HD = D_MODEL // HEADS
EPS = 1e-6
LANES = 128
SUBLANES = 8

ZA_WIDTH = 2 * (2 * QK + VW)
ZB_SMALL = 2 * VW + 2 * D_MODEL
ZB_WIDTH = ZB_SMALL + LANES
SM_I = 0
SM_F = HEADS
SM_A = 2 * HEADS

VMEM_LIMIT = 56 * 1024 * 1024


def _cparams(sem):
    return pltpu.CompilerParams(dimension_semantics=sem, vmem_limit_bytes=VMEM_LIMIT)


def _rms(x, g):
    return x * lax.rsqrt(jnp.mean(x * x, axis=-1, keepdims=True) + EPS) * g


def _sigmoid(x):
    return 1.0 / (1.0 + jnp.exp(-x))


def _log_sigmoid(x):
    return jnp.minimum(x, 0.0) - jnp.log(1.0 + jnp.exp(-jnp.abs(x)))


def _dot(a, b):
    return jnp.dot(a, b, preferred_element_type=F32)


def _dot_nt(a, b):
    return lax.dot_general(a, b, (((1,), (1,)), ((), ())), preferred_element_type=F32)


def _split3(x):
    hi = x.astype(BF16)
    r1 = x - hi.astype(F32)
    mid = r1.astype(BF16)
    lo = (r1 - mid.astype(F32)).astype(BF16)
    return hi, mid, lo


def _dot_exact_left(m_bf16, x):
    return sum(_dot(m_bf16, p) for p in _split3(x))


def _dot_exact_right(x, m_bf16):
    return sum(_dot(p, m_bf16) for p in _split3(x))


def _ffn_kernel(*refs, final):
    if final:
        x_ref, g_ref, wg_ref, wu_ref, wd_ref, fg_ref, o_ref, hn_ref, acc_ref = refs
    else:
        x_ref, g_ref, wg_ref, wu_ref, wd_ref, o_ref, hn_ref, acc_ref = refs
    j = pl.program_id(1)

    @pl.when(j == 0)
    def _():
        hn_ref[...] = _rms(x_ref[...], g_ref[...]).astype(BF16)
        acc_ref[...] = jnp.zeros_like(acc_ref)

    hn = hn_ref[...]
    a = _dot(hn, wg_ref[...])
    u = _dot(hn, wu_ref[...])
    act = (a * _sigmoid(a) * u).astype(BF16)
    acc_ref[...] += _dot(act, wd_ref[...])

    @pl.when(j == pl.num_programs(1) - 1)
    def _():
        y = x_ref[...] + 0.5 * acc_ref[...]
        if final:
            y = _rms(y, fg_ref[...])
        o_ref[...] = y


def _ffn(x, g, wg, wu, wd, final_g=None, tm=512, tf=1408):
    n = x.shape[0]
    tm = min(tm, n)
    assert n % tm == 0 and D_FF % tf == 0
    final = final_g is not None
    in_specs = [
        pl.BlockSpec((tm, D_MODEL), lambda i, j: (i, 0)),
        pl.BlockSpec((1, D_MODEL), lambda i, j: (0, 0)),
        pl.BlockSpec((D_MODEL, tf), lambda i, j: (0, j)),
        pl.BlockSpec((D_MODEL, tf), lambda i, j: (0, j)),
        pl.BlockSpec((tf, D_MODEL), lambda i, j: (j, 0)),
    ]
    args = [x, g, wg, wu, wd]
    if final:
        in_specs.append(pl.BlockSpec((1, D_MODEL), lambda i, j: (0, 0)))
        args.append(final_g)
    return pl.pallas_call(
        functools.partial(_ffn_kernel, final=final),
        grid=(n // tm, D_FF // tf),
        in_specs=in_specs,
        out_specs=pl.BlockSpec((tm, D_MODEL), lambda i, j: (i, 0)),
        out_shape=jax.ShapeDtypeStruct((n, D_MODEL), F32),
        scratch_shapes=[pltpu.VMEM((tm, D_MODEL), BF16), pltpu.VMEM((tm, D_MODEL), F32)],
        compiler_params=_cparams(("parallel", "arbitrary")),
        name="ffn_final" if final else "ffn",
    )(*args)


def _normproj_kernel(x_ref, g_ref, w_ref, o_ref, hn_ref):
    @pl.when(pl.program_id(1) == 0)
    def _():
        hn_ref[...] = _rms(x_ref[...], g_ref[...]).astype(BF16)

    o_ref[...] = _dot(hn_ref[...], w_ref[...]).astype(o_ref.dtype)


def _normproj(x, g, w, out_dtype, tn, tm=512, name="normproj"):
    n = x.shape[0]
    width = w.shape[1]
    tm = min(tm, n)
    assert n % tm == 0 and width % tn == 0
    return pl.pallas_call(
        _normproj_kernel,
        grid=(n // tm, width // tn),
        in_specs=[
            pl.BlockSpec((tm, D_MODEL), lambda i, j: (i, 0)),
            pl.BlockSpec((1, D_MODEL), lambda i, j: (0, 0)),
            pl.BlockSpec((D_MODEL, tn), lambda i, j: (0, j)),
        ],
        out_specs=pl.BlockSpec((tm, tn), lambda i, j: (i, j)),
        out_shape=jax.ShapeDtypeStruct((n, width), out_dtype),
        scratch_shapes=[pltpu.VMEM((tm, D_MODEL), BF16)],
        compiler_params=_cparams(("parallel", "arbitrary")),
        name=name,
    )(x, g, w)


def _projres_kernel(a_ref, w_ref, r_ref, o_ref):
    o_ref[...] = r_ref[...] + _dot(a_ref[...], w_ref[...])


def _projres(a, w, res, tm=512):
    n = a.shape[0]
    tm = min(tm, n)
    assert n % tm == 0
    return pl.pallas_call(
        _projres_kernel,
        grid=(n // tm,),
        in_specs=[
            pl.BlockSpec((tm, D_MODEL), lambda i: (i, 0)),
            pl.BlockSpec((D_MODEL, D_MODEL), lambda i: (0, 0)),
            pl.BlockSpec((tm, D_MODEL), lambda i: (i, 0)),
        ],
        out_specs=pl.BlockSpec((tm, D_MODEL), lambda i: (i, 0)),
        out_shape=jax.ShapeDtypeStruct((n, D_MODEL), F32),
        compiler_params=_cparams(("parallel",)),
        name="projres",
    )(a, w, res)


def _merge_kernel(gm_ref, gg_ref, hm_ref, hg_ref, x_ref, wm_ref, wg_ref, wo_ref, o_ref):
    ym = _dot(hm_ref[...], wm_ref[...])
    yg = _dot(hg_ref[...], wg_ref[...])
    y = _sigmoid(gm_ref[...]) * ym + _sigmoid(gg_ref[...]) * yg
    o_ref[...] = x_ref[...] + _dot(y.astype(BF16), wo_ref[...])


def _merge(z, hm, hg, x, wm, wg, wo, tm=512):
    n = x.shape[0]
    tm = min(tm, n)
    assert n % tm == 0
    gate_blk = 2 * VW // D_MODEL
    row = lambda i: (i, 0)
    full = lambda i: (0, 0)
    return pl.pallas_call(
        _merge_kernel,
        grid=(n // tm,),
        in_specs=[
            pl.BlockSpec((tm, D_MODEL), lambda i: (i, gate_blk)),
            pl.BlockSpec((tm, D_MODEL), lambda i: (i, gate_blk + 1)),
            pl.BlockSpec((tm, VW), row),
            pl.BlockSpec((tm, VW), row),
            pl.BlockSpec((tm, D_MODEL), row),
            pl.BlockSpec((VW, D_MODEL), full),
            pl.BlockSpec((VW, D_MODEL), full),
            pl.BlockSpec((D_MODEL, D_MODEL), full),
        ],
        out_specs=pl.BlockSpec((tm, D_MODEL), row),
        out_shape=jax.ShapeDtypeStruct((n, D_MODEL), F32),
        compiler_params=_cparams(("parallel",)),
        name="merge",
    )(z, z, hm, hg, x, wm, wg, wo)


def _attn_kernel(q_ref, k_ref, v_ref, o_ref):
    scale = C_HD ** -0.5
    for h in range(HEADS):
        cs = slice(h * C_HD, (h + 1) * C_HD)
        qh = q_ref[0, :, cs]
        kh = k_ref[0, :, cs].astype(BF16)
        vh = v_ref[0, :, cs].astype(BF16)
        s = _dot_nt(qh, kh) * scale
        s = s - jnp.max(s, axis=-1, keepdims=True)
        p = jnp.exp(s)
        p = p / jnp.sum(p, axis=-1, keepdims=True)
        o_ref[0, :, cs] = _dot(p.astype(BF16), vh).astype(o_ref.dtype)


def _attn(q, k, v, tq):
    b, t, _ = q.shape
    assert t % tq == 0
    return pl.pallas_call(
        _attn_kernel,
        grid=(b, t // tq),
        in_specs=[
            pl.BlockSpec((1, tq, D_MODEL), lambda i, j: (i, j, 0)),
            pl.BlockSpec((1, N_MEM, D_MODEL), lambda i, j: (i, 0, 0)),
            pl.BlockSpec((1, N_MEM, D_MODEL), lambda i, j: (i, 0, 0)),
        ],
        out_specs=pl.BlockSpec((1, tq, D_MODEL), lambda i, j: (i, j, 0)),
        out_shape=jax.ShapeDtypeStruct((b, t, D_MODEL), BF16),
        compiler_params=_cparams(("parallel", "arbitrary")),
        name="attn",
    )(q, k, v)


CACHE_HALVES = C_HD // LANES
CACHE_ROWS = HEADS * CACHE_HALVES


def _attn_cache_kernel(q_ref, k_ref, v_ref, o_ref, *, t):
    scale = C_HD ** -0.5
    nq = t * HEADS
    lane = lax.broadcasted_iota(jnp.int32, (nq, LANES), 1)
    rowi = lax.broadcasted_iota(jnp.int32, (nq, LANES), 0)
    valid = ((lane % CACHE_ROWS) // HEADS == 0) & (lane % HEADS == rowi % HEADS)
    n_tiles = N_MEM * CACHE_ROWS // LANES
    for bi in range(q_ref.shape[0]):
        kn = k_ref[bi].astype(BF16)
        vn = v_ref[bi].astype(BF16)
        s = _dot_nt(q_ref[bi], kn)
        tiles = []
        for j in range(n_tiles):
            cs = slice(j * LANES, (j + 1) * LANES)
            sj = s[0:nq, cs] + pltpu.roll(s[nq:2 * nq, cs], LANES - HEADS, 1)
            tiles.append(jnp.where(valid, sj * scale, -jnp.inf))
        mx = functools.reduce(jnp.maximum, [jnp.max(x, axis=1, keepdims=True) for x in tiles])
        ps = [jnp.exp(x - mx) for x in tiles]
        den = functools.reduce(jnp.add, [jnp.sum(x, axis=1, keepdims=True) for x in ps])
        inv = 1.0 / den
        p0 = jnp.concatenate([x * inv for x in ps], axis=1)
        p1 = jnp.concatenate([pltpu.roll(x * inv, HEADS, 1) for x in ps], axis=1)
        p = jnp.concatenate([p0, p1], axis=0).astype(BF16)
        o_ref[bi] = _dot(p, vn).astype(o_ref.dtype)


def _attn_cache(q, k, v, t, bb):
    assert CACHE_HALVES == 2
    b = q.shape[0]
    assert b % bb == 0
    nq = t * HEADS
    qv = q[:, :t].reshape(b, t, HEADS, CACHE_HALVES, LANES).transpose(0, 3, 1, 2, 4)
    qv = qv.reshape(b, CACHE_HALVES * nq, LANES)

    def cache_view(a):
        a = a.reshape(b, N_MEM, HEADS, CACHE_HALVES, LANES).transpose(0, 1, 3, 2, 4)
        return a.reshape(b, N_MEM * CACHE_ROWS, LANES)

    o = pl.pallas_call(
        functools.partial(_attn_cache_kernel, t=t),
        grid=(b // bb,),
        in_specs=[
            pl.BlockSpec((bb, CACHE_HALVES * nq, LANES), lambda i: (i, 0, 0)),
            pl.BlockSpec((bb, N_MEM * CACHE_ROWS, LANES), lambda i: (i, 0, 0)),
            pl.BlockSpec((bb, N_MEM * CACHE_ROWS, LANES), lambda i: (i, 0, 0)),
        ],
        out_specs=pl.BlockSpec((bb, CACHE_HALVES * nq, LANES), lambda i: (i, 0, 0)),
        out_shape=jax.ShapeDtypeStruct((b, CACHE_HALVES * nq, LANES), BF16),
        compiler_params=_cparams(("parallel",)),
        name="attn_cache",
    )(qv, cache_view(k), cache_view(v))
    o = o.reshape(b, CACHE_HALVES, t, HEADS, LANES).transpose(0, 2, 3, 1, 4).reshape(b, t, D_MODEL)
    return jnp.pad(o, ((0, 0), (0, q.shape[1] - t), (0, 0)))


def _mlstm_kernel(q_ref, k_ref, v_ref, og_ref, sm_ref, bias_ref, gn_ref, c0_ref, n0_ref, m0_ref,
                  h_ref, c1_ref, n1_ref, m1_ref, c_scr, n_scr, m_scr, *, chunk, t_valid):
    L = chunk
    tb = q_ref.shape[1]
    tstep = pl.program_id(1)
    scale = DK ** -0.5

    @pl.when(tstep == 0)
    def _():
        c_scr[...] = c0_ref[0]
        n_scr[...] = n0_ref[0]
        m_scr[...] = m0_ref[0]

    row = lax.broadcasted_iota(jnp.int32, (L, L), 0)
    col = lax.broadcasted_iota(jnp.int32, (L, L), 1)
    tri = col <= row
    tri_lo = jnp.where(tri, 1.0, 0.0).astype(BF16)
    tri_up = jnp.where(row <= col, 1.0, 0.0).astype(BF16)

    def chunk_body(ci, carry):
        r = pl.multiple_of(ci * L, L)
        rows = pl.ds(r, L)
        g = sm_ref[0, rows, :] + bias_ref[...]
        lf_all = _log_sigmoid(g)
        ig_all = g
        if t_valid < L:
            valid = lax.broadcasted_iota(jnp.int32, (L, 1), 0) < t_valid
            lf_all = jnp.where(valid, lf_all, 0.0)
            ig_all = jnp.where(valid, g, -1e30)
        f_cols = _dot_exact_left(tri_lo, lf_all)
        lf_t = lf_all.T
        ig_t = ig_all.T
        f_rows = _dot_exact_right(lf_t[0:SUBLANES, :], tri_up)

        for h in range(HEADS):
            f_col = f_cols[:, SM_F + h:SM_F + h + 1]
            f_row = f_rows[SM_F + h:SM_F + h + 1, :]
            ig_row = ig_t[SM_I + h:SM_I + h + 1, :]
            ig_col = ig_all[:, SM_I + h:SM_I + h + 1]
            dmat = jnp.where(tri, f_col - f_row + ig_row, -jnp.inf)
            m0 = m_scr[:, h:h + 1]
            m_inter = m0 + f_col
            m = jnp.maximum(m_inter, jnp.max(dmat, axis=1, keepdims=True))
            w = jnp.exp(dmat - m)
            a = jnp.exp(m_inter - m)
            qb = q_ref[0, rows, h * DK:(h + 1) * DK]
            kb = k_ref[0, rows, h * DK:(h + 1) * DK]
            vb = v_ref[0, rows, h * DV:(h + 1) * DV]
            qf = qb.astype(F32)
            kf = kb.astype(F32)
            vf = vb.astype(F32)
            s = _dot_nt(qb, kb) * (w * scale)
            c0 = c_scr[h]
            n0 = n_scr[h:h + 1, :]
            num = a * _dot_nt(qb, c0.astype(BF16)) + _dot(s.astype(BF16), vb)
            den = a * jnp.sum(qf * n0, axis=1, keepdims=True) + jnp.sum(s, axis=1, keepdims=True)
            hh = num / jnp.maximum(jnp.abs(den), jnp.exp(-m))
            m_end = m[L - 1:L, :]
            a_end = a[L - 1:L, :]
            w_col = jnp.exp(f_col[L - 1:L, :] - f_col + ig_col - m_end) * scale
            vw_t = (vf * w_col).T.astype(BF16)
            c_scr[h] = a_end * c0 + _dot(vw_t, kb)
            n_scr[h:h + 1, :] = a_end * n0 + jnp.sum(kf * w_col, axis=0, keepdims=True)
            m_scr[:, h:h + 1] = m_end
            hn = hh * lax.rsqrt(jnp.mean(hh * hh, axis=-1, keepdims=True) + EPS)
            og = og_ref[0, rows, h * DV:(h + 1) * DV]
            hn = hn * gn_ref[:, h * DV:(h + 1) * DV] * _sigmoid(og)
            h_ref[0, rows, h * DV:(h + 1) * DV] = hn.astype(h_ref.dtype)
        return carry

    lax.fori_loop(0, tb // L, chunk_body, 0)

    @pl.when(tstep == pl.num_programs(1) - 1)
    def _():
        c1_ref[0] = c_scr[...]
        n1_ref[0] = n_scr[...]
        m1_ref[0] = m_scr[...]


def _mlstm(za3, zb3, bias_row, gn, c0, n0, m0, chunk, tb, t_valid):
    b, t, _ = za3.shape
    assert t % tb == 0 and tb % chunk == 0
    assert t_valid == t or (tb == t and chunk == t)
    m0 = m0.reshape(b, 1, HEADS)
    bt = lambda col: (lambda i, j: (i, j, col))
    st4 = lambda i, j: (i, 0, 0, 0)
    st3 = lambda i, j: (i, 0, 0)
    h, c1, n1, m1 = pl.pallas_call(
        functools.partial(_mlstm_kernel, chunk=chunk, t_valid=t_valid),
        grid=(b, t // tb),
        in_specs=[
            pl.BlockSpec((1, tb, QK), bt(0)),
            pl.BlockSpec((1, tb, QK), bt(1)),
            pl.BlockSpec((1, tb, VW), bt(1)),
            pl.BlockSpec((1, tb, VW), bt(0)),
            pl.BlockSpec((1, tb, LANES), bt(ZB_SMALL // LANES)),
            pl.BlockSpec((1, LANES), lambda i, j: (0, 0)),
            pl.BlockSpec((1, VW), lambda i, j: (0, 0)),
            pl.BlockSpec((1, HEADS, DV, DK), st4),
            pl.BlockSpec((1, HEADS, DK), st3),
            pl.BlockSpec((1, 1, HEADS), st3),
        ],
        out_specs=[
            pl.BlockSpec((1, tb, VW), lambda i, j: (i, j, 0)),
            pl.BlockSpec((1, HEADS, DV, DK), st4),
            pl.BlockSpec((1, HEADS, DK), st3),
            pl.BlockSpec((1, 1, HEADS), st3),
        ],
        out_shape=[
            jax.ShapeDtypeStruct((b, t, VW), BF16),
            jax.ShapeDtypeStruct((b, HEADS, DV, DK), F32),
            jax.ShapeDtypeStruct((b, HEADS, DK), F32),
            jax.ShapeDtypeStruct((b, 1, HEADS), F32),
        ],
        scratch_shapes=[
            pltpu.VMEM((HEADS, DV, DK), F32),
            pltpu.VMEM((HEADS, DK), F32),
            pltpu.VMEM((1, HEADS), F32),
        ],
        compiler_params=_cparams(("parallel", "arbitrary")),
        name="mlstm",
    )(za3, za3, za3, zb3, zb3, bias_row, gn, c0, n0, m0)
    return h, c1, n1, m1.reshape(b, HEADS)


def _gla_kernel(q_ref, k_ref, v_ref, r_ref, sm_ref, wa_ref, ba_ref, gn_ref, s0_ref,
                h_ref, s1_ref, s_scr, b_scr, kf_scr, *, chunk, sub, t_valid):
    L = chunk
    nb = L // sub
    tb = q_ref.shape[1]
    tstep = pl.program_id(1)
    scale = DK ** -0.5

    @pl.when(tstep == 0)
    def _():
        s_scr[...] = s0_ref[0]

    row = lax.broadcasted_iota(jnp.int32, (L, L), 0)
    col = lax.broadcasted_iota(jnp.int32, (L, L), 1)
    tri_lo = jnp.where(col <= row, 1.0, 0.0).astype(BF16)
    in_blk = (col // sub) == (row // sub)
    trow = lax.broadcasted_iota(jnp.int32, (sub, DK), 0)
    eye = lax.broadcasted_iota(jnp.int32, (DK, DK), 0) == lax.broadcasted_iota(jnp.int32, (DK, DK), 1)
    place = jnp.where(lax.broadcasted_iota(jnp.int32, (sub * DK, LANES), 0) // DK
                      == lax.broadcasted_iota(jnp.int32, (sub * DK, LANES), 1) % sub, 1.0, 0.0).astype(BF16)
    wa_hi = wa_ref[...].astype(BF16)
    wa_lo = (wa_ref[...] - wa_hi.astype(F32)).astype(BF16)

    sm = sm_ref[0]
    sm_hi = sm.astype(BF16)
    sm_lo = (sm - sm_hi.astype(F32)).astype(BF16)
    a_raw = _dot(sm_hi, wa_hi) + _dot(sm_lo, wa_hi) + _dot(sm_hi, wa_lo) + ba_ref[...]
    la = _log_sigmoid(a_raw) * (math.log2(math.e) / G_TAU)
    if t_valid < L:
        valid = lax.broadcasted_iota(jnp.int32, (L, 1), 0) < t_valid
        la = jnp.where(valid, la, 0.0)
    for c in range(tb // L):
        b_scr[c * L:(c + 1) * L, :] = _dot_exact_left(tri_lo, la[c * L:(c + 1) * L, :])
    kf_scr[...] = k_ref[0].astype(F32)

    def chunk_body(ci, carry):
        r = pl.multiple_of(ci * L, L)
        rows = pl.ds(r, L)

        slabs = []
        for h in range(HEADS):
            ks = slice(h * DK, (h + 1) * DK)
            qf = q_ref[0, rows, ks].astype(F32) * scale
            for i in range(nb):
                blk = pl.ds(pl.multiple_of(r + i * sub, sub), sub)
                bs = b_scr[blk, ks]
                kblk = kf_scr[blk, ks]
                qs = qf[i * sub:(i + 1) * sub, :]
                row_slabs = []
                for s in range(sub):
                    e = jnp.exp2(jnp.where(trow >= s, bs - bs[s:s + 1, :], -jnp.inf))
                    row_slabs.append(e * qs * kblk[s:s + 1, :])
                slabs.append(jnp.concatenate(row_slabs, axis=1))
        own = _dot(jnp.concatenate(slabs, axis=0).astype(BF16), place)

        for h in range(HEADS):
            ks = slice(h * DK, (h + 1) * DK)
            vs = slice(h * DV, (h + 1) * DV)
            bh = b_scr[rows, ks]
            qf = q_ref[0, rows, ks].astype(F32) * scale
            kf = kf_scr[rows, ks]
            vb = v_ref[0, rows, vs]
            s0 = s_scr[h]
            o = _dot((qf * jnp.exp2(bh)).astype(BF16), s0.astype(BF16))

            a_own = own[h * L:(h + 1) * L, 0:L]
            if nb > 1:
                blocks = [jnp.zeros((sub, L), F32)]
                for i in range(1, nb):
                    r0 = i * sub
                    b_ref = bh[r0:r0 + 1, :]
                    qt = qf[r0:r0 + sub, :] * jnp.exp2(bh[r0:r0 + sub, :] - b_ref)
                    kt = kf[0:r0, :] * jnp.exp2(b_ref - bh[0:r0, :])
                    kt = jnp.concatenate([kt, jnp.zeros((L - r0, DK), F32)], axis=0)
                    blocks.append(_dot_nt(qt.astype(BF16), kt.astype(BF16)))
                a_intra = jnp.where(in_blk, a_own, jnp.concatenate(blocks, axis=0))
                o = o + _dot(a_intra.astype(BF16), vb)
            else:
                o = o + _dot(a_own, vb.astype(F32))

            b_end = bh[L - 1:L, :]
            e_col = jnp.sum(jnp.where(eye, jnp.exp2(b_end), 0.0), axis=1, keepdims=True)
            ke_t = (kf * jnp.exp2(b_end - bh)).T.astype(BF16)
            s_scr[h] = e_col * s0 + _dot(ke_t, vb)

            hn = o * lax.rsqrt(jnp.mean(o * o, axis=-1, keepdims=True) + EPS)
            rg = r_ref[0, rows, vs]
            hn = hn * gn_ref[:, vs] * (rg * _sigmoid(rg))
            h_ref[0, rows, vs] = hn.astype(h_ref.dtype)
        return carry

    lax.fori_loop(0, tb // L, chunk_body, 0, unroll=2 if tb // L >= 2 else 1)

    @pl.when(tstep == pl.num_programs(1) - 1)
    def _():
        s1_ref[0] = s_scr[...]


def _gla(za3, zb3, wa_pad, ba, gn, s0, chunk, sub, tb, t_valid):
    b, t, _ = za3.shape
    assert t % tb == 0 and tb % chunk == 0 and chunk % sub == 0
    assert t_valid == t or (tb == t and chunk == t)
    g0 = 2 * QK + VW
    bt = lambda col: (lambda i, j: (i, j, col))
    st4 = lambda i, j: (i, 0, 0, 0)
    h, s1 = pl.pallas_call(
        functools.partial(_gla_kernel, chunk=chunk, sub=sub, t_valid=t_valid),
        grid=(b, t // tb),
        in_specs=[
            pl.BlockSpec((1, tb, QK), bt(g0 // QK)),
            pl.BlockSpec((1, tb, QK), bt(g0 // QK + 1)),
            pl.BlockSpec((1, tb, VW), bt((g0 + 2 * QK) // VW)),
            pl.BlockSpec((1, tb, VW), bt(1)),
            pl.BlockSpec((1, tb, LANES), bt(ZB_SMALL // LANES)),
            pl.BlockSpec((LANES, QK), lambda i, j: (0, 0)),
            pl.BlockSpec((1, QK), lambda i, j: (0, 0)),
            pl.BlockSpec((1, VW), lambda i, j: (0, 0)),
            pl.BlockSpec((1, HEADS, DK, DV), st4),
        ],
        out_specs=[
            pl.BlockSpec((1, tb, VW), lambda i, j: (i, j, 0)),
            pl.BlockSpec((1, HEADS, DK, DV), st4),
        ],
        out_shape=[
            jax.ShapeDtypeStruct((b, t, VW), BF16),
            jax.ShapeDtypeStruct((b, HEADS, DK, DV), F32),
        ],
        scratch_shapes=[pltpu.VMEM((HEADS, DK, DV), F32), pltpu.VMEM((tb, QK), F32), pltpu.VMEM((tb, QK), F32)],
        compiler_params=_cparams(("parallel", "arbitrary")),
        name="gla",
    )(za3, za3, za3, zb3, zb3, wa_pad, ba, gn, s0)
    return h, s1


def _layer(x3, mem_k, mem_v, c0, n0, m0, s0, p, final_norm, t_valid, m_chunk, g_chunk, g_sub, tb, tq):
    b, t, _ = x3.shape
    n = b * t
    x = x3.reshape(n, D_MODEL)
    x = _ffn(x, p["ffn1_norm"], p["ffn1_wg"], p["ffn1_wu"], p["ffn1_wd"])
    za = _normproj(x, p["mix_norm"], p["w_in_a"], BF16, tn=ZA_WIDTH // 2, tm=1024, name="inproj_a")
    zb = _normproj(x, p["mix_norm"], p["w_in_b"], F32, tn=ZB_WIDTH // 3, tm=1024, name="inproj_b")
    za3 = za.reshape(b, t, ZA_WIDTH)
    zb3 = zb.reshape(b, t, ZB_WIDTH)
    hm, c1, n1, m1 = _mlstm(za3, zb3, p["bias_row"], p["mlstm_norm"], c0, n0, m0, m_chunk, tb, t_valid)
    hg, s1 = _gla(za3, zb3, p["wa_pad"], p["gla_ba"], p["gla_norm"], s0, g_chunk, g_sub, tb, t_valid)
    x = _merge(zb, hm.reshape(n, VW), hg.reshape(n, VW), x, p["w_br_m"], p["w_br_g"], p["w_out"])
    q = _normproj(x, p["ca_norm"], p["ca_wq"], BF16, tn=D_MODEL, name="qproj")
    if mem_k.ndim == 4:
        o = _attn_cache(q.reshape(b, t, D_MODEL), mem_k, mem_v, t=t_valid, bb=4)
    else:
        o = _attn(q.reshape(b, t, D_MODEL), mem_k, mem_v, tq)
    x = _projres(o.reshape(n, D_MODEL), p["ca_wo"], x)
    y = _ffn(x, p["ffn2_norm"], p["ffn2_wg"], p["ffn2_wu"], p["ffn2_wd"], final_g=final_norm)
    return y.reshape(b, t, D_MODEL), (c1, n1, m1, s1)


def _permute_w_in(w_in):
    sizes = (QK, QK, VW, VW, HEADS, HEADS, QK, QK, VW, VW, G_RANK, D_MODEL, D_MODEL)
    offs = [0]
    for s in sizes:
        offs.append(offs[-1] + s)
    part = lambda i: w_in[:, offs[i]:offs[i + 1]]
    (mq, mk, mv, mo, mi, mf, gq, gk, gv, gr, ga, gate_m, gate_g) = [part(i) for i in range(len(sizes))]
    pad = jnp.zeros((D_MODEL, LANES - 2 * HEADS - G_RANK), w_in.dtype)
    w_a = jnp.concatenate([mq, mk, mv, gq, gk, gv], axis=1)
    w_b = jnp.concatenate([mo, gr, gate_m, gate_g, mi, mf, ga, pad], axis=1)
    return w_a.astype(BF16), w_b.astype(BF16)


def kernel(x_prompt, x_sample, mem_prompt, state_mlstm_C, state_mlstm_n, state_mlstm_m, state_gla_S, cache_mem_k, cache_mem_v, ffn1_norm, ffn1_wg, ffn1_wu, ffn1_wd, mix_norm, w_in, b_if, gla_wa2, gla_ba, mlstm_norm, gla_norm, w_br_m, w_br_g, w_out, ca_norm, mem_norm, ca_wq, ca_wk, ca_wv, ca_wo, ffn2_norm, ffn2_wg, ffn2_wu, ffn2_wd, final_norm):
    depth = ffn1_norm.shape[0]
    assert depth == 1
    l = 0
    bp, tp, _ = x_prompt.shape
    bs, ts, _ = x_sample.shape
    row = lambda v: v.reshape(1, -1).astype(F32)
    bias_row = jnp.zeros((1, LANES), F32).at[0, :2 * HEADS].set(b_if[l])
    wa_pad = jnp.zeros((LANES, QK), F32).at[SM_A:SM_A + G_RANK, :].set(gla_wa2[l])
    w_in_a, w_in_b = _permute_w_in(w_in[l])
    p = {
        "w_in_a": w_in_a, "w_in_b": w_in_b,
        "ffn1_norm": row(ffn1_norm[l]), "ffn1_wg": ffn1_wg[l].astype(BF16), "ffn1_wu": ffn1_wu[l].astype(BF16),
        "ffn1_wd": ffn1_wd[l].astype(BF16),
        "mix_norm": row(mix_norm[l]),
        "bias_row": bias_row, "wa_pad": wa_pad, "gla_ba": row(gla_ba[l]),
        "mlstm_norm": row(mlstm_norm[l]), "gla_norm": row(gla_norm[l]),
        "w_br_m": w_br_m[l].astype(BF16), "w_br_g": w_br_g[l].astype(BF16), "w_out": w_out[l].astype(BF16),
        "ca_norm": row(ca_norm[l]), "ca_wq": ca_wq[l].astype(BF16), "ca_wo": ca_wo[l].astype(BF16),
        "ffn2_norm": row(ffn2_norm[l]), "ffn2_wg": ffn2_wg[l].astype(BF16), "ffn2_wu": ffn2_wu[l].astype(BF16),
        "ffn2_wd": ffn2_wd[l].astype(BF16),
    }
    fin = row(final_norm)

    mem2 = mem_prompt.reshape(bp * N_MEM, D_MODEL)
    mk_p = _normproj(mem2, row(mem_norm[l]), ca_wk[l].astype(BF16), F32, tn=D_MODEL, name="memk")
    mv_p = _normproj(mem2, row(mem_norm[l]), ca_wv[l].astype(BF16), F32, tn=D_MODEL, name="memv")
    mk_p = mk_p.reshape(bp, N_MEM, D_MODEL)
    mv_p = mv_p.reshape(bp, N_MEM, D_MODEL)
    zc = jnp.zeros((bp, HEADS, DV, DK), F32)
    zn = jnp.zeros((bp, HEADS, DK), F32)
    zm = jnp.zeros((bp, HEADS), F32)
    zs = jnp.zeros((bp, HEADS, DK, DV), F32)
    yp, (cp, np_, mp, sp) = _layer(x_prompt, mk_p, mv_p, zc, zn, zm, zs, p, fin, t_valid=tp,
                                   m_chunk=256, g_chunk=64, g_sub=8, tb=512, tq=512)

    tpad = -(-ts // SUBLANES) * SUBLANES
    xs = jnp.pad(x_sample, ((0, 0), (0, tpad - ts), (0, 0)))
    ys, (cs, ns, ms, ss) = _layer(xs, cache_mem_k[l], cache_mem_v[l],
                                  state_mlstm_C[l], state_mlstm_n[l], state_mlstm_m[l], state_gla_S[l],
                                  p, fin, t_valid=ts, m_chunk=tpad, g_chunk=tpad, g_sub=tpad, tb=tpad, tq=tpad)
    ys = ys[:, :ts]

    st = lambda a: a[None]
    return (yp, ys, st(cp), st(np_), st(mp), st(sp),
            st(mk_p.reshape(bp, N_MEM, HEADS, C_HD)), st(mv_p.reshape(bp, N_MEM, HEADS, C_HD)),
            st(cs), st(ns), st(ms), st(ss))
```

```python
import functools
import math

import jax
import jax.numpy as jnp
from jax import lax
from jax.experimental import pallas as pl
from jax.experimental.pallas import tpu as pltpu

F32 = jnp.float32
BF16 = jnp.bfloat16

D_MODEL = 1024
D_FF = 2816
HEADS = 4
DK = 128
DV = 256
QK = HEADS * DK
VW = HEADS * DV
G_RANK = 16
G_TAU = 16.0
N_MEM = 256
C_HD = D_MODEL // HEADS
EPS = 1e-6
LANES = 128
SUBLANES = 8

Z_MQ = 0
Z_MK = Z_MQ + QK
Z_MV = Z_MK + QK
Z_GQ = Z_MV + VW
Z_GK = Z_GQ + QK
Z_GV = Z_GK + QK
Z_MO = Z_GV + VW
Z_GR = Z_MO + VW
Z_GATE_M = Z_GR + VW
Z_GATE_G = Z_GATE_M + D_MODEL
Z_WIDTH = Z_GATE_G + D_MODEL
SM_I = 0
SM_F = HEADS
SM_A = 2 * HEADS

VMEM_LIMIT = 56 * 1024 * 1024


def _cparams(sem):
    return pltpu.CompilerParams(dimension_semantics=sem, vmem_limit_bytes=VMEM_LIMIT)


def _rms(x, g):
    return x * lax.rsqrt(jnp.mean(x * x, axis=-1, keepdims=True) + EPS) * g


def _sigmoid(x):
    return 1.0 / (1.0 + jnp.exp(-x))


def _log_sigmoid(x):
    return jnp.minimum(x, 0.0) - jnp.log(1.0 + jnp.exp(-jnp.abs(x)))


def _dot(a, b):
    return jnp.dot(a, b, preferred_element_type=F32)


def _dot_nt(a, b):
    return lax.dot_general(a, b, (((1,), (1,)), ((), ())), preferred_element_type=F32)


def _split3(x):
    hi = x.astype(BF16)
    r1 = x - hi.astype(F32)
    mid = r1.astype(BF16)
    lo = (r1 - mid.astype(F32)).astype(BF16)
    return hi, mid, lo


def _dot_exact_left(m_bf16, x):
    return sum(_dot(m_bf16, p) for p in _split3(x))


def _dot_exact_right(x, m_bf16):
    return sum(_dot(p, m_bf16) for p in _split3(x))


def _resident(shape, grid_rank=1):
    zeros = (0,) * len(shape)
    return pl.BlockSpec(shape, lambda *_: zeros, pipeline_mode=pl.Buffered(1))


def _ffn_kernel(*refs, final, fused_res):
    refs = list(refs)
    if fused_res:
        a_ref, wr_ref = refs[:2]
        refs = refs[2:]
    x_ref, g_ref, wg_ref, wu_ref, wd_ref = refs[:5]
    refs = refs[5:]
    if final:
        fg_ref = refs.pop(0)
    (o_ref,) = refs
    x = x_ref[...]
    if fused_res:
        x = x + _dot(a_ref[...], wr_ref[...])
    hn = _rms(x, g_ref[...]).astype(BF16)
    a = _dot(hn, wg_ref[...])
    u = _dot(hn, wu_ref[...])
    act = (a * _sigmoid(a) * u).astype(BF16)
    y = x + 0.5 * _dot(act, wd_ref[...])
    if final:
        y = _rms(y, fg_ref[...])
    o_ref[...] = y


def _ffn(x, g, wg, wu, wd, final_g=None, res_a=None, res_w=None, tm=512):
    n = x.shape[0]
    tm = min(tm, n)
    assert n % tm == 0
    final = final_g is not None
    fused_res = res_a is not None
    row = pl.BlockSpec((tm, D_MODEL), lambda i: (i, 0))
    in_specs, args = [], []
    if fused_res:
        in_specs += [row, _resident((D_MODEL, D_MODEL))]
        args += [res_a, res_w]
    in_specs += [row, _resident((1, D_MODEL)), _resident((D_MODEL, D_FF)), _resident((D_MODEL, D_FF)),
                 _resident((D_FF, D_MODEL))]
    args += [x, g, wg, wu, wd]
    if final:
        in_specs.append(_resident((1, D_MODEL)))
        args.append(final_g)
    return pl.pallas_call(
        functools.partial(_ffn_kernel, final=final, fused_res=fused_res),
        grid=(n // tm,),
        in_specs=in_specs,
        out_specs=row,
        out_shape=jax.ShapeDtypeStruct((n, D_MODEL), F32),
        compiler_params=_cparams(("parallel",)),
        name="ffn_final" if final else "ffn",
    )(*args)


def _normproj_kernel(*refs, side):
    if side:
        x_ref, g_ref, w_ref, ws_ref, o_ref, s_ref, hn_ref = refs
    else:
        x_ref, g_ref, w_ref, o_ref, hn_ref = refs

    @pl.when(pl.program_id(1) == 0)
    def _():
        hn = _rms(x_ref[...], g_ref[...]).astype(BF16)
        hn_ref[...] = hn
        if side:
            s_ref[...] = _dot(hn, ws_ref[...])

    o_ref[...] = _dot(hn_ref[...], w_ref[...]).astype(o_ref.dtype)


def _normproj(x, g, w, out_dtype, tn, w_side=None, tm=512, name="normproj"):
    n = x.shape[0]
    width = w.shape[1]
    tm = min(tm, n)
    assert n % tm == 0 and width % tn == 0
    side = w_side is not None
    in_specs = [
        pl.BlockSpec((tm, D_MODEL), lambda i, j: (i, 0)),
        pl.BlockSpec((1, D_MODEL), lambda i, j: (0, 0)),
        pl.BlockSpec((D_MODEL, tn), lambda i, j: (0, j)),
    ]
    out_specs = [pl.BlockSpec((tm, tn), lambda i, j: (i, j))]
    out_shape = [jax.ShapeDtypeStruct((n, width), out_dtype)]
    args = [x, g, w]
    if side:
        ws = w_side.shape[1]
        in_specs.append(pl.BlockSpec((D_MODEL, ws), lambda i, j: (0, 0)))
        out_specs.append(pl.BlockSpec((tm, ws), lambda i, j: (i, 0)))
        out_shape.append(jax.ShapeDtypeStruct((n, ws), F32))
        args.append(w_side)
    out = pl.pallas_call(
        functools.partial(_normproj_kernel, side=side),
        grid=(n // tm, width // tn),
        in_specs=in_specs,
        out_specs=out_specs,
        out_shape=out_shape,
        scratch_shapes=[pltpu.VMEM((tm, D_MODEL), BF16)],
        compiler_params=_cparams(("parallel", "arbitrary")),
        name=name,
    )(*args)
    return out if side else out[0]


def _merge_kernel(gm_ref, gg_ref, hm_ref, hg_ref, x_ref, wm_ref, wg_ref, wo_ref, gq_ref, wq_ref, o_ref, q_ref):
    ym = _dot(hm_ref[...], wm_ref[...])
    yg = _dot(hg_ref[...], wg_ref[...])
    y = _sigmoid(gm_ref[...].astype(F32)) * ym + _sigmoid(gg_ref[...].astype(F32)) * yg
    x = x_ref[...] + _dot(y.astype(BF16), wo_ref[...])
    o_ref[...] = x
    q_ref[...] = _dot(_rms(x, gq_ref[...]).astype(BF16), wq_ref[...]).astype(q_ref.dtype)


def _merge(z, hm, hg, x, wm, wg, wo, gq, wq, tm=512):
    n = x.shape[0]
    tm = min(tm, n)
    assert n % tm == 0
    row = lambda i: (i, 0)
    return pl.pallas_call(
        _merge_kernel,
        grid=(n // tm,),
        in_specs=[
            pl.BlockSpec((tm, D_MODEL), lambda i: (i, Z_GATE_M // D_MODEL)),
            pl.BlockSpec((tm, D_MODEL), lambda i: (i, Z_GATE_G // D_MODEL)),
            pl.BlockSpec((tm, VW), row),
            pl.BlockSpec((tm, VW), row),
            pl.BlockSpec((tm, D_MODEL), row),
            _resident((VW, D_MODEL)),
            _resident((VW, D_MODEL)),
            _resident((D_MODEL, D_MODEL)),
            _resident((1, D_MODEL)),
            _resident((D_MODEL, D_MODEL)),
        ],
        out_specs=[pl.BlockSpec((tm, D_MODEL), row), pl.BlockSpec((tm, D_MODEL), row)],
        out_shape=[jax.ShapeDtypeStruct((n, D_MODEL), F32), jax.ShapeDtypeStruct((n, D_MODEL), BF16)],
        compiler_params=_cparams(("parallel",)),
        name="merge",
    )(z, z, hm, hg, x, wm, wg, wo, gq, wq)


def _attn_kernel(q_ref, k_ref, v_ref, o_ref):
    scale = C_HD ** -0.5
    for h in range(HEADS):
        cs = slice(h * C_HD, (h + 1) * C_HD)
        qh = q_ref[0, :, cs]
        kh = k_ref[0, :, cs].astype(BF16)
        vh = v_ref[0, :, cs].astype(BF16)
        s = _dot_nt(qh, kh) * scale
        s = s - jnp.max(s, axis=-1, keepdims=True)
        p = jnp.exp(s)
        p = p / jnp.sum(p, axis=-1, keepdims=True)
        o_ref[0, :, cs] = _dot(p.astype(BF16), vh).astype(o_ref.dtype)


def _attn(q, k, v, tq):
    b, t, _ = q.shape
    assert t % tq == 0
    return pl.pallas_call(
        _attn_kernel,
        grid=(b, t // tq),
        in_specs=[
            pl.BlockSpec((1, tq, D_MODEL), lambda i, j: (i, j, 0)),
            pl.BlockSpec((1, N_MEM, D_MODEL), lambda i, j: (i, 0, 0)),
            pl.BlockSpec((1, N_MEM, D_MODEL), lambda i, j: (i, 0, 0)),
        ],
        out_specs=pl.BlockSpec((1, tq, D_MODEL), lambda i, j: (i, j, 0)),
        out_shape=jax.ShapeDtypeStruct((b, t, D_MODEL), BF16),
        compiler_params=_cparams(("parallel", "arbitrary")),
        name="attn",
    )(q, k, v)


CACHE_HALVES = C_HD // LANES
CACHE_ROWS = HEADS * CACHE_HALVES


def _attn_cache_kernel(q_ref, k_ref, v_ref, o_ref, *, t):
    scale = C_HD ** -0.5
    nq = t * HEADS
    lane = lax.broadcasted_iota(jnp.int32, (nq, LANES), 1)
    rowi = lax.broadcasted_iota(jnp.int32, (nq, LANES), 0)
    valid = ((lane % CACHE_ROWS) // HEADS == 0) & (lane % HEADS == rowi % HEADS)
    n_tiles = N_MEM * CACHE_ROWS // LANES
    for bi in range(q_ref.shape[0]):
        kn = k_ref[bi].astype(BF16)
        vn = v_ref[bi].astype(BF16)
        s = _dot_nt(q_ref[bi], kn)
        tiles = []
        for j in range(n_tiles):
            cs = slice(j * LANES, (j + 1) * LANES)
            sj = s[0:nq, cs] + pltpu.roll(s[nq:2 * nq, cs], LANES - HEADS, 1)
            tiles.append(jnp.where(valid, sj * scale, -jnp.inf))
        mx = functools.reduce(jnp.maximum, [jnp.max(x, axis=1, keepdims=True) for x in tiles])
        ps = [jnp.exp(x - mx) for x in tiles]
        den = functools.reduce(jnp.add, [jnp.sum(x, axis=1, keepdims=True) for x in ps])
        inv = 1.0 / den
        p0 = jnp.concatenate([x * inv for x in ps], axis=1)
        p1 = jnp.concatenate([pltpu.roll(x * inv, HEADS, 1) for x in ps], axis=1)
        p = jnp.concatenate([p0, p1], axis=0).astype(BF16)
        o_ref[bi] = _dot(p, vn).astype(o_ref.dtype)


def _attn_cache(q, k, v, t, bb):
    assert CACHE_HALVES == 2
    b = q.shape[0]
    assert b % bb == 0
    nq = t * HEADS
    qv = q[:, :t].reshape(b, t, HEADS, CACHE_HALVES, LANES).transpose(0, 3, 1, 2, 4)
    qv = qv.reshape(b, CACHE_HALVES * nq, LANES)

    def cache_view(a):
        a = a.reshape(b, N_MEM, HEADS, CACHE_HALVES, LANES).transpose(0, 1, 3, 2, 4)
        return a.reshape(b, N_MEM * CACHE_ROWS, LANES)

    o = pl.pallas_call(
        functools.partial(_attn_cache_kernel, t=t),
        grid=(b // bb,),
        in_specs=[
            pl.BlockSpec((bb, CACHE_HALVES * nq, LANES), lambda i: (i, 0, 0)),
            pl.BlockSpec((bb, N_MEM * CACHE_ROWS, LANES), lambda i: (i, 0, 0)),
            pl.BlockSpec((bb, N_MEM * CACHE_ROWS, LANES), lambda i: (i, 0, 0)),
        ],
        out_specs=pl.BlockSpec((bb, CACHE_HALVES * nq, LANES), lambda i: (i, 0, 0)),
        out_shape=jax.ShapeDtypeStruct((b, CACHE_HALVES * nq, LANES), BF16),
        compiler_params=_cparams(("parallel",)),
        name="attn_cache",
    )(qv, cache_view(k), cache_view(v))
    o = o.reshape(b, CACHE_HALVES, t, HEADS, LANES).transpose(0, 2, 3, 1, 4).reshape(b, t, D_MODEL)
    return jnp.pad(o, ((0, 0), (0, q.shape[1] - t), (0, 0)))


def _tile_lanes(rep, width):
    if width <= LANES:
        return rep[:, :width]
    return jnp.concatenate([rep] * (width // LANES), axis=1)


def _mlstm_kernel(q_ref, k_ref, v_ref, og_ref, sm_ref, bias_ref, gn_ref, c0_ref, n0_ref, m0_ref,
                  h_ref, c1_ref, n1_ref, m1_ref, *scratch, chunk, t_valid, carried):
    L = chunk
    bb, tb = q_ref.shape[0], q_ref.shape[1]
    scale = DK ** -0.5

    if carried:
        c_scr, n_scr, m_scr = scratch

        @pl.when(pl.program_id(1) == 0)
        def _():
            c_scr[...] = c0_ref[0]
            n_scr[...] = n0_ref[0]
            m_scr[...] = m0_ref[0]

    row = lax.broadcasted_iota(jnp.int32, (L, L), 0)
    col = lax.broadcasted_iota(jnp.int32, (L, L), 1)
    tri = col <= row
    tri_lo = jnp.where(tri, 1.0, 0.0).astype(BF16)
    lane = lax.broadcasted_iota(jnp.int32, (L, LANES), 1)
    spread = jnp.where(lax.broadcasted_iota(jnp.int32, (LANES, 2 * HEADS * LANES), 0)
                       == lax.broadcasted_iota(jnp.int32, (LANES, 2 * HEADS * LANES), 1) // LANES,
                       1.0, 0.0).astype(BF16)
    pick = jnp.where(lax.broadcasted_iota(jnp.int32, (SUBLANES, LANES), 0)
                     == lax.broadcasted_iota(jnp.int32, (SUBLANES, LANES), 1), 1.0, 0.0).astype(BF16)

    def chunk_group(bis, r):
        rows = pl.ds(r, L)
        cols_of, rows_of = {}, {}
        for bi in bis:
            g = sm_ref[bi, rows, :] + bias_ref[...]
            lf_all = _log_sigmoid(g)
            ig_all = g
            if t_valid < L:
                valid = lax.broadcasted_iota(jnp.int32, (L, 1), 0) < t_valid
                lf_all = jnp.where(valid, lf_all, 0.0)
                ig_all = jnp.where(valid, g, -1e30)
            f_all = _dot_exact_left(tri_lo, lf_all)
            x = jnp.where(lane < SM_F, ig_all, f_all)
            parts = _split3(x)
            cols_of[bi] = sum(_dot(p, spread) for p in parts)
            rows_of[bi] = sum(_dot_nt(pick, p) for p in parts)
        pairs = [(bi, h) for bi in bis for h in range(HEADS)]

        def state(bi):
            if carried:
                return c_scr, n_scr, m_scr, c_scr, n_scr, m_scr
            return (c0_ref.at[bi], n0_ref.at[bi], m0_ref.at[bi], c1_ref.at[bi], n1_ref.at[bi], m1_ref.at[bi])

        st = {}
        for bi, h in pairs:
            c_in, n_in, m_in = state(bi)[:3]
            slab = lambda c, bi=bi: cols_of[bi][:, c * LANES:(c + 1) * LANES]
            f_rep = slab(SM_F + h)
            ig_rep = slab(SM_I + h)
            f_row = rows_of[bi][SM_F + h:SM_F + h + 1, :]
            ig_row = rows_of[bi][SM_I + h:SM_I + h + 1, :]
            dmat = jnp.where(tri, _tile_lanes(f_rep, L) - f_row + ig_row, -jnp.inf)
            m_inter = m_in[h:h + 1, :] + f_rep
            m = jnp.maximum(m_inter, jnp.max(dmat, axis=1, keepdims=True))
            w = jnp.exp(dmat - _tile_lanes(m, L))
            a = jnp.exp(m_inter - m)
            m_end = m[L - 1:L, :]
            w_rep = jnp.exp(f_rep[L - 1:L, :] - f_rep + ig_rep - m_end) * scale
            st[bi, h] = dict(m=m, w=w, a=a, m_end=m_end, w_rep=w_rep, c0=c_in[h], n0=n_in[h:h + 1, :],
                             qb=q_ref[bi, rows, h * DK:(h + 1) * DK], kb=k_ref[bi, rows, h * DK:(h + 1) * DK],
                             vb=v_ref[bi, rows, h * DV:(h + 1) * DV], og=og_ref[bi, rows, h * DV:(h + 1) * DV])
        for key in pairs:
            d = st[key]
            d["s"] = _dot_nt(d["qb"], d["kb"]) * (d["w"] * scale)
        for key in pairs:
            d = st[key]
            d["inter"] = _dot_nt(d["qb"], d["c0"].astype(BF16))
        for key in pairs:
            d = st[key]
            d["sv"] = _dot(d["s"].astype(BF16), d["vb"])
        for key in pairs:
            d = st[key]
            vw_t = (d["vb"].astype(F32) * _tile_lanes(d["w_rep"], DV)).T.astype(BF16)
            d["c_new"] = d["a"][L - 1:L, :] * d["c0"] + _dot(vw_t, d["kb"])
        for bi, h in pairs:
            d = st[bi, h]
            a, m = d["a"], d["m"]
            qf = d["qb"].astype(F32)
            kf = d["kb"].astype(F32)
            num = _tile_lanes(a, DV) * d["inter"] + d["sv"]
            den = a * jnp.sum(qf * d["n0"], axis=1, keepdims=True) + jnp.sum(d["s"], axis=1, keepdims=True)
            hh = num / _tile_lanes(jnp.maximum(jnp.abs(den), jnp.exp(-m)), DV)
            d["n_new"] = a[L - 1:L, :] * d["n0"] + jnp.sum(kf * d["w_rep"], axis=0, keepdims=True)
            hn = hh * lax.rsqrt(jnp.mean(hh * hh, axis=-1, keepdims=True) + EPS)
            hn = hn * gn_ref[:, h * DV:(h + 1) * DV] * _sigmoid(d["og"].astype(F32))
            d["hn"] = hn.astype(h_ref.dtype)
        for bi, h in pairs:
            d = st[bi, h]
            c_out, n_out, m_out = state(bi)[3:]
            c_out[h] = d["c_new"]
            n_out[h:h + 1, :] = d["n_new"]
            m_out[h:h + 1, :] = d["m_end"]
            h_ref[bi, rows, h * DV:(h + 1) * DV] = d["hn"]

    if tb == L:
        chunk_group(list(range(bb)), 0)
    else:
        assert bb == 1

        def loop_body(ci, carry):
            chunk_group([0], pl.multiple_of(ci * L, L))
            return carry

        lax.fori_loop(0, tb // L, loop_body, 0)

    if carried:
        @pl.when(pl.program_id(1) == pl.num_programs(1) - 1)
        def _():
            c1_ref[0] = c_scr[...]
            n1_ref[0] = n_scr[...]
            m1_ref[0] = m_scr[...]


def _mixer_specs(bb, tb):
    bt = lambda col: (lambda i, j: (i, j, col))
    return bt, (lambda i, j: (i, 0, 0, 0)), (lambda i, j: (i, 0, 0))


def _mlstm(z3, sm3, bias_row, gn, c0, n0, m0, chunk, tb, bb, t_valid):
    b, t, _ = z3.shape
    assert t % tb == 0 and tb % chunk == 0 and b % bb == 0
    assert t_valid == t or (tb == t and chunk == t)
    carried = t // tb > 1
    assert not (carried and bb > 1)
    m0 = jnp.broadcast_to(m0[:, :, None], (b, HEADS, LANES))
    bt, st4, st3 = _mixer_specs(bb, tb)
    scratch = [pltpu.VMEM((HEADS, DV, DK), F32), pltpu.VMEM((HEADS, DK), F32),
               pltpu.VMEM((HEADS, LANES), F32)] if carried else []
    h, c1, n1, m1 = pl.pallas_call(
        functools.partial(_mlstm_kernel, chunk=chunk, t_valid=t_valid, carried=carried),
        grid=(b // bb, t // tb),
        in_specs=[
            pl.BlockSpec((bb, tb, QK), bt(Z_MQ // QK)),
            pl.BlockSpec((bb, tb, QK), bt(Z_MK // QK)),
            pl.BlockSpec((bb, tb, VW), bt(Z_MV // VW)),
            pl.BlockSpec((bb, tb, VW), bt(Z_MO // VW)),
            pl.BlockSpec((bb, tb, LANES), bt(0)),
            pl.BlockSpec((1, LANES), lambda i, j: (0, 0)),
            pl.BlockSpec((1, VW), lambda i, j: (0, 0)),
            pl.BlockSpec((bb, HEADS, DV, DK), st4),
            pl.BlockSpec((bb, HEADS, DK), st3),
            pl.BlockSpec((bb, HEADS, LANES), st3),
        ],
        out_specs=[
            pl.BlockSpec((bb, tb, VW), lambda i, j: (i, j, 0)),
            pl.BlockSpec((bb, HEADS, DV, DK), st4),
            pl.BlockSpec((bb, HEADS, DK), st3),
            pl.BlockSpec((bb, HEADS, LANES), st3),
        ],
        out_shape=[
            jax.ShapeDtypeStruct((b, t, VW), BF16),
            jax.ShapeDtypeStruct((b, HEADS, DV, DK), F32),
            jax.ShapeDtypeStruct((b, HEADS, DK), F32),
            jax.ShapeDtypeStruct((b, HEADS, LANES), F32),
        ],
        scratch_shapes=scratch,
        compiler_params=_cparams(("parallel", "arbitrary")),
        name="mlstm",
    )(z3, z3, z3, z3, sm3, bias_row, gn, c0, n0, m0)
    return h, c1, n1, m1[:, :, 0]


def _gla_kernel(q_ref, k_ref, v_ref, r_ref, sm_ref, wa_ref, ba_ref, gn_ref, s0_ref,
                h_ref, s1_ref, *scratch, chunk, sub, t_valid, carried):
    L = chunk
    nb = L // sub
    bb, tb = q_ref.shape[0], q_ref.shape[1]
    scale = DK ** -0.5

    if carried:
        s_scr, b_scr, kf_scr = scratch

        @pl.when(pl.program_id(1) == 0)
        def _():
            s_scr[...] = s0_ref[0]
    else:
        b_scr, kf_scr = scratch

    row = lax.broadcasted_iota(jnp.int32, (L, L), 0)
    col = lax.broadcasted_iota(jnp.int32, (L, L), 1)
    tri_lo = jnp.where(col <= row, 1.0, 0.0).astype(BF16)
    in_blk = (col // sub) == (row // sub)
    trow = lax.broadcasted_iota(jnp.int32, (sub, DK), 0)
    eye = lax.broadcasted_iota(jnp.int32, (DK, DK), 0) == lax.broadcasted_iota(jnp.int32, (DK, DK), 1)
    place = jnp.where(lax.broadcasted_iota(jnp.int32, (sub * DK, LANES), 0) // DK
                      == lax.broadcasted_iota(jnp.int32, (sub * DK, LANES), 1) % sub, 1.0, 0.0).astype(BF16)
    wa_hi = wa_ref[...].astype(BF16)
    wa_lo = (wa_ref[...] - wa_hi.astype(F32)).astype(BF16)

    def stage(bi):
        sm = sm_ref[bi]
        sm_hi = sm.astype(BF16)
        sm_lo = (sm - sm_hi.astype(F32)).astype(BF16)
        a_raw = _dot(sm_hi, wa_hi) + _dot(sm_lo, wa_hi) + _dot(sm_hi, wa_lo) + ba_ref[...]
        la = _log_sigmoid(a_raw) * (math.log2(math.e) / G_TAU)
        if t_valid < L:
            valid = lax.broadcasted_iota(jnp.int32, (L, 1), 0) < t_valid
            la = jnp.where(valid, la, 0.0)
        for c in range(tb // L):
            b_scr[bi, c * L:(c + 1) * L, :] = _dot_exact_left(tri_lo, la[c * L:(c + 1) * L, :])
        kf_scr[bi] = k_ref[bi].astype(F32)

    def chunk_group(bis, r):
        rows = pl.ds(r, L)
        pairs = [(bi, h) for bi in bis for h in range(HEADS)]

        def state(bi):
            return (s_scr, s_scr) if carried else (s0_ref.at[bi], s1_ref.at[bi])

        slabs = []
        for bi, h in pairs:
            ks = slice(h * DK, (h + 1) * DK)
            qf = q_ref[bi, rows, ks].astype(F32) * scale
            for i in range(nb):
                blk = pl.ds(pl.multiple_of(r + i * sub, sub), sub)
                bs = b_scr[bi, blk, ks]
                kblk = kf_scr[bi, blk, ks]
                qs = qf[i * sub:(i + 1) * sub, :]
                row_slabs = []
                for s in range(sub):
                    e = jnp.exp2(jnp.where(trow >= s, bs - bs[s:s + 1, :], -jnp.inf))
                    row_slabs.append(e * qs * kblk[s:s + 1, :])
                slabs.append(jnp.concatenate(row_slabs, axis=1))
        own = _dot(jnp.concatenate(slabs, axis=0).astype(BF16), place)

        st = {}
        for n, (bi, h) in enumerate(pairs):
            ks = slice(h * DK, (h + 1) * DK)
            vs = slice(h * DV, (h + 1) * DV)
            bh = b_scr[bi, rows, ks]
            qf = q_ref[bi, rows, ks].astype(F32) * scale
            kf = kf_scr[bi, rows, ks]
            s0 = state(bi)[0][h]
            d = dict(vb=v_ref[bi, rows, vs], rg=r_ref[bi, rows, vs], s0=s0, s0b=s0.astype(BF16),
                     qhat=(qf * jnp.exp2(bh)).astype(BF16), a_own=own[n * L:(n + 1) * L, 0:L])
            d["qk"] = []
            for i in range(1, nb):
                r0 = i * sub
                b_ref = bh[r0:r0 + 1, :]
                qt = qf[r0:r0 + sub, :] * jnp.exp2(bh[r0:r0 + sub, :] - b_ref)
                kt = kf[0:r0, :] * jnp.exp2(b_ref - bh[0:r0, :])
                kt = jnp.concatenate([kt, jnp.zeros((L - r0, DK), F32)], axis=0)
                d["qk"].append((qt.astype(BF16), kt.astype(BF16)))
            b_end = bh[L - 1:L, :]
            d["e_col"] = jnp.sum(jnp.where(eye, jnp.exp2(b_end), 0.0), axis=1, keepdims=True)
            d["ke_t"] = (kf * jnp.exp2(b_end - bh)).T.astype(BF16)
            st[bi, h] = d
        for key in pairs:
            d = st[key]
            d["o"] = _dot(d["qhat"], d["s0b"])
        for key in pairs:
            d = st[key]
            d["blocks"] = [_dot_nt(qt, kt) for qt, kt in d["qk"]]
        for key in pairs:
            d = st[key]
            d["s_new"] = d["e_col"] * d["s0"] + _dot(d["ke_t"], d["vb"])
        for key in pairs:
            d = st[key]
            if nb > 1:
                below = jnp.concatenate([jnp.zeros((sub, L), F32)] + d["blocks"], axis=0)
                a_intra = jnp.where(in_blk, d["a_own"], below)
                d["o"] = d["o"] + _dot(a_intra.astype(BF16), d["vb"])
            else:
                d["o"] = d["o"] + _dot(d["a_own"], d["vb"].astype(F32))
        for bi, h in pairs:
            d = st[bi, h]
            o = d["o"]
            hn = o * lax.rsqrt(jnp.mean(o * o, axis=-1, keepdims=True) + EPS)
            rg = d["rg"].astype(F32)
            hn = hn * gn_ref[:, h * DV:(h + 1) * DV] * (rg * _sigmoid(rg))
            d["hn"] = hn.astype(h_ref.dtype)
        for bi, h in pairs:
            d = st[bi, h]
            state(bi)[1][h] = d["s_new"]
            h_ref[bi, rows, h * DV:(h + 1) * DV] = d["hn"]

    for bi in range(bb):
        stage(bi)
    if tb == L:
        chunk_group(list(range(bb)), 0)
    else:
        assert bb == 1

        def loop_body(ci, carry):
            chunk_group([0], pl.multiple_of(ci * L, L))
            return carry

        lax.fori_loop(0, tb // L, loop_body, 0, unroll=2)

    if carried:
        @pl.when(pl.program_id(1) == pl.num_programs(1) - 1)
        def _():
            s1_ref[0] = s_scr[...]


def _gla(z3, sm3, wa_pad, ba, gn, s0, chunk, sub, tb, bb, t_valid):
    b, t, _ = z3.shape
    assert t % tb == 0 and tb % chunk == 0 and chunk % sub == 0 and b % bb == 0
    assert t_valid == t or (tb == t and chunk == t)
    carried = t // tb > 1
    assert not (carried and bb > 1)
    bt, st4, _ = _mixer_specs(bb, tb)
    scratch = [pltpu.VMEM((bb, tb, QK), F32), pltpu.VMEM((bb, tb, QK), F32)]
    if carried:
        scratch = [pltpu.VMEM((HEADS, DK, DV), F32)] + scratch
    h, s1 = pl.pallas_call(
        functools.partial(_gla_kernel, chunk=chunk, sub=sub, t_valid=t_valid, carried=carried),
        grid=(b // bb, t // tb),
        in_specs=[
            pl.BlockSpec((bb, tb, QK), bt(Z_GQ // QK)),
            pl.BlockSpec((bb, tb, QK), bt(Z_GK // QK)),
            pl.BlockSpec((bb, tb, VW), bt(Z_GV // VW)),
            pl.BlockSpec((bb, tb, VW), bt(Z_GR // VW)),
            pl.BlockSpec((bb, tb, LANES), bt(0)),
            pl.BlockSpec((LANES, QK), lambda i, j: (0, 0)),
            pl.BlockSpec((1, QK), lambda i, j: (0, 0)),
            pl.BlockSpec((1, VW), lambda i, j: (0, 0)),
            pl.BlockSpec((bb, HEADS, DK, DV), st4),
        ],
        out_specs=[
            pl.BlockSpec((bb, tb, VW), lambda i, j: (i, j, 0)),
            pl.BlockSpec((bb, HEADS, DK, DV), st4),
        ],
        out_shape=[
            jax.ShapeDtypeStruct((b, t, VW), BF16),
            jax.ShapeDtypeStruct((b, HEADS, DK, DV), F32),
        ],
        scratch_shapes=scratch,
        compiler_params=_cparams(("parallel", "arbitrary")),
        name="gla",
    )(z3, z3, z3, z3, sm3, wa_pad, ba, gn, s0)
    return h, s1


def _layer(x3, mem_k, mem_v, c0, n0, m0, s0, p, final_norm, t_valid, m_chunk, g_chunk, g_sub, tb, bb, tq):
    b, t, _ = x3.shape
    n = b * t
    x = x3.reshape(n, D_MODEL)
    x = _ffn(x, p["ffn1_norm"], p["ffn1_wg"], p["ffn1_wu"], p["ffn1_wd"])
    z, sm = _normproj(x, p["mix_norm"], p["w_in"], BF16, tn=Z_WIDTH // 4, w_side=p["w_in_side"], tm=1024,
                      name="inproj")
    z3 = z.reshape(b, t, Z_WIDTH)
    sm3 = sm.reshape(b, t, LANES)
    hm, c1, n1, m1 = _mlstm(z3, sm3, p["bias_row"], p["mlstm_norm"], c0, n0, m0, m_chunk, tb, bb, t_valid)
    hg, s1 = _gla(z3, sm3, p["wa_pad"], p["gla_ba"], p["gla_norm"], s0, g_chunk, g_sub, tb, bb, t_valid)
    x, q = _merge(z, hm.reshape(n, VW), hg.reshape(n, VW), x, p["w_br_m"], p["w_br_g"], p["w_out"],
                  p["ca_norm"], p["ca_wq"])
    if mem_k.ndim == 4:
        o = _attn_cache(q.reshape(b, t, D_MODEL), mem_k, mem_v, t=t_valid, bb=4)
    else:
        o = _attn(q.reshape(b, t, D_MODEL), mem_k, mem_v, tq)
    y = _ffn(x, p["ffn2_norm"], p["ffn2_wg"], p["ffn2_wu"], p["ffn2_wd"], final_g=final_norm,
             res_a=o.reshape(n, D_MODEL), res_w=p["ca_wo"])
    return y.reshape(b, t, D_MODEL), (c1, n1, m1, s1)


def _permute_w_in(w_in):
    sizes = (QK, QK, VW, VW, HEADS, HEADS, QK, QK, VW, VW, G_RANK, D_MODEL, D_MODEL)
    offs = [0]
    for s in sizes:
        offs.append(offs[-1] + s)
    part = lambda i: w_in[:, offs[i]:offs[i + 1]]
    (mq, mk, mv, mo, mi, mf, gq, gk, gv, gr, ga, gate_m, gate_g) = [part(i) for i in range(len(sizes))]
    pad = jnp.zeros((D_MODEL, LANES - 2 * HEADS - G_RANK), w_in.dtype)
    w_main = jnp.concatenate([mq, mk, mv, gq, gk, gv, mo, gr, gate_m, gate_g], axis=1)
    w_side = jnp.concatenate([mi, mf, ga, pad], axis=1)
    return w_main.astype(BF16), w_side.astype(BF16)


def kernel(x_prompt, x_sample, mem_prompt, state_mlstm_C, state_mlstm_n, state_mlstm_m, state_gla_S, cache_mem_k, cache_mem_v, ffn1_norm, ffn1_wg, ffn1_wu, ffn1_wd, mix_norm, w_in, b_if, gla_wa2, gla_ba, mlstm_norm, gla_norm, w_br_m, w_br_g, w_out, ca_norm, mem_norm, ca_wq, ca_wk, ca_wv, ca_wo, ffn2_norm, ffn2_wg, ffn2_wu, ffn2_wd, final_norm):
    depth = ffn1_norm.shape[0]
    assert depth == 1
    l = 0
    bp, tp, _ = x_prompt.shape
    bs, ts, _ = x_sample.shape
    row = lambda v: v.reshape(1, -1).astype(F32)
    bias_row = jnp.zeros((1, LANES), F32).at[0, :2 * HEADS].set(b_if[l])
    wa_pad = jnp.zeros((LANES, QK), F32).at[SM_A:SM_A + G_RANK, :].set(gla_wa2[l])
    w_in_main, w_in_side = _permute_w_in(w_in[l])
    p = {
        "w_in": w_in_main, "w_in_side": w_in_side,
        "ffn1_norm": row(ffn1_norm[l]), "ffn1_wg": ffn1_wg[l].astype(BF16), "ffn1_wu": ffn1_wu[l].astype(BF16),
        "ffn1_wd": ffn1_wd[l].astype(BF16),
        "mix_norm": row(mix_norm[l]),
        "bias_row": bias_row, "wa_pad": wa_pad, "gla_ba": row(gla_ba[l]),
        "mlstm_norm": row(mlstm_norm[l]), "gla_norm": row(gla_norm[l]),
        "w_br_m": w_br_m[l].astype(BF16), "w_br_g": w_br_g[l].astype(BF16), "w_out": w_out[l].astype(BF16),
        "ca_norm": row(ca_norm[l]), "ca_wq": ca_wq[l].astype(BF16), "ca_wo": ca_wo[l].astype(BF16),
        "ffn2_norm": row(ffn2_norm[l]), "ffn2_wg": ffn2_wg[l].astype(BF16), "ffn2_wu": ffn2_wu[l].astype(BF16),
        "ffn2_wd": ffn2_wd[l].astype(BF16),
    }
    fin = row(final_norm)

    mem2 = mem_prompt.reshape(bp * N_MEM, D_MODEL)
    mk_p = _normproj(mem2, row(mem_norm[l]), ca_wk[l].astype(BF16), F32, tn=D_MODEL, name="memk")
    mv_p = _normproj(mem2, row(mem_norm[l]), ca_wv[l].astype(BF16), F32, tn=D_MODEL, name="memv")
    mk_p = mk_p.reshape(bp, N_MEM, D_MODEL)
    mv_p = mv_p.reshape(bp, N_MEM, D_MODEL)
    zc = jnp.zeros((bp, HEADS, DV, DK), F32)
    zn = jnp.zeros((bp, HEADS, DK), F32)
    zm = jnp.zeros((bp, HEADS), F32)
    zs = jnp.zeros((bp, HEADS, DK, DV), F32)
    yp, (cp, np_, mp, sp) = _layer(x_prompt, mk_p, mv_p, zc, zn, zm, zs, p, fin, t_valid=tp,
                                   m_chunk=256, g_chunk=64, g_sub=8, tb=512, bb=1, tq=512)

    tpad = -(-ts // SUBLANES) * SUBLANES
    xs = jnp.pad(x_sample, ((0, 0), (0, tpad - ts), (0, 0)))
    ys, (cs, ns, ms, ss) = _layer(xs, cache_mem_k[l], cache_mem_v[l],
                                  state_mlstm_C[l], state_mlstm_n[l], state_mlstm_m[l], state_gla_S[l],
                                  p, fin, t_valid=ts, m_chunk=tpad, g_chunk=tpad, g_sub=tpad, tb=tpad, bb=4,
                                  tq=tpad)
    ys = ys[:, :ts]

    st = lambda a: a[None]
    return (yp, ys, st(cp), st(np_), st(mp), st(sp),
            st(mk_p.reshape(bp, N_MEM, HEADS, C_HD)), st(mv_p.reshape(bp, N_MEM, HEADS, C_HD)),
            st(cs), st(ns), st(ms), st(ss))
```

```python
import functools
import math

import jax
import jax.numpy as jnp
from jax import lax
from jax.experimental import pallas as pl
from jax.experimental.pallas import tpu as pltpu

F32 = jnp.float32
BF16 = jnp.bfloat16

D_MODEL = 1024
D_FF = 2816
HEADS = 4
DK = 128
DV = 256
QK = HEADS * DK
VW = HEADS * DV
G_RANK = 16
G_TAU = 16.0
N_MEM = 256
C_HD = D_MODEL // HEADS
EPS = 1e-6
LOG2E = math.log2(math.e)
LANES = 128
SUBLANES = 8

ZG_Q = 0
ZG_K = ZG_Q + QK
ZG_V = ZG_K + QK
ZG_R = ZG_V + VW
ZG_WIDTH = ZG_R + VW
N_PROJ = 3
N_PIECES = 8
PIECE_W = (2 * VW // N_PIECES, 2 * VW // N_PIECES, D_MODEL // N_PIECES)
P_MQ, P_MK, P_MV = 0, QK // PIECE_W[0], 2 * QK // PIECE_W[0]
P_MO, P_GATE_M = 0, VW // PIECE_W[1]
SM_I = 0
SM_F = HEADS
SM_A = 2 * HEADS

VMEM_LIMIT = 56 * 1024 * 1024


def _cparams(sem):
    return pltpu.CompilerParams(dimension_semantics=sem, vmem_limit_bytes=VMEM_LIMIT)


def _rms(x, g):
    return x * lax.rsqrt(jnp.mean(x * x, axis=-1, keepdims=True) + EPS) * g


def _sigmoid(x):
    return 1.0 / (1.0 + jnp.exp(-x))


def _log_sigmoid(x):
    return jnp.minimum(x, 0.0) - jnp.log(1.0 + jnp.exp(-jnp.abs(x)))


def _dot(a, b):
    return jnp.dot(a, b, preferred_element_type=F32)


def _dot_nt(a, b):
    return lax.dot_general(a, b, (((1,), (1,)), ((), ())), preferred_element_type=F32)


def _split3(x):
    hi = x.astype(BF16)
    r1 = x - hi.astype(F32)
    mid = r1.astype(BF16)
    lo = (r1 - mid.astype(F32)).astype(BF16)
    return hi, mid, lo


def _dot_exact_left(m_bf16, x):
    return sum(_dot(m_bf16, p) for p in _split3(x))


def _dot_exact_right(x, m_bf16):
    return sum(_dot(p, m_bf16) for p in _split3(x))


def _resident(shape, grid_rank=1):
    zeros = (0,) * len(shape)
    return pl.BlockSpec(shape, lambda *_: zeros, pipeline_mode=pl.Buffered(1))


def _ffn_kernel(*refs, final, fused_res):
    refs = list(refs)
    if fused_res:
        a_ref, wr_ref = refs[:2]
        refs = refs[2:]
    x_ref, g_ref, wg_ref, wu_ref, wd_ref = refs[:5]
    refs = refs[5:]
    if final:
        fg_ref = refs.pop(0)
    (o_ref,) = refs
    x = x_ref[...]
    if fused_res:
        x = x + _dot(a_ref[...], wr_ref[...])
    hn = _rms(x, g_ref[...]).astype(BF16)
    a = _dot(hn, wg_ref[...])
    u = _dot(hn, wu_ref[...])
    act = (a * _sigmoid(a) * u).astype(BF16)
    y = x + 0.5 * _dot(act, wd_ref[...])
    if final:
        y = _rms(y, fg_ref[...])
    o_ref[...] = y


def _ffn(x, g, wg, wu, wd, final_g=None, res_a=None, res_w=None, tm=512):
    n = x.shape[0]
    tm = min(tm, n)
    assert n % tm == 0
    final = final_g is not None
    fused_res = res_a is not None
    row = pl.BlockSpec((tm, D_MODEL), lambda i: (i, 0))
    in_specs, args = [], []
    if fused_res:
        in_specs += [row, _resident((D_MODEL, D_MODEL))]
        args += [res_a, res_w]
    in_specs += [row, _resident((1, D_MODEL)), _resident((D_MODEL, D_FF)), _resident((D_MODEL, D_FF)),
                 _resident((D_FF, D_MODEL))]
    args += [x, g, wg, wu, wd]
    if final:
        in_specs.append(_resident((1, D_MODEL)))
        args.append(final_g)
    return pl.pallas_call(
        functools.partial(_ffn_kernel, final=final, fused_res=fused_res),
        grid=(n // tm,),
        in_specs=in_specs,
        out_specs=row,
        out_shape=jax.ShapeDtypeStruct((n, D_MODEL), F32),
        compiler_params=_cparams(("parallel",)),
        name="ffn_final" if final else "ffn",
    )(*args)


def _normproj_kernel(*refs, side):
    if side:
        x_ref, g_ref, w_ref, ws_ref, o_ref, s_ref, hn_ref = refs
    else:
        x_ref, g_ref, w_ref, o_ref, hn_ref = refs

    @pl.when(pl.program_id(1) == 0)
    def _():
        hn = _rms(x_ref[...], g_ref[...]).astype(BF16)
        hn_ref[...] = hn
        if side:
            s_ref[...] = _dot(hn, ws_ref[...])

    o_ref[...] = _dot(hn_ref[...], w_ref[...]).astype(o_ref.dtype)


def _normproj(x, g, w, out_dtype, tn, w_side=None, tm=512, name="normproj"):
    n = x.shape[0]
    width = w.shape[1]
    tm = min(tm, n)
    assert n % tm == 0 and width % tn == 0
    side = w_side is not None
    in_specs = [
        pl.BlockSpec((tm, D_MODEL), lambda i, j: (i, 0)),
        pl.BlockSpec((1, D_MODEL), lambda i, j: (0, 0)),
        pl.BlockSpec((D_MODEL, tn), lambda i, j: (0, j)),
    ]
    out_specs = [pl.BlockSpec((tm, tn), lambda i, j: (i, j))]
    out_shape = [jax.ShapeDtypeStruct((n, width), out_dtype)]
    args = [x, g, w]
    if side:
        ws = w_side.shape[1]
        in_specs.append(pl.BlockSpec((D_MODEL, ws), lambda i, j: (0, 0)))
        out_specs.append(pl.BlockSpec((tm, ws), lambda i, j: (i, 0)))
        out_shape.append(jax.ShapeDtypeStruct((n, ws), F32))
        args.append(w_side)
    out = pl.pallas_call(
        functools.partial(_normproj_kernel, side=side),
        grid=(n // tm, width // tn),
        in_specs=in_specs,
        out_specs=out_specs,
        out_shape=out_shape,
        scratch_shapes=[pltpu.VMEM((tm, D_MODEL), BF16)],
        compiler_params=_cparams(("parallel", "arbitrary")),
        name=name,
    )(*args)
    return out if side else out[0]


def _merge_kernel(gm_ref, gg_ref, hm_ref, hg_ref, x_ref, wm_ref, wg_ref, wo_ref, gq_ref, wq_ref, o_ref, q_ref):
    ym = _dot(hm_ref[...], wm_ref[...])
    yg = _dot(hg_ref[...], wg_ref[...])
    gm = jnp.concatenate([gm_ref[p] for p in range(gm_ref.shape[0])], axis=1).astype(F32)
    gg = jnp.concatenate([gg_ref[p] for p in range(gg_ref.shape[0])], axis=1).astype(F32)
    y = _sigmoid(gm) * ym + _sigmoid(gg) * yg
    x = x_ref[...] + _dot(y.astype(BF16), wo_ref[...])
    o_ref[...] = x
    q_ref[...] = _dot(_rms(x, gq_ref[...]).astype(BF16), wq_ref[...]).astype(q_ref.dtype)


def _merge(p_gm, p_gg, hm, hg, x, wm, wg, wo, gq, wq, tm=512):
    n = x.shape[0]
    tm = min(tm, n)
    assert n % tm == 0
    row = lambda i: (i, 0)
    n_gm = D_MODEL // p_gm.shape[2]
    return pl.pallas_call(
        _merge_kernel,
        grid=(n // tm,),
        in_specs=[
            pl.BlockSpec((n_gm, tm, p_gm.shape[2]), lambda i: (P_GATE_M // n_gm, i, 0)),
            pl.BlockSpec((p_gg.shape[0], tm, p_gg.shape[2]), lambda i: (0, i, 0)),
            pl.BlockSpec((tm, VW), row),
            pl.BlockSpec((tm, VW), row),
            pl.BlockSpec((tm, D_MODEL), row),
            _resident((VW, D_MODEL)),
            _resident((VW, D_MODEL)),
            _resident((D_MODEL, D_MODEL)),
            _resident((1, D_MODEL)),
            _resident((D_MODEL, D_MODEL)),
        ],
        out_specs=[pl.BlockSpec((tm, D_MODEL), row), pl.BlockSpec((tm, D_MODEL), row)],
        out_shape=[jax.ShapeDtypeStruct((n, D_MODEL), F32), jax.ShapeDtypeStruct((n, D_MODEL), BF16)],
        compiler_params=_cparams(("parallel",)),
        name="merge",
    )(p_gm, p_gg, hm, hg, x, wm, wg, wo, gq, wq)


def _attn_kernel(q_ref, k_ref, v_ref, o_ref):
    scale = C_HD ** -0.5
    for h in range(HEADS):
        cs = slice(h * C_HD, (h + 1) * C_HD)
        qh = q_ref[0, :, cs]
        kh = k_ref[0, :, cs].astype(BF16)
        vh = v_ref[0, :, cs].astype(BF16)
        s = _dot_nt(qh, kh) * scale
        s = s - jnp.max(s, axis=-1, keepdims=True)
        p = jnp.exp(s)
        p = p / jnp.sum(p, axis=-1, keepdims=True)
        o_ref[0, :, cs] = _dot(p.astype(BF16), vh).astype(o_ref.dtype)


def _attn(q, k, v, tq):
    b, t, _ = q.shape
    assert t % tq == 0
    return pl.pallas_call(
        _attn_kernel,
        grid=(b, t // tq),
        in_specs=[
            pl.BlockSpec((1, tq, D_MODEL), lambda i, j: (i, j, 0)),
            pl.BlockSpec((1, N_MEM, D_MODEL), lambda i, j: (i, 0, 0)),
            pl.BlockSpec((1, N_MEM, D_MODEL), lambda i, j: (i, 0, 0)),
        ],
        out_specs=pl.BlockSpec((1, tq, D_MODEL), lambda i, j: (i, j, 0)),
        out_shape=jax.ShapeDtypeStruct((b, t, D_MODEL), BF16),
        compiler_params=_cparams(("parallel", "arbitrary")),
        name="attn",
    )(q, k, v)


CACHE_HALVES = C_HD // LANES
CACHE_ROWS = HEADS * CACHE_HALVES


def _attn_cache_kernel(q_ref, k_ref, v_ref, o_ref, *, t):
    scale = C_HD ** -0.5
    nq = t * HEADS
    lane = lax.broadcasted_iota(jnp.int32, (nq, LANES), 1)
    rowi = lax.broadcasted_iota(jnp.int32, (nq, LANES), 0)
    valid = ((lane % CACHE_ROWS) // HEADS == 0) & (lane % HEADS == rowi % HEADS)
    n_tiles = N_MEM * CACHE_ROWS // LANES
    for bi in range(q_ref.shape[0]):
        kn = k_ref[bi].astype(BF16)
        vn = v_ref[bi].astype(BF16)
        s = _dot_nt(q_ref[bi], kn)
        tiles = []
        for j in range(n_tiles):
            cs = slice(j * LANES, (j + 1) * LANES)
            sj = s[0:nq, cs] + pltpu.roll(s[nq:2 * nq, cs], LANES - HEADS, 1)
            tiles.append(jnp.where(valid, sj * scale, -jnp.inf))
        mx = functools.reduce(jnp.maximum, [jnp.max(x, axis=1, keepdims=True) for x in tiles])
        ps = [jnp.exp(x - mx) for x in tiles]
        den = functools.reduce(jnp.add, [jnp.sum(x, axis=1, keepdims=True) for x in ps])
        inv = 1.0 / den
        p0 = jnp.concatenate([x * inv for x in ps], axis=1)
        p1 = jnp.concatenate([pltpu.roll(x * inv, HEADS, 1) for x in ps], axis=1)
        p = jnp.concatenate([p0, p1], axis=0).astype(BF16)
        o_ref[bi] = _dot(p, vn).astype(o_ref.dtype)


def _attn_cache(q, k, v, t, bb):
    assert CACHE_HALVES == 2
    b = q.shape[0]
    assert b % bb == 0
    nq = t * HEADS
    qv = q[:, :t].reshape(b, t, HEADS, CACHE_HALVES, LANES).transpose(0, 3, 1, 2, 4)
    qv = qv.reshape(b, CACHE_HALVES * nq, LANES)

    def cache_view(a):
        a = a.reshape(b, N_MEM, HEADS, CACHE_HALVES, LANES).transpose(0, 1, 3, 2, 4)
        return a.reshape(b, N_MEM * CACHE_ROWS, LANES)

    o = pl.pallas_call(
        functools.partial(_attn_cache_kernel, t=t),
        grid=(b // bb,),
        in_specs=[
            pl.BlockSpec((bb, CACHE_HALVES * nq, LANES), lambda i: (i, 0, 0)),
            pl.BlockSpec((bb, N_MEM * CACHE_ROWS, LANES), lambda i: (i, 0, 0)),
            pl.BlockSpec((bb, N_MEM * CACHE_ROWS, LANES), lambda i: (i, 0, 0)),
        ],
        out_specs=pl.BlockSpec((bb, CACHE_HALVES * nq, LANES), lambda i: (i, 0, 0)),
        out_shape=jax.ShapeDtypeStruct((b, CACHE_HALVES * nq, LANES), BF16),
        compiler_params=_cparams(("parallel",)),
        name="attn_cache",
    )(qv, cache_view(k), cache_view(v))
    o = o.reshape(b, CACHE_HALVES, t, HEADS, LANES).transpose(0, 2, 3, 1, 4).reshape(b, t, D_MODEL)
    return jnp.pad(o, ((0, 0), (0, q.shape[1] - t), (0, 0)))


def _tile_lanes(rep, width):
    if width <= LANES:
        return rep[:, :width]
    return jnp.concatenate([rep] * (width // LANES), axis=1)


def _mlstm_kernel(q_ref, k_ref, v_ref, og_ref, sm_ref, bias_ref, gn_ref, c0_ref, n0_ref, m0_ref,
                  h_ref, c1_ref, n1_ref, m1_ref, *scratch, chunk, t_valid, carried):
    L = chunk
    bb, tb = q_ref.shape[1], q_ref.shape[2]
    scale = DK ** -0.5
    pw = q_ref.shape[3]
    qk_per = pw // DK
    assert pw == DV and pw % DK == 0

    if carried:
        c_scr, n_scr, m_scr = scratch

        @pl.when(pl.program_id(1) == 0)
        def _():
            c_scr[...] = c0_ref[0]
            n_scr[...] = n0_ref[0]
            m_scr[...] = m0_ref[0]

    row = lax.broadcasted_iota(jnp.int32, (L, L), 0)
    col = lax.broadcasted_iota(jnp.int32, (L, L), 1)
    tri = col <= row
    tri_lo = jnp.where(tri, 1.0, 0.0).astype(BF16)
    lane = lax.broadcasted_iota(jnp.int32, (L, LANES), 1)
    spread = jnp.where(lax.broadcasted_iota(jnp.int32, (LANES, 2 * HEADS * LANES), 0)
                       == lax.broadcasted_iota(jnp.int32, (LANES, 2 * HEADS * LANES), 1) // LANES,
                       1.0, 0.0).astype(BF16)
    pick = jnp.where(lax.broadcasted_iota(jnp.int32, (SUBLANES, LANES), 0)
                     == lax.broadcasted_iota(jnp.int32, (SUBLANES, LANES), 1), 1.0, 0.0).astype(BF16)

    def chunk_group(bis, r):
        rows = pl.ds(r, L)
        cols_of, rows_of = {}, {}
        for bi in bis:
            g = sm_ref[bi, rows, :] + bias_ref[...]
            lf_all = _log_sigmoid(g)
            ig_all = g
            if t_valid < L:
                valid = lax.broadcasted_iota(jnp.int32, (L, 1), 0) < t_valid
                lf_all = jnp.where(valid, lf_all, 0.0)
                ig_all = jnp.where(valid, g, -1e30)
            f_all = _dot_exact_left(tri_lo, lf_all)
            x = jnp.where(lane < SM_F, ig_all, f_all) * LOG2E
            parts = _split3(x)
            cols_of[bi] = sum(_dot(p, spread) for p in parts)
            rows_of[bi] = sum(_dot_nt(pick, p) for p in parts)
        pairs = [(bi, h) for bi in bis for h in range(HEADS)]

        def state(bi):
            if carried:
                return c_scr, n_scr, m_scr, c_scr, n_scr, m_scr
            return (c0_ref.at[bi], n0_ref.at[bi], m0_ref.at[bi], c1_ref.at[bi], n1_ref.at[bi], m1_ref.at[bi])

        st = {}
        for bi, h in pairs:
            c_in, n_in, m_in = state(bi)[:3]
            slab = lambda c, bi=bi: cols_of[bi][:, c * LANES:(c + 1) * LANES]
            f_rep = slab(SM_F + h)
            ig_rep = slab(SM_I + h)
            f_row = rows_of[bi][SM_F + h:SM_F + h + 1, :]
            ig_row = rows_of[bi][SM_I + h:SM_I + h + 1, :]
            dmat = jnp.where(tri, _tile_lanes(f_rep, L) - f_row + ig_row, -jnp.inf)
            m_inter = m_in[h:h + 1, :] * LOG2E + f_rep
            m = jnp.maximum(m_inter, jnp.max(dmat, axis=1, keepdims=True))
            w = jnp.exp2(dmat - _tile_lanes(m - math.log2(scale), L))
            a = jnp.exp2(m_inter - m)
            m_end = m[L - 1:L, :]
            w_rep = jnp.exp2(f_rep[L - 1:L, :] - f_rep + ig_rep - (m_end - math.log2(scale)))
            st[bi, h] = dict(m=m, w=w, a=a, m_end=m_end, w_rep=w_rep, c0=c_in[h], n0=n_in[h:h + 1, :],
                             qb=q_ref[h // qk_per, bi, rows, (h % qk_per) * DK:(h % qk_per + 1) * DK],
                             kb=k_ref[h // qk_per, bi, rows, (h % qk_per) * DK:(h % qk_per + 1) * DK],
                             vb=v_ref[h, bi, rows, :], og=og_ref[h, bi, rows, :])
        for key in pairs:
            d = st[key]
            d["s"] = _dot_nt(d["qb"], d["kb"]) * d["w"]
        for key in pairs:
            d = st[key]
            d["inter"] = _dot_nt(d["qb"], d["c0"].astype(BF16))
        for key in pairs:
            d = st[key]
            d["sv"] = _dot(d["s"].astype(BF16), d["vb"])
        for key in pairs:
            d = st[key]
            vw_t = (d["vb"].astype(F32) * _tile_lanes(d["w_rep"], DV)).T.astype(BF16)
            d["c_new"] = d["a"][L - 1:L, :] * d["c0"] + _dot(vw_t, d["kb"])
        for bi, h in pairs:
            d = st[bi, h]
            a, m = d["a"], d["m"]
            qf = d["qb"].astype(F32)
            kf = d["kb"].astype(F32)
            num = _tile_lanes(a, DV) * d["inter"] + d["sv"]
            den = a * jnp.sum(qf * d["n0"], axis=1, keepdims=True) + jnp.sum(d["s"], axis=1, keepdims=True)
            hh = num / _tile_lanes(jnp.maximum(jnp.abs(den), jnp.exp2(-m)), DV)
            d["n_new"] = a[L - 1:L, :] * d["n0"] + jnp.sum(kf * d["w_rep"], axis=0, keepdims=True)
            hn = hh * lax.rsqrt(jnp.mean(hh * hh, axis=-1, keepdims=True) + EPS)
            hn = hn * gn_ref[:, h * DV:(h + 1) * DV] * _sigmoid(d["og"].astype(F32))
            d["hn"] = hn.astype(h_ref.dtype)
        for bi, h in pairs:
            d = st[bi, h]
            c_out, n_out, m_out = state(bi)[3:]
            c_out[h] = d["c_new"]
            n_out[h:h + 1, :] = d["n_new"]
            m_out[h:h + 1, :] = d["m_end"] * (1.0 / LOG2E)
            h_ref[bi, rows, h * DV:(h + 1) * DV] = d["hn"]

    if tb == L:
        chunk_group(list(range(bb)), 0)
    else:
        assert bb == 1

        def loop_body(ci, carry):
            chunk_group([0], pl.multiple_of(ci * L, L))
            return carry

        lax.fori_loop(0, tb // L, loop_body, 0, unroll=2)

    if carried:
        @pl.when(pl.program_id(1) == pl.num_programs(1) - 1)
        def _():
            c1_ref[0] = c_scr[...]
            n1_ref[0] = n_scr[...]
            m1_ref[0] = m_scr[...]


def _mixer_specs(bb, tb):
    bt = lambda col: (lambda i, j: (i, j, col))
    return bt, (lambda i, j: (i, 0, 0, 0)), (lambda i, j: (i, 0, 0))


def _mlstm(p_qkv, p_og, sm3, bias_row, gn, c0, n0, m0, chunk, tb, bb, t_valid):
    _, b, t, pw = p_qkv.shape
    pc = lambda first, n: pl.BlockSpec((n, bb, tb, pw), lambda i, j: (first // n, i, j, 0))
    assert t % tb == 0 and tb % chunk == 0 and b % bb == 0
    assert t_valid == t or (tb == t and chunk == t)
    carried = t // tb > 1
    assert not (carried and bb > 1)
    m0 = jnp.broadcast_to(m0[:, :, None], (b, HEADS, LANES))
    bt, st4, st3 = _mixer_specs(bb, tb)
    scratch = [pltpu.VMEM((HEADS, DV, DK), F32), pltpu.VMEM((HEADS, DK), F32),
               pltpu.VMEM((HEADS, LANES), F32)] if carried else []
    h, c1, n1, m1 = pl.pallas_call(
        functools.partial(_mlstm_kernel, chunk=chunk, t_valid=t_valid, carried=carried),
        grid=(b // bb, t // tb),
        in_specs=[
            pc(P_MQ, QK // pw),
            pc(P_MK, QK // pw),
            pc(P_MV, VW // pw),
            pc(P_MO, VW // pw),
            pl.BlockSpec((bb, tb, LANES), bt(0)),
            pl.BlockSpec((1, LANES), lambda i, j: (0, 0)),
            pl.BlockSpec((1, VW), lambda i, j: (0, 0)),
            pl.BlockSpec((bb, HEADS, DV, DK), st4),
            pl.BlockSpec((bb, HEADS, DK), st3),
            pl.BlockSpec((bb, HEADS, LANES), st3),
        ],
        out_specs=[
            pl.BlockSpec((bb, tb, VW), lambda i, j: (i, j, 0)),
            pl.BlockSpec((bb, HEADS, DV, DK), st4),
            pl.BlockSpec((bb, HEADS, DK), st3),
            pl.BlockSpec((bb, HEADS, LANES), st3),
        ],
        out_shape=[
            jax.ShapeDtypeStruct((b, t, VW), BF16),
            jax.ShapeDtypeStruct((b, HEADS, DV, DK), F32),
            jax.ShapeDtypeStruct((b, HEADS, DK), F32),
            jax.ShapeDtypeStruct((b, HEADS, LANES), F32),
        ],
        scratch_shapes=scratch,
        compiler_params=_cparams(("parallel", "arbitrary")),
        name="mlstm",
    )(p_qkv, p_qkv, p_qkv, p_og, sm3, bias_row, gn, c0, n0, m0)
    return h, c1, n1, m1[:, :, 0]


def _gla_kernel(*refs, chunk, sub, t_valid, carried, proj):
    refs = list(refs)
    q_ref, k_ref, v_ref, r_ref, sm_ref, wa_ref, ba_ref, gn_ref, s0_ref = refs[:9]
    refs = refs[9:]
    if proj:
        x_ref, gx_ref = refs[:2]
        pw_refs = refs[2:2 + N_PROJ]
        refs = refs[2 + N_PROJ:]
    h_ref, s1_ref = refs[:2]
    refs = refs[2:]
    if proj:
        po_refs = refs[:N_PROJ]
        refs = refs[N_PROJ:]
    if carried:
        s_scr = refs.pop(0)
    b_scr, kf_scr = refs[:2]
    if proj:
        hn_scr = refs[2]
    L = chunk
    nb = L // sub
    bb, tb = q_ref.shape[0], q_ref.shape[1]
    scale = DK ** -0.5

    if carried:
        @pl.when(pl.program_id(1) == 0)
        def _():
            s_scr[...] = s0_ref[0]
    if proj:
        assert tb // L == pw_refs[0].shape[0]
        hn_scr[...] = _rms(x_ref[0], gx_ref[...]).astype(BF16)

    row = lax.broadcasted_iota(jnp.int32, (L, L), 0)
    col = lax.broadcasted_iota(jnp.int32, (L, L), 1)
    tri_lo = jnp.where(col <= row, 1.0, 0.0).astype(BF16)
    in_blk = (col // sub) == (row // sub)
    trow = lax.broadcasted_iota(jnp.int32, (sub, DK), 0)
    eye = lax.broadcasted_iota(jnp.int32, (DK, DK), 0) == lax.broadcasted_iota(jnp.int32, (DK, DK), 1)
    place = jnp.where(lax.broadcasted_iota(jnp.int32, (sub * DK, LANES), 0) // DK
                      == lax.broadcasted_iota(jnp.int32, (sub * DK, LANES), 1) % sub, 1.0, 0.0).astype(BF16)
    wa_hi = wa_ref[...].astype(BF16)
    wa_lo = (wa_ref[...] - wa_hi.astype(F32)).astype(BF16)

    def stage(bi):
        sm = sm_ref[bi]
        sm_hi = sm.astype(BF16)
        sm_lo = (sm - sm_hi.astype(F32)).astype(BF16)
        a_raw = _dot(sm_hi, wa_hi) + _dot(sm_lo, wa_hi) + _dot(sm_hi, wa_lo) + ba_ref[...]
        la = _log_sigmoid(a_raw) * (math.log2(math.e) / G_TAU)
        if t_valid < L:
            valid = lax.broadcasted_iota(jnp.int32, (L, 1), 0) < t_valid
            la = jnp.where(valid, la, 0.0)
        for c in range(tb // L):
            b_scr[bi, c * L:(c + 1) * L, :] = _dot_exact_left(tri_lo, la[c * L:(c + 1) * L, :])
        kf_scr[bi] = k_ref[bi].astype(F32)

    def chunk_group(bis, r, ci=None):
        rows = pl.ds(r, L)
        pairs = [(bi, h) for bi in bis for h in range(HEADS)]

        def state(bi):
            return (s_scr, s_scr) if carried else (s0_ref.at[bi], s1_ref.at[bi])

        def project(n):
            if proj:
                po_refs[n][ci, 0] = _dot(hn_scr[...], pw_refs[n][ci]).astype(po_refs[n].dtype)

        project(0)
        slabs = []
        for bi, h in pairs:
            ks = slice(h * DK, (h + 1) * DK)
            qf = q_ref[bi, rows, ks].astype(F32) * scale
            for i in range(nb):
                blk = pl.ds(pl.multiple_of(r + i * sub, sub), sub)
                bs = b_scr[bi, blk, ks]
                kblk = kf_scr[bi, blk, ks]
                qs = qf[i * sub:(i + 1) * sub, :]
                row_slabs = []
                for s in range(sub):
                    e = jnp.exp2(jnp.where(trow >= s, bs - bs[s:s + 1, :], -jnp.inf))
                    row_slabs.append(e * qs * kblk[s:s + 1, :])
                slabs.append(jnp.concatenate(row_slabs, axis=1))
        own = _dot(jnp.concatenate(slabs, axis=0).astype(BF16), place)
        project(1)

        st = {}
        for n, (bi, h) in enumerate(pairs):
            ks = slice(h * DK, (h + 1) * DK)
            vs = slice(h * DV, (h + 1) * DV)
            bh = b_scr[bi, rows, ks]
            qf = q_ref[bi, rows, ks].astype(F32) * scale
            kf = kf_scr[bi, rows, ks]
            s0 = state(bi)[0][h]
            d = dict(vb=v_ref[bi, rows, vs], rg=r_ref[bi, rows, vs], s0=s0, s0b=s0.astype(BF16),
                     qhat=(qf * jnp.exp2(bh)).astype(BF16), a_own=own[n * L:(n + 1) * L, 0:L])
            d["qk"] = []
            for i in range(1, nb):
                r0 = i * sub
                b_ref = bh[r0:r0 + 1, :]
                qt = qf[r0:r0 + sub, :] * jnp.exp2(bh[r0:r0 + sub, :] - b_ref)
                kt = kf[0:r0, :] * jnp.exp2(b_ref - bh[0:r0, :])
                kt = jnp.concatenate([kt, jnp.zeros((L - r0, DK), F32)], axis=0)
                d["qk"].append((qt.astype(BF16), kt.astype(BF16)))
            b_end = bh[L - 1:L, :]
            d["e_col"] = jnp.sum(jnp.where(eye, jnp.exp2(b_end), 0.0), axis=1, keepdims=True)
            d["ke_t"] = (kf * jnp.exp2(b_end - bh)).T.astype(BF16)
            st[bi, h] = d
        for key in pairs:
            d = st[key]
            d["o"] = _dot(d["qhat"], d["s0b"])
        project(2)
        for key in pairs:
            d = st[key]
            d["blocks"] = [_dot_nt(qt, kt) for qt, kt in d["qk"]]
        for key in pairs:
            d = st[key]
            d["s_new"] = d["e_col"] * d["s0"] + _dot(d["ke_t"], d["vb"])
        for key in pairs:
            d = st[key]
            if nb > 1:
                below = jnp.concatenate([jnp.zeros((sub, L), F32)] + d["blocks"], axis=0)
                a_intra = jnp.where(in_blk, d["a_own"], below)
                d["o"] = d["o"] + _dot(a_intra.astype(BF16), d["vb"])
            else:
                d["o"] = d["o"] + _dot(d["a_own"], d["vb"].astype(F32))
        for bi, h in pairs:
            d = st[bi, h]
            o = d["o"]
            hn = o * lax.rsqrt(jnp.mean(o * o, axis=-1, keepdims=True) + EPS)
            rg = d["rg"].astype(F32)
            hn = hn * gn_ref[:, h * DV:(h + 1) * DV] * (rg * _sigmoid(rg))
            d["hn"] = hn.astype(h_ref.dtype)
        for bi, h in pairs:
            d = st[bi, h]
            state(bi)[1][h] = d["s_new"]
            h_ref[bi, rows, h * DV:(h + 1) * DV] = d["hn"]

    for bi in range(bb):
        stage(bi)
    if tb == L:
        chunk_group(list(range(bb)), 0)
    else:
        assert bb == 1

        def loop_body(ci, carry):
            chunk_group([0], pl.multiple_of(ci * L, L), ci)
            return carry

        lax.fori_loop(0, tb // L, loop_body, 0, unroll=2)

    if carried:
        @pl.when(pl.program_id(1) == pl.num_programs(1) - 1)
        def _():
            s1_ref[0] = s_scr[...]


def _gla(zg3, sm3, wa_pad, ba, gn, s0, chunk, sub, tb, bb, t_valid, proj=None):
    b, t, _ = zg3.shape
    assert t % tb == 0 and tb % chunk == 0 and chunk % sub == 0 and b % bb == 0
    assert t_valid == t or (tb == t and chunk == t)
    carried = t // tb > 1
    assert not (carried and bb > 1)
    bt, st4, _ = _mixer_specs(bb, tb)
    scratch = [pltpu.VMEM((bb, tb, QK), F32), pltpu.VMEM((bb, tb, QK), F32)]
    if carried:
        scratch = [pltpu.VMEM((HEADS, DK, DV), F32)] + scratch
    in_specs = [
        pl.BlockSpec((bb, tb, QK), bt(ZG_Q // QK)),
        pl.BlockSpec((bb, tb, QK), bt(ZG_K // QK)),
        pl.BlockSpec((bb, tb, VW), bt(ZG_V // VW)),
        pl.BlockSpec((bb, tb, VW), bt(ZG_R // VW)),
        pl.BlockSpec((bb, tb, LANES), bt(0)),
        pl.BlockSpec((LANES, QK), lambda i, j: (0, 0)),
        pl.BlockSpec((1, QK), lambda i, j: (0, 0)),
        pl.BlockSpec((1, VW), lambda i, j: (0, 0)),
        pl.BlockSpec((bb, HEADS, DK, DV), st4),
    ]
    args = [zg3, zg3, zg3, zg3, sm3, wa_pad, ba, gn, s0]
    out_specs = [
        pl.BlockSpec((bb, tb, VW), lambda i, j: (i, j, 0)),
        pl.BlockSpec((bb, HEADS, DK, DV), st4),
    ]
    out_shape = [
        jax.ShapeDtypeStruct((b, t, VW), BF16),
        jax.ShapeDtypeStruct((b, HEADS, DK, DV), F32),
    ]
    if proj is not None:
        x3, gx, pws = proj
        assert bb == 1 and len(pws) == N_PROJ
        in_specs += [pl.BlockSpec((1, tb, D_MODEL), lambda i, j: (i, j, 0)), _resident((1, D_MODEL))]
        in_specs += [_resident(w.shape) for w in pws]
        args += [x3, gx] + list(pws)
        for w in pws:
            npieces, _, width = w.shape
            out_specs.append(pl.BlockSpec((npieces, 1, tb, width), lambda i, j: (0, i, j, 0)))
            out_shape.append(jax.ShapeDtypeStruct((npieces, b, t, width), BF16))
        scratch = scratch + [pltpu.VMEM((tb, D_MODEL), BF16)]
    out = pl.pallas_call(
        functools.partial(_gla_kernel, chunk=chunk, sub=sub, t_valid=t_valid, carried=carried,
                          proj=proj is not None),
        grid=(b // bb, t // tb),
        in_specs=in_specs,
        out_specs=out_specs,
        out_shape=out_shape,
        scratch_shapes=scratch,
        compiler_params=_cparams(("parallel", "arbitrary")),
        name="gla",
    )(*args)
    return out[0], out[1], tuple(out[2:])


def _layer(x3, mem_k, mem_v, c0, n0, m0, s0, p, final_norm, m_chunk, g_chunk, g_sub, tb, bb, tq):
    b, t, _ = x3.shape
    n = b * t
    tmix = -(-t // tb) * tb
    x = x3.reshape(n, D_MODEL)
    x = _ffn(x, p["ffn1_norm"], p["ffn1_wg"], p["ffn1_wu"], p["ffn1_wd"])
    zg, sm = _normproj(x, p["mix_norm"], p["w_in_g"], BF16, tn=ZG_WIDTH // 3, w_side=p["w_in_side"], tm=1024,
                       name="inproj")
    pad_t = lambda a, axis: jnp.pad(a, [(0, tmix - t if d == axis else 0) for d in range(a.ndim)])
    zg3 = pad_t(zg.reshape(b, t, ZG_WIDTH), 1)
    sm3 = pad_t(sm.reshape(b, t, LANES), 1)
    gla_args = (zg3, sm3, p["wa_pad"], p["gla_ba"], p["gla_norm"], s0, g_chunk, g_sub, tb, bb, t)
    if tmix // tb > 1 and tb // g_chunk == N_PIECES:
        assert tmix == t
        hg, s1, pieces = _gla(*gla_args, proj=(x.reshape(b, t, D_MODEL), p["mix_norm"], p["w_in_pieces"]))
    else:
        hg, s1, _ = _gla(*gla_args)
        pieces = []
        for w in p["w_in_pieces"]:
            npc, _, width = w.shape
            w2 = w.transpose(1, 0, 2).reshape(D_MODEL, npc * width)
            zp = _normproj(x, p["mix_norm"], w2, BF16, tn=npc * width, tm=1024, name="inproj_pieces")
            pieces.append(zp.reshape(n, npc, width).transpose(1, 0, 2).reshape(npc, b, t, width))
    hm, c1, n1, m1 = _mlstm(pad_t(pieces[0], 2), pad_t(pieces[1], 2), sm3, p["bias_row"], p["mlstm_norm"], c0, n0, m0,
                            m_chunk, tb, bb, t)
    flat = lambda a: a.reshape(a.shape[0], n, a.shape[3])
    x, q = _merge(flat(pieces[1]), flat(pieces[2]), hm[:, :t].reshape(n, VW), hg[:, :t].reshape(n, VW), x,
                  p["w_br_m"], p["w_br_g"], p["w_out"], p["ca_norm"], p["ca_wq"])
    if mem_k.ndim == 4:
        o = _attn_cache(q.reshape(b, t, D_MODEL), mem_k, mem_v, t=t, bb=4)
    else:
        o = _attn(q.reshape(b, t, D_MODEL), mem_k, mem_v, tq)
    y = _ffn(x, p["ffn2_norm"], p["ffn2_wg"], p["ffn2_wu"], p["ffn2_wd"], final_g=final_norm,
             res_a=o.reshape(n, D_MODEL), res_w=p["ca_wo"])
    return y.reshape(b, t, D_MODEL), (c1, n1, m1, s1)


def _permute_w_in(w_in):
    sizes = (QK, QK, VW, VW, HEADS, HEADS, QK, QK, VW, VW, G_RANK, D_MODEL, D_MODEL)
    offs = [0]
    for s in sizes:
        offs.append(offs[-1] + s)
    part = lambda i: w_in[:, offs[i]:offs[i + 1]]
    (mq, mk, mv, mo, mi, mf, gq, gk, gv, gr, ga, gate_m, gate_g) = [part(i) for i in range(len(sizes))]
    pad = jnp.zeros((D_MODEL, LANES - 2 * HEADS - G_RANK), w_in.dtype)
    w_g = jnp.concatenate([gq, gk, gv, gr], axis=1)
    w_side = jnp.concatenate([mi, mf, ga, pad], axis=1)

    def pieces(w, width):
        return w.reshape(D_MODEL, w.shape[1] // width, width).transpose(1, 0, 2).astype(BF16)

    w_pieces = (pieces(jnp.concatenate([mq, mk, mv], axis=1), PIECE_W[0]),
                pieces(jnp.concatenate([mo, gate_m], axis=1), PIECE_W[1]),
                pieces(gate_g, PIECE_W[2]))
    return w_g.astype(BF16), w_side.astype(BF16), w_pieces


def kernel(x_prompt, x_sample, mem_prompt, state_mlstm_C, state_mlstm_n, state_mlstm_m, state_gla_S, cache_mem_k, cache_mem_v, ffn1_norm, ffn1_wg, ffn1_wu, ffn1_wd, mix_norm, w_in, b_if, gla_wa2, gla_ba, mlstm_norm, gla_norm, w_br_m, w_br_g, w_out, ca_norm, mem_norm, ca_wq, ca_wk, ca_wv, ca_wo, ffn2_norm, ffn2_wg, ffn2_wu, ffn2_wd, final_norm):
    depth = ffn1_norm.shape[0]
    assert depth == 1
    l = 0
    bp, tp, _ = x_prompt.shape
    bs, ts, _ = x_sample.shape
    row = lambda v: v.reshape(1, -1).astype(F32)
    bias_row = jnp.zeros((1, LANES), F32).at[0, :2 * HEADS].set(b_if[l])
    wa_pad = jnp.zeros((LANES, QK), F32).at[SM_A:SM_A + G_RANK, :].set(gla_wa2[l])
    w_in_g, w_in_side, w_in_pieces = _permute_w_in(w_in[l])
    p = {
        "w_in_g": w_in_g, "w_in_side": w_in_side, "w_in_pieces": w_in_pieces,
        "ffn1_norm": row(ffn1_norm[l]), "ffn1_wg": ffn1_wg[l].astype(BF16), "ffn1_wu": ffn1_wu[l].astype(BF16),
        "ffn1_wd": ffn1_wd[l].astype(BF16),
        "mix_norm": row(mix_norm[l]),
        "bias_row": bias_row, "wa_pad": wa_pad, "gla_ba": row(gla_ba[l]),
        "mlstm_norm": row(mlstm_norm[l]), "gla_norm": row(gla_norm[l]),
        "w_br_m": w_br_m[l].astype(BF16), "w_br_g": w_br_g[l].astype(BF16), "w_out": w_out[l].astype(BF16),
        "ca_norm": row(ca_norm[l]), "ca_wq": ca_wq[l].astype(BF16), "ca_wo": ca_wo[l].astype(BF16),
        "ffn2_norm": row(ffn2_norm[l]), "ffn2_wg": ffn2_wg[l].astype(BF16), "ffn2_wu": ffn2_wu[l].astype(BF16),
        "ffn2_wd": ffn2_wd[l].astype(BF16),
    }
    fin = row(final_norm)

    mem2 = mem_prompt.reshape(bp * N_MEM, D_MODEL)
    mk_p = _normproj(mem2, row(mem_norm[l]), ca_wk[l].astype(BF16), F32, tn=D_MODEL, name="memk")
    mv_p = _normproj(mem2, row(mem_norm[l]), ca_wv[l].astype(BF16), F32, tn=D_MODEL, name="memv")
    mk_p = mk_p.reshape(bp, N_MEM, D_MODEL)
    mv_p = mv_p.reshape(bp, N_MEM, D_MODEL)
    zc = jnp.zeros((bp, HEADS, DV, DK), F32)
    zn = jnp.zeros((bp, HEADS, DK), F32)
    zm = jnp.zeros((bp, HEADS), F32)
    zs = jnp.zeros((bp, HEADS, DK, DV), F32)
    yp, (cp, np_, mp, sp) = _layer(x_prompt, mk_p, mv_p, zc, zn, zm, zs, p, fin,
                                   m_chunk=128, g_chunk=64, g_sub=8, tb=512, bb=1, tq=512)

    tpad = -(-ts // SUBLANES) * SUBLANES
    ys, (cs, ns, ms, ss) = _layer(x_sample, cache_mem_k[l], cache_mem_v[l],
                                  state_mlstm_C[l], state_mlstm_n[l], state_mlstm_m[l], state_gla_S[l],
                                  p, fin, m_chunk=tpad, g_chunk=tpad, g_sub=tpad, tb=tpad, bb=8, tq=None)

    st = lambda a: a[None]
    return (yp, ys, st(cp), st(np_), st(mp), st(sp),
            st(mk_p.reshape(bp, N_MEM, HEADS, C_HD)), st(mv_p.reshape(bp, N_MEM, HEADS, C_HD)),
            st(cs), st(ns), st(ms), st(ss))
```

```python
import functools
import math

import jax
import jax.numpy as jnp
from jax import lax
from jax.experimental import pallas as pl
from jax.experimental.pallas import tpu as pltpu

F32 = jnp.float32
BF16 = jnp.bfloat16

D_MODEL = 1024
D_FF = 2816
HEADS = 4
DK = 128
DV = 256
QK = HEADS * DK
VW = HEADS * DV
G_RANK = 16
G_TAU = 16.0
N_MEM = 256
C_HD = D_MODEL // HEADS
EPS = 1e-6
LOG2E = math.log2(math.e)
LANES = 128
SUBLANES = 8

Z_MQ = 0
Z_MK = Z_MQ + QK
Z_MV = Z_MK + QK
Z_GQ = Z_MV + VW
Z_GK = Z_GQ + QK
Z_GV = Z_GK + QK
Z_MO = Z_GV + VW
Z_GR = Z_MO + VW
Z_GATE_M = Z_GR + VW
Z_GATE_G = Z_GATE_M + D_MODEL
Z_WIDTH = Z_GATE_G + D_MODEL
SM_I = 0
SM_F = HEADS
SM_A = 2 * HEADS

VMEM_LIMIT = 56 * 1024 * 1024


def _cparams(sem):
    return pltpu.CompilerParams(dimension_semantics=sem, vmem_limit_bytes=VMEM_LIMIT)


def _rms(x, g):
    return x * lax.rsqrt(jnp.mean(x * x, axis=-1, keepdims=True) + EPS) * g


def _sigmoid(x):
    return 1.0 / (1.0 + jnp.exp(-x))


def _log_sigmoid(x):
    return jnp.minimum(x, 0.0) - jnp.log(1.0 + jnp.exp(-jnp.abs(x)))


def _dot(a, b):
    return jnp.dot(a, b, preferred_element_type=F32)


def _dot_nt(a, b):
    return lax.dot_general(a, b, (((1,), (1,)), ((), ())), preferred_element_type=F32)


def _split3(x):
    hi = x.astype(BF16)
    r1 = x - hi.astype(F32)
    mid = r1.astype(BF16)
    lo = (r1 - mid.astype(F32)).astype(BF16)
    return hi, mid, lo


def _dot_exact_left(m_bf16, x):
    return sum(_dot(m_bf16, p) for p in _split3(x))


def _resident(shape):
    zeros = (0,) * len(shape)
    return pl.BlockSpec(shape, lambda *_: zeros, pipeline_mode=pl.Buffered(1))


def _ffn_kernel(*refs, final, fused_res):
    refs = list(refs)
    if fused_res:
        a_ref, wr_ref = refs[:2]
        refs = refs[2:]
    x_ref, g_ref, wg_ref, wu_ref, wd_ref = refs[:5]
    refs = refs[5:]
    if final:
        fg_ref = refs.pop(0)
    (o_ref,) = refs
    x = x_ref[...]
    if fused_res:
        x = x + _dot(a_ref[...], wr_ref[...])
    hn = _rms(x, g_ref[...]).astype(BF16)
    a = _dot(hn, wg_ref[...])
    u = _dot(hn, wu_ref[...])
    act = (a * _sigmoid(a) * u).astype(BF16)
    y = x + 0.5 * _dot(act, wd_ref[...])
    if final:
        y = _rms(y, fg_ref[...])
    o_ref[...] = y


def _ffn(x, g, wg, wu, wd, final_g=None, res_a=None, res_w=None, tm=512):
    n = x.shape[0]
    tm = min(tm, n)
    assert n % tm == 0
    final = final_g is not None
    fused_res = res_a is not None
    row = pl.BlockSpec((tm, D_MODEL), lambda i: (i, 0))
    in_specs, args = [], []
    if fused_res:
        in_specs += [row, _resident((D_MODEL, D_MODEL))]
        args += [res_a, res_w]
    in_specs += [row, _resident((1, D_MODEL)), _resident((D_MODEL, D_FF)), _resident((D_MODEL, D_FF)),
                 _resident((D_FF, D_MODEL))]
    args += [x, g, wg, wu, wd]
    if final:
        in_specs.append(_resident((1, D_MODEL)))
        args.append(final_g)
    return pl.pallas_call(
        functools.partial(_ffn_kernel, final=final, fused_res=fused_res),
        grid=(n // tm,),
        in_specs=in_specs,
        out_specs=row,
        out_shape=jax.ShapeDtypeStruct((n, D_MODEL), F32),
        compiler_params=_cparams(("parallel",)),
        name="ffn_final" if final else "ffn",
    )(*args)


def _normproj_kernel(*refs, side):
    if side:
        x_ref, g_ref, w_ref, ws_ref, o_ref, s_ref, hn_ref = refs
    else:
        x_ref, g_ref, w_ref, o_ref, hn_ref = refs

    @pl.when(pl.program_id(1) == 0)
    def _():
        hn = _rms(x_ref[...], g_ref[...]).astype(BF16)
        hn_ref[...] = hn
        if side:
            s_ref[...] = _dot(hn, ws_ref[...])

    o_ref[...] = _dot(hn_ref[...], w_ref[...]).astype(o_ref.dtype)


def _normproj(x, g, w, out_dtype, tn, w_side=None, tm=512, name="normproj"):
    n = x.shape[0]
    width = w.shape[1]
    tm = min(tm, n)
    assert n % tm == 0 and width % tn == 0
    side = w_side is not None
    in_specs = [
        pl.BlockSpec((tm, D_MODEL), lambda i, j: (i, 0)),
        pl.BlockSpec((1, D_MODEL), lambda i, j: (0, 0)),
        pl.BlockSpec((D_MODEL, tn), lambda i, j: (0, j)),
    ]
    out_specs = [pl.BlockSpec((tm, tn), lambda i, j: (i, j))]
    out_shape = [jax.ShapeDtypeStruct((n, width), out_dtype)]
    args = [x, g, w]
    if side:
        ws = w_side.shape[1]
        in_specs.append(pl.BlockSpec((D_MODEL, ws), lambda i, j: (0, 0)))
        out_specs.append(pl.BlockSpec((tm, ws), lambda i, j: (i, 0)))
        out_shape.append(jax.ShapeDtypeStruct((n, ws), F32))
        args.append(w_side)
    out = pl.pallas_call(
        functools.partial(_normproj_kernel, side=side),
        grid=(n // tm, width // tn),
        in_specs=in_specs,
        out_specs=out_specs,
        out_shape=out_shape,
        scratch_shapes=[pltpu.VMEM((tm, D_MODEL), BF16)],
        compiler_params=_cparams(("parallel", "arbitrary")),
        name=name,
    )(*args)
    return out if side else out[0]


def _merge_kernel(gm_ref, gg_ref, hm_ref, hg_ref, x_ref, wm_ref, wg_ref, wo_ref, gq_ref, wq_ref, o_ref, q_ref):
    ym = _dot(hm_ref[...], wm_ref[...])
    yg = _dot(hg_ref[...], wg_ref[...])
    y = _sigmoid(gm_ref[...].astype(F32)) * ym + _sigmoid(gg_ref[...].astype(F32)) * yg
    x = x_ref[...] + _dot(y.astype(BF16), wo_ref[...])
    o_ref[...] = x
    q_ref[...] = _dot(_rms(x, gq_ref[...]).astype(BF16), wq_ref[...]).astype(q_ref.dtype)


def _merge(z, hm, hg, x, wm, wg, wo, gq, wq, tm=512):
    n = x.shape[0]
    tm = min(tm, n)
    assert n % tm == 0
    row = lambda i: (i, 0)
    return pl.pallas_call(
        _merge_kernel,
        grid=(n // tm,),
        in_specs=[
            pl.BlockSpec((tm, D_MODEL), lambda i: (i, Z_GATE_M // D_MODEL)),
            pl.BlockSpec((tm, D_MODEL), lambda i: (i, Z_GATE_G // D_MODEL)),
            pl.BlockSpec((tm, VW), row),
            pl.BlockSpec((tm, VW), row),
            pl.BlockSpec((tm, D_MODEL), row),
            _resident((VW, D_MODEL)),
            _resident((VW, D_MODEL)),
            _resident((D_MODEL, D_MODEL)),
            _resident((1, D_MODEL)),
            _resident((D_MODEL, D_MODEL)),
        ],
        out_specs=[pl.BlockSpec((tm, D_MODEL), row), pl.BlockSpec((tm, D_MODEL), row)],
        out_shape=[jax.ShapeDtypeStruct((n, D_MODEL), F32), jax.ShapeDtypeStruct((n, D_MODEL), BF16)],
        compiler_params=_cparams(("parallel",)),
        name="merge",
    )(z, z, hm, hg, x, wm, wg, wo, gq, wq)


def _attn_kernel(q_ref, k_ref, v_ref, o_ref):
    scale = C_HD ** -0.5
    cols = [slice(h * C_HD, (h + 1) * C_HD) for h in range(HEADS)]
    scores = [_dot_nt(q_ref[0, :, cs], k_ref[0, :, cs].astype(BF16)) for cs in cols]
    probs = []
    for s in scores:
        s = s * scale
        p = jnp.exp(s - jnp.max(s, axis=-1, keepdims=True))
        probs.append((p / jnp.sum(p, axis=-1, keepdims=True)).astype(BF16))
    outs = [_dot(p, v_ref[0, :, cs].astype(BF16)) for p, cs in zip(probs, cols)]
    for o, cs in zip(outs, cols):
        o_ref[0, :, cs] = o.astype(o_ref.dtype)


def _attn(q, k, v, tq):
    b, t, _ = q.shape
    assert t % tq == 0
    return pl.pallas_call(
        _attn_kernel,
        grid=(b, t // tq),
        in_specs=[
            pl.BlockSpec((1, tq, D_MODEL), lambda i, j: (i, j, 0)),
            pl.BlockSpec((1, N_MEM, D_MODEL), lambda i, j: (i, 0, 0)),
            pl.BlockSpec((1, N_MEM, D_MODEL), lambda i, j: (i, 0, 0)),
        ],
        out_specs=pl.BlockSpec((1, tq, D_MODEL), lambda i, j: (i, j, 0)),
        out_shape=jax.ShapeDtypeStruct((b, t, D_MODEL), BF16),
        compiler_params=_cparams(("parallel", "arbitrary")),
        name="attn",
    )(q, k, v)


CACHE_HALVES = C_HD // LANES
CACHE_ROWS = HEADS * CACHE_HALVES


def _attn_cache_kernel(q_ref, k_ref, v_ref, o_ref, *, t):
    scale = C_HD ** -0.5
    nq = t * HEADS
    lane = lax.broadcasted_iota(jnp.int32, (nq, LANES), 1)
    rowi = lax.broadcasted_iota(jnp.int32, (nq, LANES), 0)
    valid = ((lane % CACHE_ROWS) // HEADS == 0) & (lane % HEADS == rowi % HEADS)
    n_tiles = N_MEM * CACHE_ROWS // LANES
    for bi in range(q_ref.shape[0]):
        kn = k_ref[bi].astype(BF16)
        vn = v_ref[bi].astype(BF16)
        s = _dot_nt(q_ref[bi], kn)
        tiles = []
        for j in range(n_tiles):
            cs = slice(j * LANES, (j + 1) * LANES)
            sj = s[0:nq, cs] + pltpu.roll(s[nq:2 * nq, cs], LANES - HEADS, 1)
            tiles.append(jnp.where(valid, sj * scale, -jnp.inf))
        mx = functools.reduce(jnp.maximum, [jnp.max(x, axis=1, keepdims=True) for x in tiles])
        ps = [jnp.exp(x - mx) for x in tiles]
        den = functools.reduce(jnp.add, [jnp.sum(x, axis=1, keepdims=True) for x in ps])
        inv = 1.0 / den
        p0 = jnp.concatenate([x * inv for x in ps], axis=1)
        p1 = jnp.concatenate([pltpu.roll(x * inv, HEADS, 1) for x in ps], axis=1)
        p = jnp.concatenate([p0, p1], axis=0).astype(BF16)
        o_ref[bi] = _dot(p, vn).astype(o_ref.dtype)


def _attn_cache(q, k, v, bb):
    assert CACHE_HALVES == 2
    b, t, _ = q.shape
    assert b % bb == 0
    nq = t * HEADS
    qv = q.reshape(b, t, HEADS, CACHE_HALVES, LANES).transpose(0, 3, 1, 2, 4)
    qv = qv.reshape(b, CACHE_HALVES * nq, LANES)

    def cache_view(a):
        a = a.reshape(b, N_MEM, HEADS, CACHE_HALVES, LANES).transpose(0, 1, 3, 2, 4)
        return a.reshape(b, N_MEM * CACHE_ROWS, LANES)

    o = pl.pallas_call(
        functools.partial(_attn_cache_kernel, t=t),
        grid=(b // bb,),
        in_specs=[
            pl.BlockSpec((bb, CACHE_HALVES * nq, LANES), lambda i: (i, 0, 0)),
            pl.BlockSpec((bb, N_MEM * CACHE_ROWS, LANES), lambda i: (i, 0, 0)),
            pl.BlockSpec((bb, N_MEM * CACHE_ROWS, LANES), lambda i: (i, 0, 0)),
        ],
        out_specs=pl.BlockSpec((bb, CACHE_HALVES * nq, LANES), lambda i: (i, 0, 0)),
        out_shape=jax.ShapeDtypeStruct((b, CACHE_HALVES * nq, LANES), BF16),
        compiler_params=_cparams(("parallel",)),
        name="attn_cache",
    )(qv, cache_view(k), cache_view(v))
    return o.reshape(b, CACHE_HALVES, t, HEADS, LANES).transpose(0, 2, 3, 1, 4).reshape(b, t, D_MODEL)


def _tile_lanes(rep, width):
    if width <= LANES:
        return rep[:, :width]
    return jnp.concatenate([rep] * (width // LANES), axis=1)


def _mlstm_kernel(q_ref, k_ref, v_ref, og_ref, sm_ref, bias_ref, gn_ref, c0_ref, n0_ref, m0_ref,
                  h_ref, c1_ref, n1_ref, m1_ref, *scratch, chunk, t_valid, carried):
    L = chunk
    bb, tb = q_ref.shape[0], q_ref.shape[1]
    scale = DK ** -0.5

    if carried:
        c_scr, n_scr, m_scr = scratch

        @pl.when(pl.program_id(1) == 0)
        def _():
            c_scr[...] = c0_ref[0]
            n_scr[...] = n0_ref[0]
            m_scr[...] = m0_ref[0]

    row = lax.broadcasted_iota(jnp.int32, (L, L), 0)
    col = lax.broadcasted_iota(jnp.int32, (L, L), 1)
    tri = col <= row
    tri_lo = jnp.where(tri, 1.0, 0.0).astype(BF16)
    lane = lax.broadcasted_iota(jnp.int32, (L, LANES), 1)
    spread = jnp.where(lax.broadcasted_iota(jnp.int32, (LANES, 2 * HEADS * LANES), 0)
                       == lax.broadcasted_iota(jnp.int32, (LANES, 2 * HEADS * LANES), 1) // LANES,
                       1.0, 0.0).astype(BF16)
    pick = jnp.where(lax.broadcasted_iota(jnp.int32, (SUBLANES, LANES), 0)
                     == lax.broadcasted_iota(jnp.int32, (SUBLANES, LANES), 1), 1.0, 0.0).astype(BF16)

    def chunk_group(bis, r):
        rows = pl.ds(r, L)
        cols_of, rows_of = {}, {}
        for bi in bis:
            g = sm_ref[bi, rows, :] + bias_ref[...]
            lf_all = _log_sigmoid(g)
            ig_all = g
            if t_valid < L:
                valid = lax.broadcasted_iota(jnp.int32, (L, 1), 0) < t_valid
                lf_all = jnp.where(valid, lf_all, 0.0)
                ig_all = jnp.where(valid, g, -1e30)
            f_all = _dot_exact_left(tri_lo, lf_all)
            x = jnp.where(lane < SM_F, ig_all, f_all) * LOG2E
            parts = _split3(x)
            cols_of[bi] = sum(_dot(p, spread) for p in parts)
            rows_of[bi] = sum(_dot_nt(pick, p) for p in parts)
        pairs = [(bi, h) for bi in bis for h in range(HEADS)]

        def state(bi):
            if carried:
                return c_scr, n_scr, m_scr, c_scr, n_scr, m_scr
            return (c0_ref.at[bi], n0_ref.at[bi], m0_ref.at[bi], c1_ref.at[bi], n1_ref.at[bi], m1_ref.at[bi])

        st = {}
        for bi, h in pairs:
            c_in, n_in, m_in = state(bi)[:3]
            slab = lambda c, bi=bi: cols_of[bi][:, c * LANES:(c + 1) * LANES]
            f_rep = slab(SM_F + h)
            ig_rep = slab(SM_I + h)
            f_row = rows_of[bi][SM_F + h:SM_F + h + 1, :]
            ig_row = rows_of[bi][SM_I + h:SM_I + h + 1, :]
            dmat = jnp.where(tri, _tile_lanes(f_rep, L) - f_row + ig_row, -jnp.inf)
            m_inter = m_in[h:h + 1, :] * LOG2E + f_rep
            m = jnp.maximum(m_inter, jnp.max(dmat, axis=1, keepdims=True))
            w = jnp.exp2(dmat - _tile_lanes(m - math.log2(scale), L))
            a = jnp.exp2(m_inter - m)
            m_end = m[L - 1:L, :]
            w_rep = jnp.exp2(f_rep[L - 1:L, :] - f_rep + ig_rep - (m_end - math.log2(scale)))
            st[bi, h] = dict(m=m, w=w, a=a, m_end=m_end, w_rep=w_rep, c0=c_in[h], n0=n_in[h:h + 1, :],
                             qb=q_ref[bi, rows, h * DK:(h + 1) * DK], kb=k_ref[bi, rows, h * DK:(h + 1) * DK],
                             vb=v_ref[bi, rows, h * DV:(h + 1) * DV], og=og_ref[bi, rows, h * DV:(h + 1) * DV])
        for key in pairs:
            d = st[key]
            d["s"] = _dot_nt(d["qb"], d["kb"]) * d["w"]
        for key in pairs:
            d = st[key]
            d["inter"] = _dot_nt(d["qb"], d["c0"].astype(BF16))
        for key in pairs:
            d = st[key]
            d["sv"] = _dot(d["s"].astype(BF16), d["vb"])
        for key in pairs:
            d = st[key]
            vw_t = (d["vb"].astype(F32) * _tile_lanes(d["w_rep"], DV)).T.astype(BF16)
            d["c_new"] = d["a"][L - 1:L, :] * d["c0"] + _dot(vw_t, d["kb"])
        for bi, h in pairs:
            d = st[bi, h]
            a, m = d["a"], d["m"]
            qf = d["qb"].astype(F32)
            kf = d["kb"].astype(F32)
            num = _tile_lanes(a, DV) * d["inter"] + d["sv"]
            den = a * jnp.sum(qf * d["n0"], axis=1, keepdims=True) + jnp.sum(d["s"], axis=1, keepdims=True)
            hh = num / _tile_lanes(jnp.maximum(jnp.abs(den), jnp.exp2(-m)), DV)
            d["n_new"] = a[L - 1:L, :] * d["n0"] + jnp.sum(kf * d["w_rep"], axis=0, keepdims=True)
            hn = hh * lax.rsqrt(jnp.mean(hh * hh, axis=-1, keepdims=True) + EPS)
            hn = hn * gn_ref[:, h * DV:(h + 1) * DV] * _sigmoid(d["og"].astype(F32))
            d["hn"] = hn.astype(h_ref.dtype)
        for bi, h in pairs:
            d = st[bi, h]
            c_out, n_out, m_out = state(bi)[3:]
            c_out[h] = d["c_new"]
            n_out[h:h + 1, :] = d["n_new"]
            m_out[h:h + 1, :] = d["m_end"] * (1.0 / LOG2E)
            h_ref[bi, rows, h * DV:(h + 1) * DV] = d["hn"]

    if tb == L:
        chunk_group(list(range(bb)), 0)
    else:
        assert bb == 1

        def loop_body(ci, carry):
            chunk_group([0], pl.multiple_of(ci * L, L))
            return carry

        lax.fori_loop(0, tb // L, loop_body, 0, unroll=2)

    if carried:
        @pl.when(pl.program_id(1) == pl.num_programs(1) - 1)
        def _():
            c1_ref[0] = c_scr[...]
            n1_ref[0] = n_scr[...]
            m1_ref[0] = m_scr[...]


def _mixer_specs():
    bt = lambda col: (lambda i, j: (i, j, col))
    return bt, (lambda i, j: (i, 0, 0, 0)), (lambda i, j: (i, 0, 0))


def _mlstm(z3, sm3, bias_row, gn, c0, n0, m0, chunk, tb, bb, t_valid):
    b, t, _ = z3.shape
    assert t % tb == 0 and tb % chunk == 0 and b % bb == 0
    assert t_valid == t or (tb == t and chunk == t)
    carried = t // tb > 1
    assert not (carried and bb > 1)
    m0 = jnp.broadcast_to(m0[:, :, None], (b, HEADS, LANES))
    bt, st4, st3 = _mixer_specs()
    scratch = [pltpu.VMEM((HEADS, DV, DK), F32), pltpu.VMEM((HEADS, DK), F32),
               pltpu.VMEM((HEADS, LANES), F32)] if carried else []
    h, c1, n1, m1 = pl.pallas_call(
        functools.partial(_mlstm_kernel, chunk=chunk, t_valid=t_valid, carried=carried),
        grid=(b // bb, t // tb),
        in_specs=[
            pl.BlockSpec((bb, tb, QK), bt(Z_MQ // QK)),
            pl.BlockSpec((bb, tb, QK), bt(Z_MK // QK)),
            pl.BlockSpec((bb, tb, VW), bt(Z_MV // VW)),
            pl.BlockSpec((bb, tb, VW), bt(Z_MO // VW)),
            pl.BlockSpec((bb, tb, LANES), bt(0)),
            pl.BlockSpec((1, LANES), lambda i, j: (0, 0)),
            pl.BlockSpec((1, VW), lambda i, j: (0, 0)),
            pl.BlockSpec((bb, HEADS, DV, DK), st4),
            pl.BlockSpec((bb, HEADS, DK), st3),
            pl.BlockSpec((bb, HEADS, LANES), st3),
        ],
        out_specs=[
            pl.BlockSpec((bb, tb, VW), lambda i, j: (i, j, 0)),
            pl.BlockSpec((bb, HEADS, DV, DK), st4),
            pl.BlockSpec((bb, HEADS, DK), st3),
            pl.BlockSpec((bb, HEADS, LANES), st3),
        ],
        out_shape=[
            jax.ShapeDtypeStruct((b, t, VW), BF16),
            jax.ShapeDtypeStruct((b, HEADS, DV, DK), F32),
            jax.ShapeDtypeStruct((b, HEADS, DK), F32),
            jax.ShapeDtypeStruct((b, HEADS, LANES), F32),
        ],
        scratch_shapes=scratch,
        compiler_params=_cparams(("parallel", "arbitrary")),
        name="mlstm",
    )(z3, z3, z3, z3, sm3, bias_row, gn, c0, n0, m0)
    return h, c1, n1, m1[:, :, 0]


def _gla_kernel(q_ref, k_ref, v_ref, r_ref, sm_ref, wa_ref, ba_ref, gn_ref, s0_ref,
                h_ref, s1_ref, *scratch, chunk, sub, t_valid, carried):
    L = chunk
    nb = L // sub
    bb, tb = q_ref.shape[0], q_ref.shape[1]
    scale = DK ** -0.5

    if carried:
        s_scr, b_scr, kf_scr = scratch

        @pl.when(pl.program_id(1) == 0)
        def _():
            s_scr[...] = s0_ref[0]
    else:
        b_scr, kf_scr = scratch

    row = lax.broadcasted_iota(jnp.int32, (L, L), 0)
    col = lax.broadcasted_iota(jnp.int32, (L, L), 1)
    tri_lo = jnp.where(col <= row, 1.0, 0.0).astype(BF16)
    in_blk = (col // sub) == (row // sub)
    trow = lax.broadcasted_iota(jnp.int32, (sub, DK), 0)
    eye = lax.broadcasted_iota(jnp.int32, (DK, DK), 0) == lax.broadcasted_iota(jnp.int32, (DK, DK), 1)
    place = jnp.where(lax.broadcasted_iota(jnp.int32, (sub * DK, LANES), 0) // DK
                      == lax.broadcasted_iota(jnp.int32, (sub * DK, LANES), 1) % sub, 1.0, 0.0).astype(BF16)
    wa_hi = wa_ref[...].astype(BF16)
    wa_lo = (wa_ref[...] - wa_hi.astype(F32)).astype(BF16)

    def stage(bi):
        sm = sm_ref[bi]
        sm_hi = sm.astype(BF16)
        sm_lo = (sm - sm_hi.astype(F32)).astype(BF16)
        a_raw = _dot(sm_hi, wa_hi) + _dot(sm_lo, wa_hi) + _dot(sm_hi, wa_lo) + ba_ref[...]
        la = _log_sigmoid(a_raw) * (LOG2E / G_TAU)
        if t_valid < L:
            valid = lax.broadcasted_iota(jnp.int32, (L, 1), 0) < t_valid
            la = jnp.where(valid, la, 0.0)
        for c in range(tb // L):
            b_scr[bi, c * L:(c + 1) * L, :] = _dot_exact_left(tri_lo, la[c * L:(c + 1) * L, :])
        kf_scr[bi] = k_ref[bi].astype(F32)

    def chunk_group(bis, r):
        rows = pl.ds(r, L)
        pairs = [(bi, h) for bi in bis for h in range(HEADS)]

        def state(bi):
            return (s_scr, s_scr) if carried else (s0_ref.at[bi], s1_ref.at[bi])

        slabs = []
        for bi, h in pairs:
            ks = slice(h * DK, (h + 1) * DK)
            qf = q_ref[bi, rows, ks].astype(F32) * scale
            for i in range(nb):
                blk = pl.ds(pl.multiple_of(r + i * sub, sub), sub)
                bs = b_scr[bi, blk, ks]
                kblk = kf_scr[bi, blk, ks]
                qs = qf[i * sub:(i + 1) * sub, :]
                row_slabs = []
                for s in range(sub):
                    e = jnp.exp2(jnp.where(trow >= s, bs - bs[s:s + 1, :], -jnp.inf))
                    row_slabs.append(e * qs * kblk[s:s + 1, :])
                slabs.append(jnp.concatenate(row_slabs, axis=1))
        own = _dot(jnp.concatenate(slabs, axis=0).astype(BF16), place)

        st = {}
        for n, (bi, h) in enumerate(pairs):
            ks = slice(h * DK, (h + 1) * DK)
            vs = slice(h * DV, (h + 1) * DV)
            bh = b_scr[bi, rows, ks]
            qf = q_ref[bi, rows, ks].astype(F32) * scale
            kf = kf_scr[bi, rows, ks]
            s0 = state(bi)[0][h]
            d = dict(vb=v_ref[bi, rows, vs], rg=r_ref[bi, rows, vs], s0=s0, s0b=s0.astype(BF16),
                     qhat=(qf * jnp.exp2(bh)).astype(BF16), a_own=own[n * L:(n + 1) * L, 0:L])
            d["qk"] = []
            for i in range(1, nb):
                r0 = i * sub
                b_ref = bh[r0:r0 + 1, :]
                qt = qf[r0:r0 + sub, :] * jnp.exp2(bh[r0:r0 + sub, :] - b_ref)
                kt = kf[0:r0, :] * jnp.exp2(b_ref - bh[0:r0, :])
                kt = jnp.concatenate([kt, jnp.zeros((L - r0, DK), F32)], axis=0)
                d["qk"].append((qt.astype(BF16), kt.astype(BF16)))
            b_end = bh[L - 1:L, :]
            d["e_col"] = jnp.sum(jnp.where(eye, jnp.exp2(b_end), 0.0), axis=1, keepdims=True)
            d["ke_t"] = (kf * jnp.exp2(b_end - bh)).T.astype(BF16)
            st[bi, h] = d
        for key in pairs:
            d = st[key]
            d["o"] = _dot(d["qhat"], d["s0b"])
        for key in pairs:
            d = st[key]
            d["blocks"] = [_dot_nt(qt, kt) for qt, kt in d["qk"]]
        for key in pairs:
            d = st[key]
            d["s_new"] = d["e_col"] * d["s0"] + _dot(d["ke_t"], d["vb"])
        for key in pairs:
            d = st[key]
            if nb > 1:
                below = jnp.concatenate([jnp.zeros((sub, L), F32)] + d["blocks"], axis=0)
                a_intra = jnp.where(in_blk, d["a_own"], below)
                d["o"] = d["o"] + _dot(a_intra.astype(BF16), d["vb"])
            else:
                d["o"] = d["o"] + _dot(d["a_own"], d["vb"].astype(F32))
        for bi, h in pairs:
            d = st[bi, h]
            o = d["o"]
            hn = o * lax.rsqrt(jnp.mean(o * o, axis=-1, keepdims=True) + EPS)
            rg = d["rg"].astype(F32)
            hn = hn * gn_ref[:, h * DV:(h + 1) * DV] * (rg * _sigmoid(rg))
            d["hn"] = hn.astype(h_ref.dtype)
        for bi, h in pairs:
            d = st[bi, h]
            state(bi)[1][h] = d["s_new"]
            h_ref[bi, rows, h * DV:(h + 1) * DV] = d["hn"]

    for bi in range(bb):
        stage(bi)
    if tb == L:
        chunk_group(list(range(bb)), 0)
    else:
        assert bb == 1

        def loop_body(ci, carry):
            chunk_group([0], pl.multiple_of(ci * L, L))
            return carry

        lax.fori_loop(0, tb // L, loop_body, 0, unroll=2)

    if carried:
        @pl.when(pl.program_id(1) == pl.num_programs(1) - 1)
        def _():
            s1_ref[0] = s_scr[...]


def _gla(z3, sm3, wa_pad, ba, gn, s0, chunk, sub, tb, bb, t_valid):
    b, t, _ = z3.shape
    assert t % tb == 0 and tb % chunk == 0 and chunk % sub == 0 and b % bb == 0
    assert t_valid == t or (tb == t and chunk == t)
    carried = t // tb > 1
    assert not (carried and bb > 1)
    bt, st4, _ = _mixer_specs()
    scratch = [pltpu.VMEM((bb, tb, QK), F32), pltpu.VMEM((bb, tb, QK), F32)]
    if carried:
        scratch = [pltpu.VMEM((HEADS, DK, DV), F32)] + scratch
    h, s1 = pl.pallas_call(
        functools.partial(_gla_kernel, chunk=chunk, sub=sub, t_valid=t_valid, carried=carried),
        grid=(b // bb, t // tb),
        in_specs=[
            pl.BlockSpec((bb, tb, QK), bt(Z_GQ // QK)),
            pl.BlockSpec((bb, tb, QK), bt(Z_GK // QK)),
            pl.BlockSpec((bb, tb, VW), bt(Z_GV // VW)),
            pl.BlockSpec((bb, tb, VW), bt(Z_GR // VW)),
            pl.BlockSpec((bb, tb, LANES), bt(0)),
            pl.BlockSpec((LANES, QK), lambda i, j: (0, 0)),
            pl.BlockSpec((1, QK), lambda i, j: (0, 0)),
            pl.BlockSpec((1, VW), lambda i, j: (0, 0)),
            pl.BlockSpec((bb, HEADS, DK, DV), st4),
        ],
        out_specs=[
            pl.BlockSpec((bb, tb, VW), lambda i, j: (i, j, 0)),
            pl.BlockSpec((bb, HEADS, DK, DV), st4),
        ],
        out_shape=[
            jax.ShapeDtypeStruct((b, t, VW), BF16),
            jax.ShapeDtypeStruct((b, HEADS, DK, DV), F32),
        ],
        scratch_shapes=scratch,
        compiler_params=_cparams(("parallel", "arbitrary")),
        name="gla",
    )(z3, z3, z3, z3, sm3, wa_pad, ba, gn, s0)
    return h, s1


def _layer(x3, mem_k, mem_v, c0, n0, m0, s0, p, final_norm, m_chunk, g_chunk, g_sub, tb, bb, tq):
    b, t, _ = x3.shape
    n = b * t
    tmix = -(-t // tb) * tb
    x = x3.reshape(n, D_MODEL)
    x = _ffn(x, p["ffn1_norm"], p["ffn1_wg"], p["ffn1_wu"], p["ffn1_wd"])
    z, sm = _normproj(x, p["mix_norm"], p["w_in"], BF16, tn=Z_WIDTH // 4, w_side=p["w_in_side"], tm=1024,
                      name="inproj")
    pad_t = lambda a: jnp.pad(a, ((0, 0), (0, tmix - t), (0, 0)))
    z3 = pad_t(z.reshape(b, t, Z_WIDTH))
    sm3 = pad_t(sm.reshape(b, t, LANES))
    hm, c1, n1, m1 = _mlstm(z3, sm3, p["bias_row"], p["mlstm_norm"], c0, n0, m0, m_chunk, tb, bb, t)
    hg, s1 = _gla(z3, sm3, p["wa_pad"], p["gla_ba"], p["gla_norm"], s0, g_chunk, g_sub, tb, bb, t)
    x, q = _merge(z, hm[:, :t].reshape(n, VW), hg[:, :t].reshape(n, VW), x, p["w_br_m"], p["w_br_g"],
                  p["w_out"], p["ca_norm"], p["ca_wq"])
    if mem_k.ndim == 4:
        o = _attn_cache(q.reshape(b, t, D_MODEL), mem_k, mem_v, bb=4)
    else:
        o = _attn(q.reshape(b, t, D_MODEL), mem_k, mem_v, tq)
    y = _ffn(x, p["ffn2_norm"], p["ffn2_wg"], p["ffn2_wu"], p["ffn2_wd"], final_g=final_norm,
             res_a=o.reshape(n, D_MODEL), res_w=p["ca_wo"])
    return y.reshape(b, t, D_MODEL), (c1, n1, m1, s1)


def _permute_w_in(w_in):
    sizes = (QK, QK, VW, VW, HEADS, HEADS, QK, QK, VW, VW, G_RANK, D_MODEL, D_MODEL)
    offs = [0]
    for s in sizes:
        offs.append(offs[-1] + s)
    part = lambda i: w_in[:, offs[i]:offs[i + 1]]
    (mq, mk, mv, mo, mi, mf, gq, gk, gv, gr, ga, gate_m, gate_g) = [part(i) for i in range(len(sizes))]
    pad = jnp.zeros((D_MODEL, LANES - 2 * HEADS - G_RANK), w_in.dtype)
    w_main = jnp.concatenate([mq, mk, mv, gq, gk, gv, mo, gr, gate_m, gate_g], axis=1)
    w_side = jnp.concatenate([mi, mf, ga, pad], axis=1)
    return w_main.astype(BF16), w_side.astype(BF16)


def kernel(x_prompt, x_sample, mem_prompt, state_mlstm_C, state_mlstm_n, state_mlstm_m, state_gla_S, cache_mem_k, cache_mem_v, ffn1_norm, ffn1_wg, ffn1_wu, ffn1_wd, mix_norm, w_in, b_if, gla_wa2, gla_ba, mlstm_norm, gla_norm, w_br_m, w_br_g, w_out, ca_norm, mem_norm, ca_wq, ca_wk, ca_wv, ca_wo, ffn2_norm, ffn2_wg, ffn2_wu, ffn2_wd, final_norm):
    depth = ffn1_norm.shape[0]
    assert depth == 1
    l = 0
    bp, tp, _ = x_prompt.shape
    bs, ts, _ = x_sample.shape
    row = lambda v: v.reshape(1, -1).astype(F32)
    bias_row = jnp.zeros((1, LANES), F32).at[0, :2 * HEADS].set(b_if[l])
    wa_pad = jnp.zeros((LANES, QK), F32).at[SM_A:SM_A + G_RANK, :].set(gla_wa2[l])
    w_in_main, w_in_side = _permute_w_in(w_in[l])
    p = {
        "w_in": w_in_main, "w_in_side": w_in_side,
        "ffn1_norm": row(ffn1_norm[l]), "ffn1_wg": ffn1_wg[l].astype(BF16), "ffn1_wu": ffn1_wu[l].astype(BF16),
        "ffn1_wd": ffn1_wd[l].astype(BF16),
        "mix_norm": row(mix_norm[l]),
        "bias_row": bias_row, "wa_pad": wa_pad, "gla_ba": row(gla_ba[l]),
        "mlstm_norm": row(mlstm_norm[l]), "gla_norm": row(gla_norm[l]),
        "w_br_m": w_br_m[l].astype(BF16), "w_br_g": w_br_g[l].astype(BF16), "w_out": w_out[l].astype(BF16),
        "ca_norm": row(ca_norm[l]), "ca_wq": ca_wq[l].astype(BF16), "ca_wo": ca_wo[l].astype(BF16),
        "ffn2_norm": row(ffn2_norm[l]), "ffn2_wg": ffn2_wg[l].astype(BF16), "ffn2_wu": ffn2_wu[l].astype(BF16),
        "ffn2_wd": ffn2_wd[l].astype(BF16),
    }
    fin = row(final_norm)

    mem2 = mem_prompt.reshape(bp * N_MEM, D_MODEL)
    mk_p = _normproj(mem2, row(mem_norm[l]), ca_wk[l].astype(BF16), F32, tn=D_MODEL, name="memk")
    mv_p = _normproj(mem2, row(mem_norm[l]), ca_wv[l].astype(BF16), F32, tn=D_MODEL, name="memv")
    mk_p = mk_p.reshape(bp, N_MEM, D_MODEL)
    mv_p = mv_p.reshape(bp, N_MEM, D_MODEL)
    zc = jnp.zeros((bp, HEADS, DV, DK), F32)
    zn = jnp.zeros((bp, HEADS, DK), F32)
    zm = jnp.zeros((bp, HEADS), F32)
    zs = jnp.zeros((bp, HEADS, DK, DV), F32)
    yp, (cp, np_, mp, sp) = _layer(x_prompt, mk_p, mv_p, zc, zn, zm, zs, p, fin,
                                   m_chunk=128, g_chunk=64, g_sub=8, tb=512, bb=1, tq=512)

    tpad = -(-ts // SUBLANES) * SUBLANES
    ys, (cs, ns, ms, ss) = _layer(x_sample, cache_mem_k[l], cache_mem_v[l],
                                  state_mlstm_C[l], state_mlstm_n[l], state_mlstm_m[l], state_gla_S[l],
                                  p, fin, m_chunk=tpad, g_chunk=tpad, g_sub=tpad, tb=tpad, bb=8, tq=None)

    st = lambda a: a[None]
    return (yp, ys, st(cp), st(np_), st(mp), st(sp),
            st(mk_p.reshape(bp, N_MEM, HEADS, C_HD)), st(mv_p.reshape(bp, N_MEM, HEADS, C_HD)),
            st(cs), st(ns), st(ms), st(ss))
```

```python
import functools
import math

import jax
import jax.numpy as jnp
from jax import lax
from jax.experimental import pallas as pl
from jax.experimental.pallas import tpu as pltpu

F32 = jnp.float32
BF16 = jnp.bfloat16

D_MODEL = 1024
D_FF = 2816
HEADS = 4
DK = 128
DV = 256
QK = HEADS * DK
VW = HEADS * DV
G_RANK = 16
G_TAU = 16.0
N_MEM = 256
C_HD = D_MODEL // HEADS
EPS = 1e-6
LOG2E = math.log2(math.e)
LANES = 128
SUBLANES = 8

Z_MQ = 0
Z_MK = Z_MQ + QK
Z_MV = Z_MK + QK
Z_GQ = Z_MV + VW
Z_GK = Z_GQ + QK
Z_GV = Z_GK + QK
Z_MO = Z_GV + VW
Z_GR = Z_MO + VW
Z_GATE_M = Z_GR + VW
Z_GATE_G = Z_GATE_M + D_MODEL
Z_WIDTH = Z_GATE_G + D_MODEL
SM_I = 0
SM_F = HEADS
SM_A = 2 * HEADS

VMEM_LIMIT = 56 * 1024 * 1024


def _cparams(sem):
    return pltpu.CompilerParams(dimension_semantics=sem, vmem_limit_bytes=VMEM_LIMIT)


def _rms(x, g):
    return x * lax.rsqrt(jnp.mean(x * x, axis=-1, keepdims=True) + EPS) * g


def _sigmoid(x):
    return 1.0 / (1.0 + jnp.exp(-x))


def _log_sigmoid(x):
    return jnp.minimum(x, 0.0) - jnp.log(1.0 + jnp.exp(-jnp.abs(x)))


def _dot(a, b):
    return jnp.dot(a, b, preferred_element_type=F32)


def _dot_nt(a, b):
    return lax.dot_general(a, b, (((1,), (1,)), ((), ())), preferred_element_type=F32)


def _split3(x):
    hi = x.astype(BF16)
    r1 = x - hi.astype(F32)
    mid = r1.astype(BF16)
    lo = (r1 - mid.astype(F32)).astype(BF16)
    return hi, mid, lo


def _dot_exact_left(m_bf16, x, terms=3):
    return sum(_dot(m_bf16, p) for p in _split3(x)[:terms])


def _resident(shape):
    zeros = (0,) * len(shape)
    return pl.BlockSpec(shape, lambda *_: zeros, pipeline_mode=pl.Buffered(1))


def _ffn_kernel(*refs, final, fused_res):
    refs = list(refs)
    if fused_res:
        a_ref, wr_ref = refs[:2]
        refs = refs[2:]
    x_ref, g_ref, wg_ref, wu_ref, wd_ref = refs[:5]
    refs = refs[5:]
    if final:
        fg_ref = refs.pop(0)
    (o_ref,) = refs
    x = x_ref[...]
    if fused_res:
        x = x + _dot(a_ref[...], wr_ref[...])
    hn = _rms(x, g_ref[...]).astype(BF16)
    a = _dot(hn, wg_ref[...])
    u = _dot(hn, wu_ref[...])
    act = (a * _sigmoid(a) * u).astype(BF16)
    y = x + 0.5 * _dot(act, wd_ref[...])
    if final:
        y = _rms(y, fg_ref[...])
    o_ref[...] = y


def _ffn(x, g, wg, wu, wd, final_g=None, res_a=None, res_w=None, tm=512):
    n = x.shape[0]
    tm = min(tm, n)
    assert n % tm == 0
    final = final_g is not None
    fused_res = res_a is not None
    row = pl.BlockSpec((tm, D_MODEL), lambda i: (i, 0))
    in_specs, args = [], []
    if fused_res:
        in_specs += [row, _resident((D_MODEL, D_MODEL))]
        args += [res_a, res_w]
    in_specs += [row, _resident((1, D_MODEL)), _resident((D_MODEL, D_FF)), _resident((D_MODEL, D_FF)),
                 _resident((D_FF, D_MODEL))]
    args += [x, g, wg, wu, wd]
    if final:
        in_specs.append(_resident((1, D_MODEL)))
        args.append(final_g)
    return pl.pallas_call(
        functools.partial(_ffn_kernel, final=final, fused_res=fused_res),
        grid=(n // tm,),
        in_specs=in_specs,
        out_specs=row,
        out_shape=jax.ShapeDtypeStruct((n, D_MODEL), F32),
        compiler_params=_cparams(("parallel",)),
        name="ffn_final" if final else "ffn",
    )(*args)


def _normproj_kernel(*refs, side):
    if side:
        x_ref, g_ref, w_ref, ws_ref, o_ref, s_ref, hn_ref = refs
    else:
        x_ref, g_ref, w_ref, o_ref, hn_ref = refs

    @pl.when(pl.program_id(1) == 0)
    def _():
        hn = _rms(x_ref[...], g_ref[...]).astype(BF16)
        hn_ref[...] = hn
        if side:
            s_ref[...] = _dot(hn, ws_ref[...])

    o_ref[...] = _dot(hn_ref[...], w_ref[...]).astype(o_ref.dtype)


def _normproj(x, g, w, out_dtype, tn, w_side=None, tm=512, name="normproj"):
    n = x.shape[0]
    width = w.shape[1]
    tm = min(tm, n)
    assert n % tm == 0 and width % tn == 0
    side = w_side is not None
    in_specs = [
        pl.BlockSpec((tm, D_MODEL), lambda i, j: (i, 0)),
        pl.BlockSpec((1, D_MODEL), lambda i, j: (0, 0)),
        _resident((D_MODEL, tn)) if tn == width else pl.BlockSpec((D_MODEL, tn), lambda i, j: (0, j)),
    ]
    out_specs = [pl.BlockSpec((tm, tn), lambda i, j: (i, j))]
    out_shape = [jax.ShapeDtypeStruct((n, width), out_dtype)]
    args = [x, g, w]
    if side:
        ws = w_side.shape[1]
        in_specs.append(pl.BlockSpec((D_MODEL, ws), lambda i, j: (0, 0)))
        out_specs.append(pl.BlockSpec((tm, ws), lambda i, j: (i, 0)))
        out_shape.append(jax.ShapeDtypeStruct((n, ws), F32))
        args.append(w_side)
    out = pl.pallas_call(
        functools.partial(_normproj_kernel, side=side),
        grid=(n // tm, width // tn),
        in_specs=in_specs,
        out_specs=out_specs,
        out_shape=out_shape,
        scratch_shapes=[pltpu.VMEM((tm, D_MODEL), BF16)],
        compiler_params=_cparams(("parallel", "arbitrary")),
        name=name,
    )(*args)
    return out if side else out[0]


def _merge_kernel(gm_ref, gg_ref, hm_ref, hg_ref, x_ref, wm_ref, wg_ref, wo_ref, gq_ref, wq_ref, o_ref, q_ref):
    ym = _dot(hm_ref[...], wm_ref[...])
    yg = _dot(hg_ref[...], wg_ref[...])
    y = _sigmoid(gm_ref[...].astype(F32)) * ym + _sigmoid(gg_ref[...].astype(F32)) * yg
    x = x_ref[...] + _dot(y.astype(BF16), wo_ref[...])
    o_ref[...] = x
    q_ref[...] = _dot(_rms(x, gq_ref[...]).astype(BF16), wq_ref[...]).astype(q_ref.dtype)


def _merge(z, hm, hg, x, wm, wg, wo, gq, wq, tm=512):
    n = x.shape[0]
    tm = min(tm, n)
    assert n % tm == 0
    row = lambda i: (i, 0)
    return pl.pallas_call(
        _merge_kernel,
        grid=(n // tm,),
        in_specs=[
            pl.BlockSpec((tm, D_MODEL), lambda i: (i, Z_GATE_M // D_MODEL)),
            pl.BlockSpec((tm, D_MODEL), lambda i: (i, Z_GATE_G // D_MODEL)),
            pl.BlockSpec((tm, VW), row),
            pl.BlockSpec((tm, VW), row),
            pl.BlockSpec((tm, D_MODEL), row),
            _resident((VW, D_MODEL)),
            _resident((VW, D_MODEL)),
            _resident((D_MODEL, D_MODEL)),
            _resident((1, D_MODEL)),
            _resident((D_MODEL, D_MODEL)),
        ],
        out_specs=[pl.BlockSpec((tm, D_MODEL), row), pl.BlockSpec((tm, D_MODEL), row)],
        out_shape=[jax.ShapeDtypeStruct((n, D_MODEL), F32), jax.ShapeDtypeStruct((n, D_MODEL), BF16)],
        compiler_params=_cparams(("parallel",)),
        name="merge",
    )(z, z, hm, hg, x, wm, wg, wo, gq, wq)


def _attn_kernel(q_ref, k_ref, v_ref, o_ref):
    scale = C_HD ** -0.5
    cols = [slice(h * C_HD, (h + 1) * C_HD) for h in range(HEADS)]
    scores = [_dot_nt(q_ref[0, :, cs], k_ref[0, :, cs].astype(BF16)) for cs in cols]
    probs = []
    for s in scores:
        s = s * scale
        p = jnp.exp(s - jnp.max(s, axis=-1, keepdims=True))
        probs.append((p / jnp.sum(p, axis=-1, keepdims=True)).astype(BF16))
    outs = [_dot(p, v_ref[0, :, cs].astype(BF16)) for p, cs in zip(probs, cols)]
    for o, cs in zip(outs, cols):
        o_ref[0, :, cs] = o.astype(o_ref.dtype)


def _attn(q, k, v, tq):
    b, t, _ = q.shape
    assert t % tq == 0
    return pl.pallas_call(
        _attn_kernel,
        grid=(b, t // tq),
        in_specs=[
            pl.BlockSpec((1, tq, D_MODEL), lambda i, j: (i, j, 0)),
            pl.BlockSpec((1, N_MEM, D_MODEL), lambda i, j: (i, 0, 0)),
            pl.BlockSpec((1, N_MEM, D_MODEL), lambda i, j: (i, 0, 0)),
        ],
        out_specs=pl.BlockSpec((1, tq, D_MODEL), lambda i, j: (i, j, 0)),
        out_shape=jax.ShapeDtypeStruct((b, t, D_MODEL), BF16),
        compiler_params=_cparams(("parallel", "arbitrary")),
        name="attn",
    )(q, k, v)


CACHE_HALVES = C_HD // LANES
CACHE_ROWS = HEADS * CACHE_HALVES


def _attn_cache_kernel(q_ref, k_ref, v_ref, o_ref, *, t):
    scale = C_HD ** -0.5
    nq = t * HEADS
    lane = lax.broadcasted_iota(jnp.int32, (nq, LANES), 1)
    rowi = lax.broadcasted_iota(jnp.int32, (nq, LANES), 0)
    valid = ((lane % CACHE_ROWS) // HEADS == 0) & (lane % HEADS == rowi % HEADS)
    n_tiles = N_MEM * CACHE_ROWS // LANES
    seqs = range(q_ref.shape[0])
    scores = [_dot_nt(q_ref[bi], k_ref[bi].astype(BF16)) for bi in seqs]
    probs = []
    for s in scores:
        tiles = []
        for j in range(n_tiles):
            cs = slice(j * LANES, (j + 1) * LANES)
            sj = s[0:nq, cs] + pltpu.roll(s[nq:2 * nq, cs], LANES - HEADS, 1)
            tiles.append(jnp.where(valid, sj * scale, -jnp.inf))
        mx = functools.reduce(jnp.maximum, [jnp.max(x, axis=1, keepdims=True) for x in tiles])
        ps = [jnp.exp(x - mx) for x in tiles]
        den = functools.reduce(jnp.add, [jnp.sum(x, axis=1, keepdims=True) for x in ps])
        inv = 1.0 / den
        p0 = jnp.concatenate([x * inv for x in ps], axis=1)
        p1 = jnp.concatenate([pltpu.roll(x * inv, HEADS, 1) for x in ps], axis=1)
        probs.append(jnp.concatenate([p0, p1], axis=0).astype(BF16))
    outs = [_dot(p, v_ref[bi].astype(BF16)) for p, bi in zip(probs, seqs)]
    for o, bi in zip(outs, seqs):
        o_ref[bi] = o.astype(o_ref.dtype)


def _attn_cache(q, k, v, bb):
    assert CACHE_HALVES == 2
    b, t, _ = q.shape
    assert b % bb == 0
    nq = t * HEADS
    qv = q.reshape(b, t, HEADS, CACHE_HALVES, LANES).transpose(0, 3, 1, 2, 4)
    qv = qv.reshape(b, CACHE_HALVES * nq, LANES)

    def cache_view(a):
        a = a.reshape(b, N_MEM, HEADS, CACHE_HALVES, LANES).transpose(0, 1, 3, 2, 4)
        return a.reshape(b, N_MEM * CACHE_ROWS, LANES)

    o = pl.pallas_call(
        functools.partial(_attn_cache_kernel, t=t),
        grid=(b // bb,),
        in_specs=[
            pl.BlockSpec((bb, CACHE_HALVES * nq, LANES), lambda i: (i, 0, 0)),
            pl.BlockSpec((bb, N_MEM * CACHE_ROWS, LANES), lambda i: (i, 0, 0)),
            pl.BlockSpec((bb, N_MEM * CACHE_ROWS, LANES), lambda i: (i, 0, 0)),
        ],
        out_specs=pl.BlockSpec((bb, CACHE_HALVES * nq, LANES), lambda i: (i, 0, 0)),
        out_shape=jax.ShapeDtypeStruct((b, CACHE_HALVES * nq, LANES), BF16),
        compiler_params=_cparams(("parallel",)),
        name="attn_cache",
    )(qv, cache_view(k), cache_view(v))
    return o.reshape(b, CACHE_HALVES, t, HEADS, LANES).transpose(0, 2, 3, 1, 4).reshape(b, t, D_MODEL)


def _tile_lanes(rep, width):
    if width <= LANES:
        return rep[:, :width]
    return jnp.concatenate([rep] * (width // LANES), axis=1)


def _mlstm_kernel(q_ref, k_ref, v_ref, og_ref, sm_ref, bias_ref, gn_ref, c0_ref, n0_ref, m0_ref,
                  h_ref, c1_ref, n1_ref, m1_ref, *scratch, chunk, t_valid, carried):
    L = chunk
    bb, tb = q_ref.shape[0], q_ref.shape[1]
    scale = DK ** -0.5

    if carried:
        c_scr, n_scr, m_scr = scratch

        @pl.when(pl.program_id(1) == 0)
        def _():
            c_scr[...] = c0_ref[0]
            n_scr[...] = n0_ref[0]
            m_scr[...] = m0_ref[0]

    row = lax.broadcasted_iota(jnp.int32, (L, L), 0)
    col = lax.broadcasted_iota(jnp.int32, (L, L), 1)
    tri = col <= row
    tri_lo = jnp.where(tri, 1.0, 0.0).astype(BF16)
    lane = lax.broadcasted_iota(jnp.int32, (L, LANES), 1)
    spread = jnp.where(lax.broadcasted_iota(jnp.int32, (LANES, 2 * HEADS * LANES), 0)
                       == lax.broadcasted_iota(jnp.int32, (LANES, 2 * HEADS * LANES), 1) // LANES,
                       1.0, 0.0).astype(BF16)
    pick = jnp.where(lax.broadcasted_iota(jnp.int32, (SUBLANES, LANES), 0)
                     == lax.broadcasted_iota(jnp.int32, (SUBLANES, LANES), 1), 1.0, 0.0).astype(BF16)

    def chunk_group(bis, r):
        rows = pl.ds(r, L)
        cols_of, rows_of = {}, {}
        for bi in bis:
            g = sm_ref[bi, rows, :] + bias_ref[...]
            lf_all = _log_sigmoid(g)
            ig_all = g
            if t_valid < L:
                valid = lax.broadcasted_iota(jnp.int32, (L, 1), 0) < t_valid
                lf_all = jnp.where(valid, lf_all, 0.0)
                ig_all = jnp.where(valid, g, -1e30)
            f_all = _dot_exact_left(tri_lo, lf_all)
            x = jnp.where(lane < SM_F, ig_all, f_all) * LOG2E
            parts = _split3(x)
            cols_of[bi] = sum(_dot(p, spread) for p in parts)
            rows_of[bi] = sum(_dot_nt(pick, p) for p in parts)
        pairs = [(bi, h) for bi in bis for h in range(HEADS)]

        def state(bi):
            if carried:
                return c_scr, n_scr, m_scr, c_scr, n_scr, m_scr
            return (c0_ref.at[bi], n0_ref.at[bi], m0_ref.at[bi], c1_ref.at[bi], n1_ref.at[bi], m1_ref.at[bi])

        st = {}
        for bi, h in pairs:
            c_in, n_in, m_in = state(bi)[:3]
            slab = lambda c, bi=bi: cols_of[bi][:, c * LANES:(c + 1) * LANES]
            f_rep = slab(SM_F + h)
            ig_rep = slab(SM_I + h)
            f_row = rows_of[bi][SM_F + h:SM_F + h + 1, :]
            ig_row = rows_of[bi][SM_I + h:SM_I + h + 1, :]
            dmat = jnp.where(tri, _tile_lanes(f_rep, L) - f_row + ig_row, -jnp.inf)
            m_inter = m_in[h:h + 1, :] * LOG2E + f_rep
            m = jnp.maximum(m_inter, jnp.max(dmat, axis=1, keepdims=True))
            w = jnp.exp2(dmat - _tile_lanes(m - math.log2(scale), L))
            a = jnp.exp2(m_inter - m)
            m_end = m[L - 1:L, :]
            w_rep = jnp.exp2(f_rep[L - 1:L, :] - f_rep + ig_rep - (m_end - math.log2(scale)))
            st[bi, h] = dict(m=m, w=w, a=a, m_end=m_end, w_rep=w_rep, c0=c_in[h], n0=n_in[h:h + 1, :],
                             qb=q_ref[bi, rows, h * DK:(h + 1) * DK], kb=k_ref[bi, rows, h * DK:(h + 1) * DK],
                             vb=v_ref[bi, rows, h * DV:(h + 1) * DV], og=og_ref[bi, rows, h * DV:(h + 1) * DV])
        for key in pairs:
            d = st[key]
            d["s"] = _dot_nt(d["qb"], d["kb"]) * d["w"]
        for key in pairs:
            d = st[key]
            d["inter"] = _dot_nt(d["qb"], d["c0"].astype(BF16))
        for key in pairs:
            d = st[key]
            d["sv"] = _dot(d["s"].astype(BF16), d["vb"])
        for key in pairs:
            d = st[key]
            vw_t = (d["vb"].astype(F32) * _tile_lanes(d["w_rep"], DV)).T.astype(BF16)
            d["c_new"] = d["a"][L - 1:L, :] * d["c0"] + _dot(vw_t, d["kb"])
        for bi, h in pairs:
            d = st[bi, h]
            a, m = d["a"], d["m"]
            qf = d["qb"].astype(F32)
            kf = d["kb"].astype(F32)
            num = _tile_lanes(a, DV) * d["inter"] + d["sv"]
            den = a * jnp.sum(qf * d["n0"], axis=1, keepdims=True) + jnp.sum(d["s"], axis=1, keepdims=True)
            hh = num / _tile_lanes(jnp.maximum(jnp.abs(den), jnp.exp2(-m)), DV)
            d["n_new"] = a[L - 1:L, :] * d["n0"] + jnp.sum(kf * d["w_rep"], axis=0, keepdims=True)
            hn = hh * lax.rsqrt(jnp.mean(hh * hh, axis=-1, keepdims=True) + EPS)
            hn = hn * gn_ref[:, h * DV:(h + 1) * DV] * _sigmoid(d["og"].astype(F32))
            d["hn"] = hn.astype(h_ref.dtype)
        for bi, h in pairs:
            d = st[bi, h]
            c_out, n_out, m_out = state(bi)[3:]
            c_out[h] = d["c_new"]
            n_out[h:h + 1, :] = d["n_new"]
            m_out[h:h + 1, :] = d["m_end"] * (1.0 / LOG2E)
            h_ref[bi, rows, h * DV:(h + 1) * DV] = d["hn"]

    if tb == L:
        chunk_group(list(range(bb)), 0)
    else:
        assert bb == 1

        def loop_body(ci, carry):
            chunk_group([0], pl.multiple_of(ci * L, L))
            return carry

        lax.fori_loop(0, tb // L, loop_body, 0, unroll=2)

    if carried:
        @pl.when(pl.program_id(1) == pl.num_programs(1) - 1)
        def _():
            c1_ref[0] = c_scr[...]
            n1_ref[0] = n_scr[...]
            m1_ref[0] = m_scr[...]


def _mixer_specs():
    bt = lambda col: (lambda i, j: (i, j, col))
    return bt, (lambda i, j: (i, 0, 0, 0)), (lambda i, j: (i, 0, 0))


def _mlstm(z3, sm3, bias_row, gn, c0, n0, m0, chunk, tb, bb, t_valid):
    b, t, _ = z3.shape
    assert t % tb == 0 and tb % chunk == 0 and b % bb == 0
    assert t_valid == t or (tb == t and chunk == t)
    carried = t // tb > 1
    assert not (carried and bb > 1)
    m0 = jnp.broadcast_to(m0[:, :, None], (b, HEADS, LANES))
    bt, st4, st3 = _mixer_specs()
    scratch = [pltpu.VMEM((HEADS, DV, DK), F32), pltpu.VMEM((HEADS, DK), F32),
               pltpu.VMEM((HEADS, LANES), F32)] if carried else []
    h, c1, n1, m1 = pl.pallas_call(
        functools.partial(_mlstm_kernel, chunk=chunk, t_valid=t_valid, carried=carried),
        grid=(b // bb, t // tb),
        in_specs=[
            pl.BlockSpec((bb, tb, QK), bt(Z_MQ // QK)),
            pl.BlockSpec((bb, tb, QK), bt(Z_MK // QK)),
            pl.BlockSpec((bb, tb, VW), bt(Z_MV // VW)),
            pl.BlockSpec((bb, tb, VW), bt(Z_MO // VW)),
            pl.BlockSpec((bb, tb, LANES), bt(0)),
            pl.BlockSpec((1, LANES), lambda i, j: (0, 0)),
            pl.BlockSpec((1, VW), lambda i, j: (0, 0)),
            pl.BlockSpec((bb, HEADS, DV, DK), st4),
            pl.BlockSpec((bb, HEADS, DK), st3),
            pl.BlockSpec((bb, HEADS, LANES), st3),
        ],
        out_specs=[
            pl.BlockSpec((bb, tb, VW), lambda i, j: (i, j, 0)),
            pl.BlockSpec((bb, HEADS, DV, DK), st4),
            pl.BlockSpec((bb, HEADS, DK), st3),
            pl.BlockSpec((bb, HEADS, LANES), st3),
        ],
        out_shape=[
            jax.ShapeDtypeStruct((b, t, VW), BF16),
            jax.ShapeDtypeStruct((b, HEADS, DV, DK), F32),
            jax.ShapeDtypeStruct((b, HEADS, DK), F32),
            jax.ShapeDtypeStruct((b, HEADS, LANES), F32),
        ],
        scratch_shapes=scratch,
        compiler_params=_cparams(("parallel", "arbitrary")),
        name="mlstm",
    )(z3, z3, z3, z3, sm3, bias_row, gn, c0, n0, m0)
    return h, c1, n1, m1[:, :, 0]


def _gla_kernel(q_ref, k_ref, v_ref, r_ref, sm_ref, wa_ref, ba_ref, gn_ref, s0_ref,
                h_ref, s1_ref, *scratch, chunk, sub, t_valid, carried):
    L = chunk
    nb = L // sub
    bb, tb = q_ref.shape[0], q_ref.shape[1]
    scale = DK ** -0.5

    if carried:
        s_scr, b_scr, kf_scr = scratch

        @pl.when(pl.program_id(1) == 0)
        def _():
            s_scr[...] = s0_ref[0]
    else:
        b_scr, kf_scr = scratch

    row = lax.broadcasted_iota(jnp.int32, (L, L), 0)
    col = lax.broadcasted_iota(jnp.int32, (L, L), 1)
    tri_lo = jnp.where(col <= row, 1.0, 0.0).astype(BF16)
    in_blk = (col // sub) == (row // sub)
    trow = lax.broadcasted_iota(jnp.int32, (sub, DK), 0)
    eye = lax.broadcasted_iota(jnp.int32, (DK, DK), 0) == lax.broadcasted_iota(jnp.int32, (DK, DK), 1)
    place = jnp.where(lax.broadcasted_iota(jnp.int32, (sub * DK, LANES), 0) // DK
                      == lax.broadcasted_iota(jnp.int32, (sub * DK, LANES), 1) % sub, 1.0, 0.0).astype(BF16)
    wa_hi = wa_ref[...].astype(BF16)
    wa_lo = (wa_ref[...] - wa_hi.astype(F32)).astype(BF16)

    def stage(bi):
        sm = sm_ref[bi]
        sm_hi = sm.astype(BF16)
        sm_lo = (sm - sm_hi.astype(F32)).astype(BF16)
        a_raw = _dot(sm_hi, wa_hi) + _dot(sm_lo, wa_hi) + _dot(sm_hi, wa_lo) + ba_ref[...]
        la = _log_sigmoid(a_raw) * (LOG2E / G_TAU)
        if t_valid < L:
            valid = lax.broadcasted_iota(jnp.int32, (L, 1), 0) < t_valid
            la = jnp.where(valid, la, 0.0)
        for c in range(tb // L):
            b_scr[bi, c * L:(c + 1) * L, :] = _dot_exact_left(tri_lo, la[c * L:(c + 1) * L, :], terms=2)
        kf_scr[bi] = k_ref[bi].astype(F32)

    def chunk_group(bis, r):
        rows = pl.ds(r, L)
        pairs = [(bi, h) for bi in bis for h in range(HEADS)]

        def state(bi):
            return (s_scr, s_scr) if carried else (s0_ref.at[bi], s1_ref.at[bi])

        slabs = []
        for bi, h in pairs:
            ks = slice(h * DK, (h + 1) * DK)
            qf = q_ref[bi, rows, ks].astype(F32) * scale
            for i in range(nb):
                blk = pl.ds(pl.multiple_of(r + i * sub, sub), sub)
                bs = b_scr[bi, blk, ks]
                kblk = kf_scr[bi, blk, ks]
                qs = qf[i * sub:(i + 1) * sub, :]
                row_slabs = []
                for s in range(sub):
                    e = jnp.exp2(jnp.where(trow >= s, bs - bs[s:s + 1, :], -jnp.inf))
                    row_slabs.append(e * qs * kblk[s:s + 1, :])
                slabs.append(jnp.concatenate(row_slabs, axis=1))
        own = _dot(jnp.concatenate(slabs, axis=0).astype(BF16), place)

        st = {}
        for n, (bi, h) in enumerate(pairs):
            ks = slice(h * DK, (h + 1) * DK)
            vs = slice(h * DV, (h + 1) * DV)
            bh = b_scr[bi, rows, ks]
            qf = q_ref[bi, rows, ks].astype(F32) * scale
            kf = kf_scr[bi, rows, ks]
            s0 = state(bi)[0][h]
            d = dict(vb=v_ref[bi, rows, vs], rg=r_ref[bi, rows, vs], s0=s0, s0b=s0.astype(BF16),
                     qhat=(qf * jnp.exp2(bh)).astype(BF16), a_own=own[n * L:(n + 1) * L, 0:L])
            d["qk"] = []
            for i in range(1, nb):
                r0 = i * sub
                b_ref = bh[r0:r0 + 1, :]
                qt = qf[r0:r0 + sub, :] * jnp.exp2(bh[r0:r0 + sub, :] - b_ref)
                kt = kf[0:r0, :] * jnp.exp2(b_ref - bh[0:r0, :])
                kt = jnp.concatenate([kt, jnp.zeros((L - r0, DK), F32)], axis=0)
                d["qk"].append((qt.astype(BF16), kt.astype(BF16)))
            b_end = bh[L - 1:L, :]
            d["e_col"] = jnp.sum(jnp.where(eye, jnp.exp2(b_end), 0.0), axis=1, keepdims=True)
            d["ke_t"] = (kf * jnp.exp2(b_end - bh)).T.astype(BF16)
            st[bi, h] = d
        for key in pairs:
            d = st[key]
            d["o"] = _dot(d["qhat"], d["s0b"])
        for key in pairs:
            d = st[key]
            d["blocks"] = [_dot_nt(qt, kt) for qt, kt in d["qk"]]
        for key in pairs:
            d = st[key]
            d["s_new"] = d["e_col"] * d["s0"] + _dot(d["ke_t"], d["vb"])
        for key in pairs:
            d = st[key]
            if nb > 1:
                below = jnp.concatenate([jnp.zeros((sub, L), F32)] + d["blocks"], axis=0)
                a_intra = jnp.where(in_blk, d["a_own"], below)
                d["o"] = d["o"] + _dot(a_intra.astype(BF16), d["vb"])
            else:
                d["o"] = d["o"] + _dot(d["a_own"], d["vb"].astype(F32))
        for bi, h in pairs:
            d = st[bi, h]
            o = d["o"]
            hn = o * lax.rsqrt(jnp.mean(o * o, axis=-1, keepdims=True) + EPS)
            rg = d["rg"].astype(F32)
            hn = hn * gn_ref[:, h * DV:(h + 1) * DV] * (rg * _sigmoid(rg))
            d["hn"] = hn.astype(h_ref.dtype)
        for bi, h in pairs:
            d = st[bi, h]
            state(bi)[1][h] = d["s_new"]
            h_ref[bi, rows, h * DV:(h + 1) * DV] = d["hn"]

    for bi in range(bb):
        stage(bi)
    if tb == L:
        chunk_group(list(range(bb)), 0)
    else:
        assert bb == 1

        def loop_body(ci, carry):
            chunk_group([0], pl.multiple_of(ci * L, L))
            return carry

        lax.fori_loop(0, tb // L, loop_body, 0, unroll=2)

    if carried:
        @pl.when(pl.program_id(1) == pl.num_programs(1) - 1)
        def _():
            s1_ref[0] = s_scr[...]


def _gla(z3, sm3, wa_pad, ba, gn, s0, chunk, sub, tb, bb, t_valid):
    b, t, _ = z3.shape
    assert t % tb == 0 and tb % chunk == 0 and chunk % sub == 0 and b % bb == 0
    assert t_valid == t or (tb == t and chunk == t)
    carried = t // tb > 1
    assert not (carried and bb > 1)
    bt, st4, _ = _mixer_specs()
    scratch = [pltpu.VMEM((bb, tb, QK), F32), pltpu.VMEM((bb, tb, QK), F32)]
    if carried:
        scratch = [pltpu.VMEM((HEADS, DK, DV), F32)] + scratch
    h, s1 = pl.pallas_call(
        functools.partial(_gla_kernel, chunk=chunk, sub=sub, t_valid=t_valid, carried=carried),
        grid=(b // bb, t // tb),
        in_specs=[
            pl.BlockSpec((bb, tb, QK), bt(Z_GQ // QK)),
            pl.BlockSpec((bb, tb, QK), bt(Z_GK // QK)),
            pl.BlockSpec((bb, tb, VW), bt(Z_GV // VW)),
            pl.BlockSpec((bb, tb, VW), bt(Z_GR // VW)),
            pl.BlockSpec((bb, tb, LANES), bt(0)),
            pl.BlockSpec((LANES, QK), lambda i, j: (0, 0)),
            pl.BlockSpec((1, QK), lambda i, j: (0, 0)),
            pl.BlockSpec((1, VW), lambda i, j: (0, 0)),
            pl.BlockSpec((bb, HEADS, DK, DV), st4),
        ],
        out_specs=[
            pl.BlockSpec((bb, tb, VW), lambda i, j: (i, j, 0)),
            pl.BlockSpec((bb, HEADS, DK, DV), st4),
        ],
        out_shape=[
            jax.ShapeDtypeStruct((b, t, VW), BF16),
            jax.ShapeDtypeStruct((b, HEADS, DK, DV), F32),
        ],
        scratch_shapes=scratch,
        compiler_params=_cparams(("parallel", "arbitrary")),
        name="gla",
    )(z3, z3, z3, z3, sm3, wa_pad, ba, gn, s0)
    return h, s1


def _layer(x3, mem_k, mem_v, c0, n0, m0, s0, p, final_norm, m_chunk, g_chunk, g_sub, tb, bb, tq):
    b, t, _ = x3.shape
    n = b * t
    tmix = -(-t // tb) * tb
    x = x3.reshape(n, D_MODEL)
    x = _ffn(x, p["ffn1_norm"], p["ffn1_wg"], p["ffn1_wu"], p["ffn1_wd"])
    z, sm = _normproj(x, p["mix_norm"], p["w_in"], BF16, tn=Z_WIDTH, w_side=p["w_in_side"], tm=512, name="inproj")
    pad_t = lambda a: jnp.pad(a, ((0, 0), (0, tmix - t), (0, 0)))
    z3 = pad_t(z.reshape(b, t, Z_WIDTH))
    sm3 = pad_t(sm.reshape(b, t, LANES))
    hm, c1, n1, m1 = _mlstm(z3, sm3, p["bias_row"], p["mlstm_norm"], c0, n0, m0, m_chunk, tb, bb, t)
    hg, s1 = _gla(z3, sm3, p["wa_pad"], p["gla_ba"], p["gla_norm"], s0, g_chunk, g_sub, tb, bb, t)
    x, q = _merge(z, hm[:, :t].reshape(n, VW), hg[:, :t].reshape(n, VW), x, p["w_br_m"], p["w_br_g"],
                  p["w_out"], p["ca_norm"], p["ca_wq"])
    if mem_k.ndim == 4:
        o = _attn_cache(q.reshape(b, t, D_MODEL), mem_k, mem_v, bb=4)
    else:
        o = _attn(q.reshape(b, t, D_MODEL), mem_k, mem_v, tq)
    y = _ffn(x, p["ffn2_norm"], p["ffn2_wg"], p["ffn2_wu"], p["ffn2_wd"], final_g=final_norm,
             res_a=o.reshape(n, D_MODEL), res_w=p["ca_wo"])
    return y.reshape(b, t, D_MODEL), (c1, n1, m1, s1)


def _permute_w_in(w_in):
    sizes = (QK, QK, VW, VW, HEADS, HEADS, QK, QK, VW, VW, G_RANK, D_MODEL, D_MODEL)
    offs = [0]
    for s in sizes:
        offs.append(offs[-1] + s)
    part = lambda i: w_in[:, offs[i]:offs[i + 1]]
    (mq, mk, mv, mo, mi, mf, gq, gk, gv, gr, ga, gate_m, gate_g) = [part(i) for i in range(len(sizes))]
    pad = jnp.zeros((D_MODEL, LANES - 2 * HEADS - G_RANK), w_in.dtype)
    w_main = jnp.concatenate([mq, mk, mv, gq, gk, gv, mo, gr, gate_m, gate_g], axis=1)
    w_side = jnp.concatenate([mi, mf, ga, pad], axis=1)
    return w_main.astype(BF16), w_side.astype(BF16)


def kernel(x_prompt, x_sample, mem_prompt, state_mlstm_C, state_mlstm_n, state_mlstm_m, state_gla_S, cache_mem_k, cache_mem_v, ffn1_norm, ffn1_wg, ffn1_wu, ffn1_wd, mix_norm, w_in, b_if, gla_wa2, gla_ba, mlstm_norm, gla_norm, w_br_m, w_br_g, w_out, ca_norm, mem_norm, ca_wq, ca_wk, ca_wv, ca_wo, ffn2_norm, ffn2_wg, ffn2_wu, ffn2_wd, final_norm):
    depth = ffn1_norm.shape[0]
    assert depth == 1
    l = 0
    bp, tp, _ = x_prompt.shape
    bs, ts, _ = x_sample.shape
    row = lambda v: v.reshape(1, -1).astype(F32)
    bias_row = jnp.zeros((1, LANES), F32).at[0, :2 * HEADS].set(b_if[l])
    wa_pad = jnp.zeros((LANES, QK), F32).at[SM_A:SM_A + G_RANK, :].set(gla_wa2[l])
    w_in_main, w_in_side = _permute_w_in(w_in[l])
    p = {
        "w_in": w_in_main, "w_in_side": w_in_side,
        "ffn1_norm": row(ffn1_norm[l]), "ffn1_wg": ffn1_wg[l].astype(BF16), "ffn1_wu": ffn1_wu[l].astype(BF16),
        "ffn1_wd": ffn1_wd[l].astype(BF16),
        "mix_norm": row(mix_norm[l]),
        "bias_row": bias_row, "wa_pad": wa_pad, "gla_ba": row(gla_ba[l]),
        "mlstm_norm": row(mlstm_norm[l]), "gla_norm": row(gla_norm[l]),
        "w_br_m": w_br_m[l].astype(BF16), "w_br_g": w_br_g[l].astype(BF16), "w_out": w_out[l].astype(BF16),
        "ca_norm": row(ca_norm[l]), "ca_wq": ca_wq[l].astype(BF16), "ca_wo": ca_wo[l].astype(BF16),
        "ffn2_norm": row(ffn2_norm[l]), "ffn2_wg": ffn2_wg[l].astype(BF16), "ffn2_wu": ffn2_wu[l].astype(BF16),
        "ffn2_wd": ffn2_wd[l].astype(BF16),
    }
    fin = row(final_norm)

    mem2 = mem_prompt.reshape(bp * N_MEM, D_MODEL)
    mk_p = _normproj(mem2, row(mem_norm[l]), ca_wk[l].astype(BF16), F32, tn=D_MODEL, name="memk")
    mv_p = _normproj(mem2, row(mem_norm[l]), ca_wv[l].astype(BF16), F32, tn=D_MODEL, name="memv")
    mk_p = mk_p.reshape(bp, N_MEM, D_MODEL)
    mv_p = mv_p.reshape(bp, N_MEM, D_MODEL)
    zc = jnp.zeros((bp, HEADS, DV, DK), F32)
    zn = jnp.zeros((bp, HEADS, DK), F32)
    zm = jnp.zeros((bp, HEADS), F32)
    zs = jnp.zeros((bp, HEADS, DK, DV), F32)
    yp, (cp, np_, mp, sp) = _layer(x_prompt, mk_p, mv_p, zc, zn, zm, zs, p, fin,
                                   m_chunk=128, g_chunk=64, g_sub=8, tb=512, bb=1, tq=512)

    tpad = -(-ts // SUBLANES) * SUBLANES
    ys, (cs, ns, ms, ss) = _layer(x_sample, cache_mem_k[l], cache_mem_v[l],
                                  state_mlstm_C[l], state_mlstm_n[l], state_mlstm_m[l], state_gla_S[l],
                                  p, fin, m_chunk=tpad, g_chunk=tpad, g_sub=tpad, tb=tpad, bb=8, tq=None)

    st = lambda a: a[None]
    return (yp, ys, st(cp), st(np_), st(mp), st(sp),
            st(mk_p.reshape(bp, N_MEM, HEADS, C_HD)), st(mv_p.reshape(bp, N_MEM, HEADS, C_HD)),
            st(cs), st(ns), st(ms), st(ss))
```

```python
import functools
import math

import jax
import jax.numpy as jnp
from jax import lax
from jax.experimental import pallas as pl
from jax.experimental.pallas import tpu as pltpu

F32 = jnp.float32
BF16 = jnp.bfloat16

D_MODEL = 1024
D_FF = 2816
HEADS = 4
DK = 128
DV = 256
QK = HEADS * DK
VW = HEADS * DV
G_RANK = 16
G_TAU = 16.0
N_MEM = 256
C_HD = D_MODEL // HEADS
EPS = 1e-6
LOG2E = math.log2(math.e)
LANES = 128
SUBLANES = 8

Z_MQ = 0
Z_MK = Z_MQ + QK
Z_MV = Z_MK + QK
Z_GQ = Z_MV + VW
Z_GK = Z_GQ + QK
Z_GV = Z_GK + QK
Z_MO = Z_GV + VW
Z_GR = Z_MO + VW
Z_GATE_M = Z_GR + VW
Z_GATE_G = Z_GATE_M + D_MODEL
Z_WIDTH = Z_GATE_G + D_MODEL
SM_I = 0
SM_F = HEADS
SM_A = 2 * HEADS

VMEM_LIMIT = 56 * 1024 * 1024


def _cparams(sem):
    return pltpu.CompilerParams(dimension_semantics=sem, vmem_limit_bytes=VMEM_LIMIT)


def _rms(x, g):
    return x * lax.rsqrt(jnp.mean(x * x, axis=-1, keepdims=True) + EPS) * g


def _sigmoid(x):
    return 1.0 / (1.0 + jnp.exp(-x))


def _log_sigmoid(x):
    return jnp.minimum(x, 0.0) - jnp.log(1.0 + jnp.exp(-jnp.abs(x)))


def _dot(a, b):
    return jnp.dot(a, b, preferred_element_type=F32)


def _dot_nt(a, b):
    return lax.dot_general(a, b, (((1,), (1,)), ((), ())), preferred_element_type=F32)


def _split3(x):
    hi = x.astype(BF16)
    r1 = x - hi.astype(F32)
    mid = r1.astype(BF16)
    lo = (r1 - mid.astype(F32)).astype(BF16)
    return hi, mid, lo


def _dot_exact_left(m_bf16, x, terms=3):
    return sum(_dot(m_bf16, p) for p in _split3(x)[:terms])


def _resident(shape):
    zeros = (0,) * len(shape)
    return pl.BlockSpec(shape, lambda *_: zeros, pipeline_mode=pl.Buffered(1))


def _ffn_kernel(*refs, final, fused_res):
    refs = list(refs)
    if fused_res:
        a_ref, wr_ref = refs[:2]
        refs = refs[2:]
    x_ref, g_ref, wg_ref, wu_ref, wd_ref = refs[:5]
    refs = refs[5:]
    if final:
        fg_ref = refs.pop(0)
    (o_ref,) = refs
    x = x_ref[...]
    if fused_res:
        x = x + _dot(a_ref[...], wr_ref[...])
    hn = _rms(x, g_ref[...]).astype(BF16)
    a = _dot(hn, wg_ref[...])
    u = _dot(hn, wu_ref[...])
    act = (a * _sigmoid(a) * u).astype(BF16)
    y = x + 0.5 * _dot(act, wd_ref[...])
    if final:
        y = _rms(y, fg_ref[...])
    o_ref[...] = y


def _ffn(x, g, wg, wu, wd, final_g=None, res_a=None, res_w=None, tm=512):
    n = x.shape[0]
    tm = min(tm, n)
    assert n % tm == 0
    final = final_g is not None
    fused_res = res_a is not None
    row = pl.BlockSpec((tm, D_MODEL), lambda i: (i, 0))
    in_specs, args = [], []
    if fused_res:
        in_specs += [row, _resident((D_MODEL, D_MODEL))]
        args += [res_a, res_w]
    in_specs += [row, _resident((1, D_MODEL)), _resident((D_MODEL, D_FF)), _resident((D_MODEL, D_FF)),
                 _resident((D_FF, D_MODEL))]
    args += [x, g, wg, wu, wd]
    if final:
        in_specs.append(_resident((1, D_MODEL)))
        args.append(final_g)
    return pl.pallas_call(
        functools.partial(_ffn_kernel, final=final, fused_res=fused_res),
        grid=(n // tm,),
        in_specs=in_specs,
        out_specs=row,
        out_shape=jax.ShapeDtypeStruct((n, D_MODEL), F32),
        compiler_params=_cparams(("parallel",)),
        name="ffn_final" if final else "ffn",
    )(*args)


def _normproj_kernel(x_ref, g_ref, w_ref, o_ref):
    o_ref[...] = _dot(_rms(x_ref[...], g_ref[...]).astype(BF16), w_ref[...]).astype(o_ref.dtype)


def _normproj(x, g, w, out_dtype, tm=512, name="normproj"):
    n = x.shape[0]
    width = w.shape[1]
    tm = min(tm, n)
    assert n % tm == 0
    return pl.pallas_call(
        _normproj_kernel,
        grid=(n // tm,),
        in_specs=[pl.BlockSpec((tm, D_MODEL), lambda i: (i, 0)), _resident((1, D_MODEL)),
                  _resident((D_MODEL, width))],
        out_specs=pl.BlockSpec((tm, width), lambda i: (i, 0)),
        out_shape=jax.ShapeDtypeStruct((n, width), out_dtype),
        compiler_params=_cparams(("parallel",)),
        name=name,
    )(x, g, w)


def _inproj_kernel(x_ref, g_ref, w_ref, ws_ref, wa_ref, ba_ref, z_ref, s_ref, b_ref, *, chunk):
    tm = x_ref.shape[0]
    half = Z_WIDTH // 2
    hn = _rms(x_ref[...], g_ref[...]).astype(BF16)
    sm = _dot(hn, ws_ref[...])
    s_ref[...] = sm
    wa_hi = wa_ref[...].astype(BF16)
    wa_lo = (wa_ref[...] - wa_hi.astype(F32)).astype(BF16)
    sm_hi = sm.astype(BF16)
    sm_lo = (sm - sm_hi.astype(F32)).astype(BF16)
    a_raw = _dot(sm_hi, wa_hi) + _dot(sm_lo, wa_hi) + _dot(sm_hi, wa_lo) + ba_ref[...]
    z_ref[:, :half] = _dot(hn, w_ref[:, :half]).astype(z_ref.dtype)
    la = _log_sigmoid(a_raw) * (LOG2E / G_TAU)
    span = min(tm, 2 * LANES)
    assert span % chunk == 0 and tm % span == 0
    row = lax.broadcasted_iota(jnp.int32, (span, span), 0)
    col = lax.broadcasted_iota(jnp.int32, (span, span), 1)
    tri = jnp.where((col <= row) & (col // chunk == row // chunk), 1.0, 0.0).astype(BF16)
    for c in range(tm // span):
        b_ref[c * span:(c + 1) * span, :] = _dot_exact_left(tri, la[c * span:(c + 1) * span, :], terms=2)
    z_ref[:, half:] = _dot(hn, w_ref[:, half:]).astype(z_ref.dtype)


def _inproj(x, g, w, w_side, wa_pad, ba, chunk, tm=512):
    n = x.shape[0]
    tm = min(tm, n)
    assert n % tm == 0 and tm % chunk == 0
    row = lambda width: pl.BlockSpec((tm, width), lambda i: (i, 0))
    return pl.pallas_call(
        functools.partial(_inproj_kernel, chunk=chunk),
        grid=(n // tm,),
        in_specs=[row(D_MODEL), _resident((1, D_MODEL)), _resident((D_MODEL, Z_WIDTH)),
                  _resident((D_MODEL, LANES)), _resident((LANES, QK)), _resident((1, QK))],
        out_specs=[row(Z_WIDTH), row(LANES), row(QK)],
        out_shape=[jax.ShapeDtypeStruct((n, Z_WIDTH), BF16), jax.ShapeDtypeStruct((n, LANES), F32),
                   jax.ShapeDtypeStruct((n, QK), F32)],
        compiler_params=_cparams(("parallel",)),
        name="inproj",
    )(x, g, w, w_side, wa_pad, ba)


def _head_norm(h_ref, gn_ref, gate):
    outs = []
    for h in range(HEADS):
        cs = slice(h * DV, (h + 1) * DV)
        o = h_ref[:, cs].astype(F32)
        outs.append(o * lax.rsqrt(jnp.mean(o * o, axis=-1, keepdims=True) + EPS) * gn_ref[:, cs] * gate[:, cs])
    return jnp.concatenate(outs, axis=1).astype(BF16)


def _merge_kernel(og_ref, rg_ref, gm_ref, gg_ref, hm_ref, hg_ref, x_ref, nm_ref, ng_ref, wm_ref, wg_ref, wo_ref,
                  gq_ref, wq_ref, o_ref, q_ref):
    rg = rg_ref[...].astype(F32)
    hm = _head_norm(hm_ref, nm_ref, _sigmoid(og_ref[...].astype(F32)))
    hg = _head_norm(hg_ref, ng_ref, rg * _sigmoid(rg))
    ym = _dot(hm, wm_ref[...])
    yg = _dot(hg, wg_ref[...])
    y = _sigmoid(gm_ref[...].astype(F32)) * ym + _sigmoid(gg_ref[...].astype(F32)) * yg
    x = x_ref[...] + _dot(y.astype(BF16), wo_ref[...])
    o_ref[...] = x
    q_ref[...] = _dot(_rms(x, gq_ref[...]).astype(BF16), wq_ref[...]).astype(q_ref.dtype)


def _merge(z, hm, hg, x, nm, ng, wm, wg, wo, gq, wq, tm=512):
    n = x.shape[0]
    tm = min(tm, n)
    assert n % tm == 0
    row = lambda i: (i, 0)
    zcol = lambda off, width: pl.BlockSpec((tm, width), lambda i: (i, off // width))
    return pl.pallas_call(
        _merge_kernel,
        grid=(n // tm,),
        in_specs=[
            zcol(Z_MO, VW),
            zcol(Z_GR, VW),
            zcol(Z_GATE_M, D_MODEL),
            zcol(Z_GATE_G, D_MODEL),
            pl.BlockSpec((tm, VW), row),
            pl.BlockSpec((tm, VW), row),
            pl.BlockSpec((tm, D_MODEL), row),
            _resident((1, VW)),
            _resident((1, VW)),
            _resident((VW, D_MODEL)),
            _resident((VW, D_MODEL)),
            _resident((D_MODEL, D_MODEL)),
            _resident((1, D_MODEL)),
            _resident((D_MODEL, D_MODEL)),
        ],
        out_specs=[pl.BlockSpec((tm, D_MODEL), row), pl.BlockSpec((tm, D_MODEL), row)],
        out_shape=[jax.ShapeDtypeStruct((n, D_MODEL), F32), jax.ShapeDtypeStruct((n, D_MODEL), BF16)],
        compiler_params=_cparams(("parallel",)),
        name="merge",
    )(z, z, z, z, hm, hg, x, nm, ng, wm, wg, wo, gq, wq)


def _attn_kernel(q_ref, k_ref, v_ref, o_ref):
    scale = C_HD ** -0.5
    cols = [slice(h * C_HD, (h + 1) * C_HD) for h in range(HEADS)]
    scores = [_dot_nt(q_ref[0, :, cs], k_ref[0, :, cs].astype(BF16)) for cs in cols]
    probs = []
    for s in scores:
        s = s * scale
        p = jnp.exp(s - jnp.max(s, axis=-1, keepdims=True))
        probs.append((p / jnp.sum(p, axis=-1, keepdims=True)).astype(BF16))
    outs = [_dot(p, v_ref[0, :, cs].astype(BF16)) for p, cs in zip(probs, cols)]
    for o, cs in zip(outs, cols):
        o_ref[0, :, cs] = o.astype(o_ref.dtype)


def _attn(q, k, v, tq):
    b, t, _ = q.shape
    assert t % tq == 0
    return pl.pallas_call(
        _attn_kernel,
        grid=(b, t // tq),
        in_specs=[
            pl.BlockSpec((1, tq, D_MODEL), lambda i, j: (i, j, 0)),
            pl.BlockSpec((1, N_MEM, D_MODEL), lambda i, j: (i, 0, 0)),
            pl.BlockSpec((1, N_MEM, D_MODEL), lambda i, j: (i, 0, 0)),
        ],
        out_specs=pl.BlockSpec((1, tq, D_MODEL), lambda i, j: (i, j, 0)),
        out_shape=jax.ShapeDtypeStruct((b, t, D_MODEL), BF16),
        compiler_params=_cparams(("parallel", "arbitrary")),
        name="attn",
    )(q, k, v)


CACHE_HALVES = C_HD // LANES
CACHE_ROWS = HEADS * CACHE_HALVES


def _attn_cache_kernel(q_ref, k_ref, v_ref, o_ref, *, t):
    scale = C_HD ** -0.5
    nq = t * HEADS
    lane = lax.broadcasted_iota(jnp.int32, (nq, LANES), 1)
    rowi = lax.broadcasted_iota(jnp.int32, (nq, LANES), 0)
    valid = ((lane % CACHE_ROWS) // HEADS == 0) & (lane % HEADS == rowi % HEADS)
    n_tiles = N_MEM * CACHE_ROWS // LANES
    seqs = range(q_ref.shape[0])
    scores = [_dot_nt(q_ref[bi], k_ref[bi].astype(BF16)) for bi in seqs]
    probs = []
    for s in scores:
        tiles = []
        for j in range(n_tiles):
            cs = slice(j * LANES, (j + 1) * LANES)
            sj = s[0:nq, cs] + pltpu.roll(s[nq:2 * nq, cs], LANES - HEADS, 1)
            tiles.append(jnp.where(valid, sj * scale, -jnp.inf))
        mx = functools.reduce(jnp.maximum, [jnp.max(x, axis=1, keepdims=True) for x in tiles])
        ps = [jnp.exp(x - mx) for x in tiles]
        den = functools.reduce(jnp.add, [jnp.sum(x, axis=1, keepdims=True) for x in ps])
        inv = 1.0 / den
        p0 = jnp.concatenate([x * inv for x in ps], axis=1)
        p1 = jnp.concatenate([pltpu.roll(x * inv, HEADS, 1) for x in ps], axis=1)
        probs.append(jnp.concatenate([p0, p1], axis=0).astype(BF16))
    outs = [_dot(p, v_ref[bi].astype(BF16)) for p, bi in zip(probs, seqs)]
    for o, bi in zip(outs, seqs):
        o_ref[bi] = o.astype(o_ref.dtype)


def _attn_cache(q, k, v, bb):
    assert CACHE_HALVES == 2
    b, t, _ = q.shape
    assert b % bb == 0
    nq = t * HEADS
    qv = q.reshape(b, t, HEADS, CACHE_HALVES, LANES).transpose(0, 3, 1, 2, 4)
    qv = qv.reshape(b, CACHE_HALVES * nq, LANES)

    def cache_view(a):
        a = a.reshape(b, N_MEM, HEADS, CACHE_HALVES, LANES).transpose(0, 1, 3, 2, 4)
        return a.reshape(b, N_MEM * CACHE_ROWS, LANES)

    o = pl.pallas_call(
        functools.partial(_attn_cache_kernel, t=t),
        grid=(b // bb,),
        in_specs=[
            pl.BlockSpec((bb, CACHE_HALVES * nq, LANES), lambda i: (i, 0, 0)),
            pl.BlockSpec((bb, N_MEM * CACHE_ROWS, LANES), lambda i: (i, 0, 0)),
            pl.BlockSpec((bb, N_MEM * CACHE_ROWS, LANES), lambda i: (i, 0, 0)),
        ],
        out_specs=pl.BlockSpec((bb, CACHE_HALVES * nq, LANES), lambda i: (i, 0, 0)),
        out_shape=jax.ShapeDtypeStruct((b, CACHE_HALVES * nq, LANES), BF16),
        compiler_params=_cparams(("parallel",)),
        name="attn_cache",
    )(qv, cache_view(k), cache_view(v))
    return o.reshape(b, CACHE_HALVES, t, HEADS, LANES).transpose(0, 2, 3, 1, 4).reshape(b, t, D_MODEL)


def _tile_lanes(rep, width):
    if width <= LANES:
        return rep[:, :width]
    return jnp.concatenate([rep] * (width // LANES), axis=1)


def _mlstm_kernel(q_ref, k_ref, v_ref, sm_ref, bias_ref, c0_ref, n0_ref, m0_ref,
                  h_ref, c1_ref, n1_ref, m1_ref, *scratch, chunk, t_valid, carried):
    L = chunk
    bb, tb = q_ref.shape[0], q_ref.shape[1]
    scale = DK ** -0.5

    if carried:
        c_scr, n_scr, m_scr = scratch

        @pl.when(pl.program_id(1) == 0)
        def _():
            c_scr[...] = c0_ref[0]
            n_scr[...] = n0_ref[0]
            m_scr[...] = m0_ref[0]

    row = lax.broadcasted_iota(jnp.int32, (L, L), 0)
    col = lax.broadcasted_iota(jnp.int32, (L, L), 1)
    tri = col <= row
    tri_lo = jnp.where(tri, 1.0, 0.0).astype(BF16)
    lane = lax.broadcasted_iota(jnp.int32, (L, LANES), 1)
    spread = jnp.where(lax.broadcasted_iota(jnp.int32, (LANES, 2 * HEADS * LANES), 0)
                       == lax.broadcasted_iota(jnp.int32, (LANES, 2 * HEADS * LANES), 1) // LANES,
                       1.0, 0.0).astype(BF16)
    pick = jnp.where(lax.broadcasted_iota(jnp.int32, (SUBLANES, LANES), 0)
                     == lax.broadcasted_iota(jnp.int32, (SUBLANES, LANES), 1), 1.0, 0.0).astype(BF16)

    def chunk_group(bis, r):
        rows = pl.ds(r, L)
        cols_of, rows_of = {}, {}
        for bi in bis:
            g = sm_ref[bi, rows, :] + bias_ref[...]
            lf_all = _log_sigmoid(g)
            ig_all = g
            if t_valid < L:
                valid = lax.broadcasted_iota(jnp.int32, (L, 1), 0) < t_valid
                lf_all = jnp.where(valid, lf_all, 0.0)
                ig_all = jnp.where(valid, g, -1e30)
            f_all = _dot_exact_left(tri_lo, lf_all)
            x = jnp.where(lane < SM_F, ig_all, f_all) * LOG2E
            parts = _split3(x)
            cols_of[bi] = sum(_dot(p, spread) for p in parts)
            rows_of[bi] = sum(_dot_nt(pick, p) for p in parts)
        pairs = [(bi, h) for bi in bis for h in range(HEADS)]

        def state(bi):
            if carried:
                return c_scr, n_scr, m_scr, c_scr, n_scr, m_scr
            return (c0_ref.at[bi], n0_ref.at[bi], m0_ref.at[bi], c1_ref.at[bi], n1_ref.at[bi], m1_ref.at[bi])

        st = {}
        for bi, h in pairs:
            c_in, n_in, m_in = state(bi)[:3]
            slab = lambda c, bi=bi: cols_of[bi][:, c * LANES:(c + 1) * LANES]
            f_rep = slab(SM_F + h)
            ig_rep = slab(SM_I + h)
            f_row = rows_of[bi][SM_F + h:SM_F + h + 1, :]
            ig_row = rows_of[bi][SM_I + h:SM_I + h + 1, :]
            dmat = jnp.where(tri, _tile_lanes(f_rep, L) - f_row + ig_row, -jnp.inf)
            m_inter = m_in[h:h + 1, :] * LOG2E + f_rep
            m = jnp.maximum(m_inter, jnp.max(dmat, axis=1, keepdims=True))
            w = jnp.exp2(dmat - _tile_lanes(m - math.log2(scale), L))
            a = jnp.exp2(m_inter - m)
            m_end = m[L - 1:L, :]
            w_rep = jnp.exp2(f_rep[L - 1:L, :] - f_rep + ig_rep - (m_end - math.log2(scale)))
            st[bi, h] = dict(m=m, w=w, a=a, m_end=m_end, w_rep=w_rep, c0=c_in[h], n0=n_in[h:h + 1, :],
                             qb=q_ref[bi, rows, h * DK:(h + 1) * DK], kb=k_ref[bi, rows, h * DK:(h + 1) * DK],
                             vb=v_ref[bi, rows, h * DV:(h + 1) * DV])
        for key in pairs:
            d = st[key]
            d["s"] = _dot_nt(d["qb"], d["kb"]) * d["w"]
        for key in pairs:
            d = st[key]
            d["inter"] = _dot_nt(d["qb"], d["c0"].astype(BF16))
        for key in pairs:
            d = st[key]
            d["sv"] = _dot(d["s"].astype(BF16), d["vb"])
        for key in pairs:
            d = st[key]
            vw_t = (d["vb"].astype(F32) * _tile_lanes(d["w_rep"], DV)).T.astype(BF16)
            d["c_new"] = d["a"][L - 1:L, :] * d["c0"] + _dot(vw_t, d["kb"])
        for bi, h in pairs:
            d = st[bi, h]
            a, m = d["a"], d["m"]
            qf = d["qb"].astype(F32)
            kf = d["kb"].astype(F32)
            num = _tile_lanes(a, DV) * d["inter"] + d["sv"]
            den = a * jnp.sum(qf * d["n0"], axis=1, keepdims=True) + jnp.sum(d["s"], axis=1, keepdims=True)
            hh = num / _tile_lanes(jnp.maximum(jnp.abs(den), jnp.exp2(-m)), DV)
            d["n_new"] = a[L - 1:L, :] * d["n0"] + jnp.sum(kf * d["w_rep"], axis=0, keepdims=True)
            d["hn"] = hh.astype(h_ref.dtype)
        for bi, h in pairs:
            d = st[bi, h]
            c_out, n_out, m_out = state(bi)[3:]
            c_out[h] = d["c_new"]
            n_out[h:h + 1, :] = d["n_new"]
            m_out[h:h + 1, :] = d["m_end"] * (1.0 / LOG2E)
            h_ref[bi, rows, h * DV:(h + 1) * DV] = d["hn"]

    if tb == L:
        chunk_group(list(range(bb)), 0)
    else:
        assert bb == 1

        def loop_body(ci, carry):
            chunk_group([0], pl.multiple_of(ci * L, L))
            return carry

        lax.fori_loop(0, tb // L, loop_body, 0, unroll=2)

    if carried:
        @pl.when(pl.program_id(1) == pl.num_programs(1) - 1)
        def _():
            c1_ref[0] = c_scr[...]
            n1_ref[0] = n_scr[...]
            m1_ref[0] = m_scr[...]


def _mixer_specs():
    bt = lambda col: (lambda i, j: (i, j, col))
    return bt, (lambda i, j: (i, 0, 0, 0)), (lambda i, j: (i, 0, 0))


def _mlstm(z3, sm3, bias_row, c0, n0, m0, chunk, tb, bb, t_valid):
    b, t, _ = z3.shape
    assert t % tb == 0 and tb % chunk == 0 and b % bb == 0
    assert t_valid == t or (tb == t and chunk == t)
    carried = t // tb > 1
    assert not (carried and bb > 1)
    m0 = jnp.broadcast_to(m0[:, :, None], (b, HEADS, LANES))
    bt, st4, st3 = _mixer_specs()
    scratch = [pltpu.VMEM((HEADS, DV, DK), F32), pltpu.VMEM((HEADS, DK), F32),
               pltpu.VMEM((HEADS, LANES), F32)] if carried else []
    h, c1, n1, m1 = pl.pallas_call(
        functools.partial(_mlstm_kernel, chunk=chunk, t_valid=t_valid, carried=carried),
        grid=(b // bb, t // tb),
        in_specs=[
            pl.BlockSpec((bb, tb, QK), bt(Z_MQ // QK)),
            pl.BlockSpec((bb, tb, QK), bt(Z_MK // QK)),
            pl.BlockSpec((bb, tb, VW), bt(Z_MV // VW)),
            pl.BlockSpec((bb, tb, LANES), bt(0)),
            pl.BlockSpec((1, LANES), lambda i, j: (0, 0)),
            pl.BlockSpec((bb, HEADS, DV, DK), st4),
            pl.BlockSpec((bb, HEADS, DK), st3),
            pl.BlockSpec((bb, HEADS, LANES), st3),
        ],
        out_specs=[
            pl.BlockSpec((bb, tb, VW), lambda i, j: (i, j, 0)),
            pl.BlockSpec((bb, HEADS, DV, DK), st4),
            pl.BlockSpec((bb, HEADS, DK), st3),
            pl.BlockSpec((bb, HEADS, LANES), st3),
        ],
        out_shape=[
            jax.ShapeDtypeStruct((b, t, VW), BF16),
            jax.ShapeDtypeStruct((b, HEADS, DV, DK), F32),
            jax.ShapeDtypeStruct((b, HEADS, DK), F32),
            jax.ShapeDtypeStruct((b, HEADS, LANES), F32),
        ],
        scratch_shapes=scratch,
        compiler_params=_cparams(("parallel", "arbitrary")),
        name="mlstm",
    )(z3, z3, z3, sm3, bias_row, c0, n0, m0)
    return h, c1, n1, m1[:, :, 0]


def _gla_kernel(q_ref, k_ref, v_ref, b_ref, s0_ref, h_ref, s1_ref, *scratch, chunk, sub, carried):
    L = chunk
    nb = L // sub
    bb, tb = q_ref.shape[0], q_ref.shape[1]
    scale = DK ** -0.5

    if carried:
        s_scr, kf_scr = scratch

        @pl.when(pl.program_id(1) == 0)
        def _():
            s_scr[...] = s0_ref[0]
    else:
        (kf_scr,) = scratch

    row = lax.broadcasted_iota(jnp.int32, (L, L), 0)
    col = lax.broadcasted_iota(jnp.int32, (L, L), 1)
    in_blk = (col // sub) == (row // sub)
    trow = lax.broadcasted_iota(jnp.int32, (sub, DK), 0)
    eye = lax.broadcasted_iota(jnp.int32, (DK, DK), 0) == lax.broadcasted_iota(jnp.int32, (DK, DK), 1)
    place = jnp.where(lax.broadcasted_iota(jnp.int32, (sub * DK, LANES), 0) // DK
                      == lax.broadcasted_iota(jnp.int32, (sub * DK, LANES), 1) % sub, 1.0, 0.0).astype(BF16)

    def chunk_group(bis, r):
        rows = pl.ds(r, L)
        pairs = [(bi, h) for bi in bis for h in range(HEADS)]

        def state(bi):
            return (s_scr, s_scr) if carried else (s0_ref.at[bi], s1_ref.at[bi])

        slabs = []
        for bi, h in pairs:
            ks = slice(h * DK, (h + 1) * DK)
            qf = q_ref[bi, rows, ks].astype(F32) * scale
            for i in range(nb):
                blk = pl.ds(pl.multiple_of(r + i * sub, sub), sub)
                bs = b_ref[bi, blk, ks]
                kblk = kf_scr[bi, blk, ks]
                qs = qf[i * sub:(i + 1) * sub, :]
                row_slabs = []
                for s in range(sub):
                    e = jnp.exp2(jnp.where(trow >= s, bs - bs[s:s + 1, :], -jnp.inf))
                    row_slabs.append(e * qs * kblk[s:s + 1, :])
                slabs.append(jnp.concatenate(row_slabs, axis=1))
        own = _dot(jnp.concatenate(slabs, axis=0).astype(BF16), place)

        st = {}
        for n, (bi, h) in enumerate(pairs):
            ks = slice(h * DK, (h + 1) * DK)
            vs = slice(h * DV, (h + 1) * DV)
            bh = b_ref[bi, rows, ks]
            qf = q_ref[bi, rows, ks].astype(F32) * scale
            kf = kf_scr[bi, rows, ks]
            s0 = state(bi)[0][h]
            d = dict(vb=v_ref[bi, rows, vs], s0=s0, s0b=s0.astype(BF16),
                     qhat=(qf * jnp.exp2(bh)).astype(BF16), a_own=own[n * L:(n + 1) * L, 0:L])
            d["qk"] = []
            for i in range(1, nb):
                r0 = i * sub
                anchor = bh[r0:r0 + 1, :]
                qt = qf[r0:r0 + sub, :] * jnp.exp2(bh[r0:r0 + sub, :] - anchor)
                kt = kf[0:r0, :] * jnp.exp2(anchor - bh[0:r0, :])
                kt = jnp.concatenate([kt, jnp.zeros((L - r0, DK), F32)], axis=0)
                d["qk"].append((qt.astype(BF16), kt.astype(BF16)))
            b_end = bh[L - 1:L, :]
            d["e_col"] = jnp.sum(jnp.where(eye, jnp.exp2(b_end), 0.0), axis=1, keepdims=True)
            d["ke_t"] = (kf * jnp.exp2(b_end - bh)).T.astype(BF16)
            st[bi, h] = d
        for key in pairs:
            d = st[key]
            d["o"] = _dot(d["qhat"], d["s0b"])
        for key in pairs:
            d = st[key]
            d["blocks"] = [_dot_nt(qt, kt) for qt, kt in d["qk"]]
        for key in pairs:
            d = st[key]
            d["s_new"] = d["e_col"] * d["s0"] + _dot(d["ke_t"], d["vb"])
        for key in pairs:
            d = st[key]
            if nb > 1:
                below = jnp.concatenate([jnp.zeros((sub, L), F32)] + d["blocks"], axis=0)
                a_intra = jnp.where(in_blk, d["a_own"], below)
                d["o"] = d["o"] + _dot(a_intra.astype(BF16), d["vb"])
            else:
                d["o"] = d["o"] + _dot(d["a_own"], d["vb"].astype(F32))
        for bi, h in pairs:
            d = st[bi, h]
            d["hn"] = d["o"].astype(h_ref.dtype)
        for bi, h in pairs:
            d = st[bi, h]
            state(bi)[1][h] = d["s_new"]
            h_ref[bi, rows, h * DV:(h + 1) * DV] = d["hn"]

    for bi in range(bb):
        kf_scr[bi] = k_ref[bi].astype(F32)
    if tb == L:
        chunk_group(list(range(bb)), 0)
    else:
        assert bb == 1

        def loop_body(ci, carry):
            chunk_group([0], pl.multiple_of(ci * L, L))
            return carry

        lax.fori_loop(0, tb // L, loop_body, 0, unroll=2)

    if carried:
        @pl.when(pl.program_id(1) == pl.num_programs(1) - 1)
        def _():
            s1_ref[0] = s_scr[...]


def _gla(z3, b3, s0, chunk, sub, tb, bb):
    b, t, _ = z3.shape
    assert t % tb == 0 and tb % chunk == 0 and chunk % sub == 0 and b % bb == 0
    carried = t // tb > 1
    assert not (carried and bb > 1)
    bt, st4, _ = _mixer_specs()
    scratch = [pltpu.VMEM((bb, tb, QK), F32)]
    if carried:
        scratch = [pltpu.VMEM((HEADS, DK, DV), F32)] + scratch
    h, s1 = pl.pallas_call(
        functools.partial(_gla_kernel, chunk=chunk, sub=sub, carried=carried),
        grid=(b // bb, t // tb),
        in_specs=[
            pl.BlockSpec((bb, tb, QK), bt(Z_GQ // QK)),
            pl.BlockSpec((bb, tb, QK), bt(Z_GK // QK)),
            pl.BlockSpec((bb, tb, VW), bt(Z_GV // VW)),
            pl.BlockSpec((bb, tb, QK), bt(0)),
            pl.BlockSpec((bb, HEADS, DK, DV), st4),
        ],
        out_specs=[
            pl.BlockSpec((bb, tb, VW), lambda i, j: (i, j, 0)),
            pl.BlockSpec((bb, HEADS, DK, DV), st4),
        ],
        out_shape=[
            jax.ShapeDtypeStruct((b, t, VW), BF16),
            jax.ShapeDtypeStruct((b, HEADS, DK, DV), F32),
        ],
        scratch_shapes=scratch,
        compiler_params=_cparams(("parallel", "arbitrary")),
        name="gla",
    )(z3, z3, z3, b3, s0)
    return h, s1


def _layer(x3, mem_k, mem_v, c0, n0, m0, s0, p, final_norm, m_chunk, g_chunk, g_sub, tb, bb, tq):
    b, t, _ = x3.shape
    n = b * t
    tmix = -(-t // tb) * tb
    x = x3.reshape(n, D_MODEL)
    x = _ffn(x, p["ffn1_norm"], p["ffn1_wg"], p["ffn1_wu"], p["ffn1_wd"])
    z, sm, bdec = _inproj(x, p["mix_norm"], p["w_in"], p["w_in_side"], p["wa_pad"], p["gla_ba"],
                          chunk=min(g_chunk, t))
    pad_t = lambda a, mode="constant": jnp.pad(a, ((0, 0), (0, tmix - t), (0, 0)), mode=mode)
    z3 = pad_t(z.reshape(b, t, Z_WIDTH))
    sm3 = pad_t(sm.reshape(b, t, LANES))
    b3 = pad_t(bdec.reshape(b, t, QK), "edge")
    hm, c1, n1, m1 = _mlstm(z3, sm3, p["bias_row"], c0, n0, m0, m_chunk, tb, bb, t)
    hg, s1 = _gla(z3, b3, s0, g_chunk, g_sub, tb, bb)
    x, q = _merge(z, hm[:, :t].reshape(n, VW), hg[:, :t].reshape(n, VW), x, p["mlstm_norm"], p["gla_norm"],
                  p["w_br_m"], p["w_br_g"], p["w_out"], p["ca_norm"], p["ca_wq"])
    if mem_k.ndim == 4:
        o = _attn_cache(q.reshape(b, t, D_MODEL), mem_k, mem_v, bb=4)
    else:
        o = _attn(q.reshape(b, t, D_MODEL), mem_k, mem_v, tq)
    y = _ffn(x, p["ffn2_norm"], p["ffn2_wg"], p["ffn2_wu"], p["ffn2_wd"], final_g=final_norm,
             res_a=o.reshape(n, D_MODEL), res_w=p["ca_wo"])
    return y.reshape(b, t, D_MODEL), (c1, n1, m1, s1)


def _permute_w_in(w_in):
    sizes = (QK, QK, VW, VW, HEADS, HEADS, QK, QK, VW, VW, G_RANK, D_MODEL, D_MODEL)
    offs = [0]
    for s in sizes:
        offs.append(offs[-1] + s)
    part = lambda i: w_in[:, offs[i]:offs[i + 1]]
    (mq, mk, mv, mo, mi, mf, gq, gk, gv, gr, ga, gate_m, gate_g) = [part(i) for i in range(len(sizes))]
    pad = jnp.zeros((D_MODEL, LANES - 2 * HEADS - G_RANK), w_in.dtype)
    w_main = jnp.concatenate([mq, mk, mv, gq, gk, gv, mo, gr, gate_m, gate_g], axis=1)
    w_side = jnp.concatenate([mi, mf, ga, pad], axis=1)
    return w_main.astype(BF16), w_side.astype(BF16)


def kernel(x_prompt, x_sample, mem_prompt, state_mlstm_C, state_mlstm_n, state_mlstm_m, state_gla_S, cache_mem_k, cache_mem_v, ffn1_norm, ffn1_wg, ffn1_wu, ffn1_wd, mix_norm, w_in, b_if, gla_wa2, gla_ba, mlstm_norm, gla_norm, w_br_m, w_br_g, w_out, ca_norm, mem_norm, ca_wq, ca_wk, ca_wv, ca_wo, ffn2_norm, ffn2_wg, ffn2_wu, ffn2_wd, final_norm):
    depth = ffn1_norm.shape[0]
    assert depth == 1
    l = 0
    bp, tp, _ = x_prompt.shape
    bs, ts, _ = x_sample.shape
    row = lambda v: v.reshape(1, -1).astype(F32)
    bias_row = jnp.zeros((1, LANES), F32).at[0, :2 * HEADS].set(b_if[l])
    wa_pad = jnp.zeros((LANES, QK), F32).at[SM_A:SM_A + G_RANK, :].set(gla_wa2[l])
    w_in_main, w_in_side = _permute_w_in(w_in[l])
    p = {
        "w_in": w_in_main, "w_in_side": w_in_side,
        "ffn1_norm": row(ffn1_norm[l]), "ffn1_wg": ffn1_wg[l].astype(BF16), "ffn1_wu": ffn1_wu[l].astype(BF16),
        "ffn1_wd": ffn1_wd[l].astype(BF16),
        "mix_norm": row(mix_norm[l]),
        "bias_row": bias_row, "wa_pad": wa_pad, "gla_ba": row(gla_ba[l]),
        "mlstm_norm": row(mlstm_norm[l]), "gla_norm": row(gla_norm[l]),
        "w_br_m": w_br_m[l].astype(BF16), "w_br_g": w_br_g[l].astype(BF16), "w_out": w_out[l].astype(BF16),
        "ca_norm": row(ca_norm[l]), "ca_wq": ca_wq[l].astype(BF16), "ca_wo": ca_wo[l].astype(BF16),
        "ffn2_norm": row(ffn2_norm[l]), "ffn2_wg": ffn2_wg[l].astype(BF16), "ffn2_wu": ffn2_wu[l].astype(BF16),
        "ffn2_wd": ffn2_wd[l].astype(BF16),
    }
    fin = row(final_norm)

    mem2 = mem_prompt.reshape(bp * N_MEM, D_MODEL)
    mk_p = _normproj(mem2, row(mem_norm[l]), ca_wk[l].astype(BF16), F32, name="memk")
    mv_p = _normproj(mem2, row(mem_norm[l]), ca_wv[l].astype(BF16), F32, name="memv")
    mk_p = mk_p.reshape(bp, N_MEM, D_MODEL)
    mv_p = mv_p.reshape(bp, N_MEM, D_MODEL)
    zc = jnp.zeros((bp, HEADS, DV, DK), F32)
    zn = jnp.zeros((bp, HEADS, DK), F32)
    zm = jnp.zeros((bp, HEADS), F32)
    zs = jnp.zeros((bp, HEADS, DK, DV), F32)
    yp, (cp, np_, mp, sp) = _layer(x_prompt, mk_p, mv_p, zc, zn, zm, zs, p, fin,
                                   m_chunk=128, g_chunk=64, g_sub=8, tb=512, bb=1, tq=512)

    tpad = -(-ts // SUBLANES) * SUBLANES
    ys, (cs, ns, ms, ss) = _layer(x_sample, cache_mem_k[l], cache_mem_v[l],
                                  state_mlstm_C[l], state_mlstm_n[l], state_mlstm_m[l], state_gla_S[l],
                                  p, fin, m_chunk=tpad, g_chunk=tpad, g_sub=tpad, tb=tpad, bb=8, tq=None)

    st = lambda a: a[None]
    return (yp, ys, st(cp), st(np_), st(mp), st(sp),
            st(mk_p.reshape(bp, N_MEM, HEADS, C_HD)), st(mv_p.reshape(bp, N_MEM, HEADS, C_HD)),
            st(cs), st(ns), st(ms), st(ss))
```

```python
import functools
import math

import jax
import jax.numpy as jnp
from jax import lax
from jax.experimental import pallas as pl
from jax.experimental.pallas import tpu as pltpu

F32 = jnp.float32
BF16 = jnp.bfloat16

D_MODEL = 1024
D_FF = 2816
HEADS = 4
DK = 128
DV = 256
QK = HEADS * DK
VW = HEADS * DV
G_RANK = 16
G_TAU = 16.0
N_MEM = 256
C_HD = D_MODEL // HEADS
EPS = 1e-6
LOG2E = math.log2(math.e)
LANES = 128
SUBLANES = 8

Z_MQ = 0
Z_MK = Z_MQ + QK
Z_MV = Z_MK + QK
Z_GQ = Z_MV + VW
Z_GK = Z_GQ + QK
Z_GV = Z_GK + QK
Z_MO = Z_GV + VW
Z_GR = Z_MO + VW
Z_GATE_M = Z_GR + VW
Z_GATE_G = Z_GATE_M + D_MODEL
Z_WIDTH = Z_GATE_G + D_MODEL
SM_I = 0
SM_F = HEADS
SM_A = 2 * HEADS

VMEM_LIMIT = 56 * 1024 * 1024


def _cparams(sem):
    return pltpu.CompilerParams(dimension_semantics=sem, vmem_limit_bytes=VMEM_LIMIT)


def _rms(x, g):
    return x * lax.rsqrt(jnp.mean(x * x, axis=-1, keepdims=True) + EPS) * g


def _two_sigmoid_of_twice(xh):
    return jnp.tanh(xh) + 1.0


def _log_sigmoid(x):
    return jnp.minimum(x, 0.0) - jnp.log(1.0 + jnp.exp(-jnp.abs(x)))


def _dot(a, b):
    return jnp.dot(a, b, preferred_element_type=F32)


def _dot_nt(a, b):
    return lax.dot_general(a, b, (((1,), (1,)), ((), ())), preferred_element_type=F32)


def _split3(x):
    hi = x.astype(BF16)
    r1 = x - hi.astype(F32)
    mid = r1.astype(BF16)
    lo = (r1 - mid.astype(F32)).astype(BF16)
    return hi, mid, lo


def _dot_exact_left(m_bf16, x, terms=3):
    return sum(_dot(m_bf16, p) for p in _split3(x)[:terms])


def _resident(shape):
    zeros = (0,) * len(shape)
    return pl.BlockSpec(shape, lambda *_: zeros, pipeline_mode=pl.Buffered(1))


def _ffn_kernel(*refs, final, fused_res):
    refs = list(refs)
    if fused_res:
        a_ref, wr_ref = refs[:2]
        refs = refs[2:]
    x_ref, g_ref, wg_ref, wu_ref, wd_ref = refs[:5]
    refs = refs[5:]
    if final:
        fg_ref = refs.pop(0)
    (o_ref,) = refs
    x = x_ref[...]
    if fused_res:
        x = x + _dot(a_ref[...], wr_ref[...])
    hn = _rms(x, g_ref[...]).astype(BF16)
    a = _dot(hn, wg_ref[...])
    u = _dot(hn, wu_ref[...])
    act = (a * _two_sigmoid_of_twice(a) * u).astype(BF16)
    y = x + 0.5 * _dot(act, wd_ref[...])
    if final:
        y = _rms(y, fg_ref[...])
    o_ref[...] = y


def _ffn(x, g, wg, wu, wd, final_g=None, res_a=None, res_w=None, tm=512):
    n = x.shape[0]
    tm = min(tm, n)
    assert n % tm == 0
    final = final_g is not None
    fused_res = res_a is not None
    row = pl.BlockSpec((tm, D_MODEL), lambda i: (i, 0))
    in_specs, args = [], []
    if fused_res:
        in_specs += [row, _resident((D_MODEL, D_MODEL))]
        args += [res_a, res_w]
    in_specs += [row, _resident((1, D_MODEL)), _resident((D_MODEL, D_FF)), _resident((D_MODEL, D_FF)),
                 _resident((D_FF, D_MODEL))]
    args += [x, g, wg, wu, wd]
    if final:
        in_specs.append(_resident((1, D_MODEL)))
        args.append(final_g)
    return pl.pallas_call(
        functools.partial(_ffn_kernel, final=final, fused_res=fused_res),
        grid=(n // tm,),
        in_specs=in_specs,
        out_specs=row,
        out_shape=jax.ShapeDtypeStruct((n, D_MODEL), F32),
        compiler_params=_cparams(("parallel",)),
        name="ffn_final" if final else "ffn",
    )(*args)


def _normproj_kernel(x_ref, g_ref, w_ref, o_ref):
    o_ref[...] = _dot(_rms(x_ref[...], g_ref[...]).astype(BF16), w_ref[...]).astype(o_ref.dtype)


def _normproj(x, g, w, out_dtype, tm=512, name="normproj"):
    n = x.shape[0]
    width = w.shape[1]
    tm = min(tm, n)
    assert n % tm == 0
    return pl.pallas_call(
        _normproj_kernel,
        grid=(n // tm,),
        in_specs=[pl.BlockSpec((tm, D_MODEL), lambda i: (i, 0)), _resident((1, D_MODEL)),
                  _resident((D_MODEL, width))],
        out_specs=pl.BlockSpec((tm, width), lambda i: (i, 0)),
        out_shape=jax.ShapeDtypeStruct((n, width), out_dtype),
        compiler_params=_cparams(("parallel",)),
        name=name,
    )(x, g, w)


def _inproj_kernel(x_ref, g_ref, w_ref, ws_ref, wa_ref, ba_ref, z_ref, s_ref, b_ref, *, chunk):
    tm = x_ref.shape[0]
    half = Z_WIDTH // 2
    hn = _rms(x_ref[...], g_ref[...]).astype(BF16)
    sm = _dot(hn, ws_ref[...])
    s_ref[...] = sm
    wa_hi = wa_ref[...].astype(BF16)
    wa_lo = (wa_ref[...] - wa_hi.astype(F32)).astype(BF16)
    sm_hi = sm.astype(BF16)
    sm_lo = (sm - sm_hi.astype(F32)).astype(BF16)
    a_raw = _dot(sm_hi, wa_hi) + _dot(sm_lo, wa_hi) + _dot(sm_hi, wa_lo) + ba_ref[...]
    z_ref[:, :half] = _dot(hn, w_ref[:, :half]).astype(z_ref.dtype)
    la = _log_sigmoid(a_raw) * (LOG2E / G_TAU)
    span = min(tm, 2 * LANES)
    assert span % chunk == 0 and tm % span == 0
    row = lax.broadcasted_iota(jnp.int32, (span, span), 0)
    col = lax.broadcasted_iota(jnp.int32, (span, span), 1)
    tri = jnp.where((col <= row) & (col // chunk == row // chunk), 1.0, 0.0).astype(BF16)
    for c in range(tm // span):
        b_ref[c * span:(c + 1) * span, :] = _dot_exact_left(tri, la[c * span:(c + 1) * span, :], terms=2)
    z_ref[:, half:] = _dot(hn, w_ref[:, half:]).astype(z_ref.dtype)


def _inproj(x, g, w, w_side, wa_pad, ba, chunk, tm=512):
    n = x.shape[0]
    tm = min(tm, n)
    assert n % tm == 0 and tm % chunk == 0
    row = lambda width: pl.BlockSpec((tm, width), lambda i: (i, 0))
    return pl.pallas_call(
        functools.partial(_inproj_kernel, chunk=chunk),
        grid=(n // tm,),
        in_specs=[row(D_MODEL), _resident((1, D_MODEL)), _resident((D_MODEL, Z_WIDTH)),
                  _resident((D_MODEL, LANES)), _resident((LANES, QK)), _resident((1, QK))],
        out_specs=[row(Z_WIDTH), row(LANES), row(QK)],
        out_shape=[jax.ShapeDtypeStruct((n, Z_WIDTH), BF16), jax.ShapeDtypeStruct((n, LANES), F32),
                   jax.ShapeDtypeStruct((n, QK), F32)],
        compiler_params=_cparams(("parallel",)),
        name="inproj",
    )(x, g, w, w_side, wa_pad, ba)


def _head_norm(h, gn, gate):
    outs = []
    for hd in range(HEADS):
        cs = slice(hd * DV, (hd + 1) * DV)
        o = h[:, cs].astype(F32)
        outs.append(o * lax.rsqrt(jnp.mean(o * o, axis=-1, keepdims=True) + EPS) * gn[:, cs] * gate[:, cs])
    return jnp.concatenate(outs, axis=1).astype(BF16)


def _merge_kernel(og_ref, rg_ref, gm_ref, gg_ref, hm_ref, hg_ref, x_ref, nm_ref, ng_ref, wm_ref, wg_ref, wo_ref,
                  gq_ref, wq_ref, o_ref, q_ref):
    half_nm = 0.5 * nm_ref[...]
    hm = _head_norm(hm_ref[...], half_nm, _two_sigmoid_of_twice(og_ref[...].astype(F32)))
    ym = _dot(hm, wm_ref[...])
    rg = rg_ref[...].astype(F32)
    hg = _head_norm(hg_ref[...], ng_ref[...], rg * _two_sigmoid_of_twice(rg))
    yg = _dot(hg, wg_ref[...])
    y = 0.5 * (_two_sigmoid_of_twice(gm_ref[...].astype(F32)) * ym
               + _two_sigmoid_of_twice(gg_ref[...].astype(F32)) * yg)
    x = x_ref[...] + _dot(y.astype(BF16), wo_ref[...])
    o_ref[...] = x
    q_ref[...] = _dot(_rms(x, gq_ref[...]).astype(BF16), wq_ref[...]).astype(q_ref.dtype)


def _merge(z, hm, hg, x, nm, ng, wm, wg, wo, gq, wq, tm=512):
    n = x.shape[0]
    tm = min(tm, n)
    assert n % tm == 0
    row = lambda i: (i, 0)
    zcol = lambda off, width: pl.BlockSpec((tm, width), lambda i: (i, off // width))
    return pl.pallas_call(
        _merge_kernel,
        grid=(n // tm,),
        in_specs=[
            zcol(Z_MO, VW),
            zcol(Z_GR, VW),
            zcol(Z_GATE_M, D_MODEL),
            zcol(Z_GATE_G, D_MODEL),
            pl.BlockSpec((tm, VW), row),
            pl.BlockSpec((tm, VW), row),
            pl.BlockSpec((tm, D_MODEL), row),
            _resident((1, VW)),
            _resident((1, VW)),
            _resident((VW, D_MODEL)),
            _resident((VW, D_MODEL)),
            _resident((D_MODEL, D_MODEL)),
            _resident((1, D_MODEL)),
            _resident((D_MODEL, D_MODEL)),
        ],
        out_specs=[pl.BlockSpec((tm, D_MODEL), row), pl.BlockSpec((tm, D_MODEL), row)],
        out_shape=[jax.ShapeDtypeStruct((n, D_MODEL), F32), jax.ShapeDtypeStruct((n, D_MODEL), BF16)],
        compiler_params=_cparams(("parallel",)),
        name="merge",
    )(z, z, z, z, hm, hg, x, nm, ng, wm, wg, wo, gq, wq)


def _attn_kernel(q_ref, k_ref, v_ref, o_ref):
    scale = C_HD ** -0.5
    cols = [slice(h * C_HD, (h + 1) * C_HD) for h in range(HEADS)]
    scores = [_dot_nt(q_ref[0, :, cs], k_ref[0, :, cs].astype(BF16)) for cs in cols]
    probs = []
    for s in scores:
        s = s * scale
        p = jnp.exp(s - jnp.max(s, axis=-1, keepdims=True))
        probs.append((p / jnp.sum(p, axis=-1, keepdims=True)).astype(BF16))
    outs = [_dot(p, v_ref[0, :, cs].astype(BF16)) for p, cs in zip(probs, cols)]
    for o, cs in zip(outs, cols):
        o_ref[0, :, cs] = o.astype(o_ref.dtype)


def _attn(q, k, v, tq):
    b, t, _ = q.shape
    assert t % tq == 0
    return pl.pallas_call(
        _attn_kernel,
        grid=(b, t // tq),
        in_specs=[
            pl.BlockSpec((1, tq, D_MODEL), lambda i, j: (i, j, 0)),
            pl.BlockSpec((1, N_MEM, D_MODEL), lambda i, j: (i, 0, 0)),
            pl.BlockSpec((1, N_MEM, D_MODEL), lambda i, j: (i, 0, 0)),
        ],
        out_specs=pl.BlockSpec((1, tq, D_MODEL), lambda i, j: (i, j, 0)),
        out_shape=jax.ShapeDtypeStruct((b, t, D_MODEL), BF16),
        compiler_params=_cparams(("parallel", "arbitrary")),
        name="attn",
    )(q, k, v)


CACHE_HALVES = C_HD // LANES
CACHE_ROWS = HEADS * CACHE_HALVES


def _attn_cache_kernel(q_ref, k_ref, v_ref, o_ref, *, t):
    scale = C_HD ** -0.5
    nq = t * HEADS
    lane = lax.broadcasted_iota(jnp.int32, (nq, LANES), 1)
    rowi = lax.broadcasted_iota(jnp.int32, (nq, LANES), 0)
    valid = ((lane % CACHE_ROWS) // HEADS == 0) & (lane % HEADS == rowi % HEADS)
    n_tiles = N_MEM * CACHE_ROWS // LANES
    seqs = range(q_ref.shape[0])
    scores = [_dot_nt(q_ref[bi], k_ref[bi].astype(BF16)) for bi in seqs]
    probs = []
    for s in scores:
        tiles = []
        for j in range(n_tiles):
            cs = slice(j * LANES, (j + 1) * LANES)
            sj = s[0:nq, cs] + pltpu.roll(s[nq:2 * nq, cs], LANES - HEADS, 1)
            tiles.append(jnp.where(valid, sj * scale, -jnp.inf))
        mx = functools.reduce(jnp.maximum, [jnp.max(x, axis=1, keepdims=True) for x in tiles])
        ps = [jnp.exp(x - mx) for x in tiles]
        den = functools.reduce(jnp.add, [jnp.sum(x, axis=1, keepdims=True) for x in ps])
        inv = 1.0 / den
        p0 = jnp.concatenate([x * inv for x in ps], axis=1)
        p1 = jnp.concatenate([pltpu.roll(x * inv, HEADS, 1) for x in ps], axis=1)
        probs.append(jnp.concatenate([p0, p1], axis=0).astype(BF16))
    outs = [_dot(p, v_ref[bi].astype(BF16)) for p, bi in zip(probs, seqs)]
    for o, bi in zip(outs, seqs):
        o_ref[bi] = o.astype(o_ref.dtype)


def _attn_cache(q, k, v, bb):
    assert CACHE_HALVES == 2
    b, t, _ = q.shape
    assert b % bb == 0
    nq = t * HEADS
    qv = q.reshape(b, t, HEADS, CACHE_HALVES, LANES).transpose(0, 3, 1, 2, 4)
    qv = qv.reshape(b, CACHE_HALVES * nq, LANES)

    def cache_view(a):
        a = a.reshape(b, N_MEM, HEADS, CACHE_HALVES, LANES).transpose(0, 1, 3, 2, 4)
        return a.reshape(b, N_MEM * CACHE_ROWS, LANES)

    o = pl.pallas_call(
        functools.partial(_attn_cache_kernel, t=t),
        grid=(b // bb,),
        in_specs=[
            pl.BlockSpec((bb, CACHE_HALVES * nq, LANES), lambda i: (i, 0, 0)),
            pl.BlockSpec((bb, N_MEM * CACHE_ROWS, LANES), lambda i: (i, 0, 0)),
            pl.BlockSpec((bb, N_MEM * CACHE_ROWS, LANES), lambda i: (i, 0, 0)),
        ],
        out_specs=pl.BlockSpec((bb, CACHE_HALVES * nq, LANES), lambda i: (i, 0, 0)),
        out_shape=jax.ShapeDtypeStruct((b, CACHE_HALVES * nq, LANES), BF16),
        compiler_params=_cparams(("parallel",)),
        name="attn_cache",
    )(qv, cache_view(k), cache_view(v))
    return o.reshape(b, CACHE_HALVES, t, HEADS, LANES).transpose(0, 2, 3, 1, 4).reshape(b, t, D_MODEL)


def _tile_lanes(rep, width):
    if width <= LANES:
        return rep[:, :width]
    return jnp.concatenate([rep] * (width // LANES), axis=1)


def _mlstm_kernel(q_ref, k_ref, v_ref, sm_ref, bias_ref, c0_ref, n0_ref, m0_ref,
                  h_ref, c1_ref, n1_ref, m1_ref, *scratch, chunk, t_valid, carried):
    L = chunk
    bb, tb = q_ref.shape[0], q_ref.shape[1]
    scale = DK ** -0.5

    if carried:
        c_scr, n_scr, m_scr = scratch

        @pl.when(pl.program_id(1) == 0)
        def _():
            c_scr[...] = c0_ref[0]
            n_scr[...] = n0_ref[0]
            m_scr[...] = m0_ref[0]

    row = lax.broadcasted_iota(jnp.int32, (L, L), 0)
    col = lax.broadcasted_iota(jnp.int32, (L, L), 1)
    tri = col <= row
    tri_lo = jnp.where(tri, 1.0, 0.0).astype(BF16)
    lane = lax.broadcasted_iota(jnp.int32, (L, LANES), 1)
    spread = jnp.where(lax.broadcasted_iota(jnp.int32, (LANES, 2 * HEADS * LANES), 0)
                       == lax.broadcasted_iota(jnp.int32, (LANES, 2 * HEADS * LANES), 1) // LANES,
                       1.0, 0.0).astype(BF16)
    pick = jnp.where(lax.broadcasted_iota(jnp.int32, (SUBLANES, LANES), 0)
                     == lax.broadcasted_iota(jnp.int32, (SUBLANES, LANES), 1), 1.0, 0.0).astype(BF16)

    def chunk_group(bis, r):
        rows = pl.ds(r, L)
        cols_of, rows_of = {}, {}
        for bi in bis:
            g = sm_ref[bi, rows, :] + bias_ref[...]
            lf_all = _log_sigmoid(g)
            ig_all = g
            if t_valid < L:
                valid = lax.broadcasted_iota(jnp.int32, (L, 1), 0) < t_valid
                lf_all = jnp.where(valid, lf_all, 0.0)
                ig_all = jnp.where(valid, g, -1e30)
            f_all = _dot_exact_left(tri_lo, lf_all)
            x = jnp.where(lane < SM_F, ig_all, f_all) * LOG2E
            parts = _split3(x)
            cols_of[bi] = sum(_dot(p, spread) for p in parts)
            rows_of[bi] = sum(_dot_nt(pick, p) for p in parts)
        pairs = [(bi, h) for bi in bis for h in range(HEADS)]

        def state(bi):
            if carried:
                return c_scr, n_scr, m_scr, c_scr, n_scr, m_scr
            return (c0_ref.at[bi], n0_ref.at[bi], m0_ref.at[bi], c1_ref.at[bi], n1_ref.at[bi], m1_ref.at[bi])

        st = {}
        for bi, h in pairs:
            c_in, n_in, m_in = state(bi)[:3]
            slab = lambda c, bi=bi: cols_of[bi][:, c * LANES:(c + 1) * LANES]
            f_rep = slab(SM_F + h)
            ig_rep = slab(SM_I + h)
            f_row = rows_of[bi][SM_F + h:SM_F + h + 1, :]
            ig_row = rows_of[bi][SM_I + h:SM_I + h + 1, :]
            dmat = jnp.where(tri, _tile_lanes(f_rep, L) - f_row + ig_row, -jnp.inf)
            m_inter = m_in[h:h + 1, :] * LOG2E + f_rep
            m = jnp.maximum(m_inter, jnp.max(dmat, axis=1, keepdims=True))
            w = jnp.exp2(dmat - _tile_lanes(m - math.log2(scale), L))
            a = jnp.exp2(m_inter - m)
            m_end = m[L - 1:L, :]
            w_rep = jnp.exp2(f_rep[L - 1:L, :] - f_rep + ig_rep - (m_end - math.log2(scale)))
            st[bi, h] = dict(m=m, w=w, a=a, m_end=m_end, w_rep=w_rep, c0=c_in[h], n0=n_in[h:h + 1, :],
                             qb=q_ref[bi, rows, h * DK:(h + 1) * DK], kb=k_ref[bi, rows, h * DK:(h + 1) * DK],
                             vb=v_ref[bi, rows, h * DV:(h + 1) * DV])
        for key in pairs:
            d = st[key]
            d["s"] = _dot_nt(d["qb"], d["kb"]) * d["w"]
        for key in pairs:
            d = st[key]
            d["inter"] = _dot_nt(d["qb"], d["c0"].astype(BF16))
        for key in pairs:
            d = st[key]
            d["sv"] = _dot(d["s"].astype(BF16), d["vb"])
        for key in pairs:
            d = st[key]
            vw_t = (d["vb"].astype(F32) * _tile_lanes(d["w_rep"], DV)).T.astype(BF16)
            d["c_new"] = d["a"][L - 1:L, :] * d["c0"] + _dot(vw_t, d["kb"])
        for bi, h in pairs:
            d = st[bi, h]
            a, m = d["a"], d["m"]
            qf = d["qb"].astype(F32)
            kf = d["kb"].astype(F32)
            num = _tile_lanes(a, DV) * d["inter"] + d["sv"]
            den = a * jnp.sum(qf * d["n0"], axis=1, keepdims=True) + jnp.sum(d["s"], axis=1, keepdims=True)
            hh = num / _tile_lanes(jnp.maximum(jnp.abs(den), jnp.exp2(-m)), DV)
            d["n_new"] = a[L - 1:L, :] * d["n0"] + jnp.sum(kf * d["w_rep"], axis=0, keepdims=True)
            d["hn"] = hh.astype(h_ref.dtype)
        for bi, h in pairs:
            d = st[bi, h]
            c_out, n_out, m_out = state(bi)[3:]
            c_out[h] = d["c_new"]
            n_out[h:h + 1, :] = d["n_new"]
            m_out[h:h + 1, :] = d["m_end"] * (1.0 / LOG2E)
            h_ref[bi, rows, h * DV:(h + 1) * DV] = d["hn"]

    if tb == L:
        chunk_group(list(range(bb)), 0)
    else:
        assert bb == 1

        def loop_body(ci, carry):
            chunk_group([0], pl.multiple_of(ci * L, L))
            return carry

        lax.fori_loop(0, tb // L, loop_body, 0, unroll=2)

    if carried:
        @pl.when(pl.program_id(1) == pl.num_programs(1) - 1)
        def _():
            c1_ref[0] = c_scr[...]
            n1_ref[0] = n_scr[...]
            m1_ref[0] = m_scr[...]


def _mixer_specs():
    bt = lambda col: (lambda i, j: (i, j, col))
    return bt, (lambda i, j: (i, 0, 0, 0)), (lambda i, j: (i, 0, 0))


def _mlstm(z3, sm3, bias_row, c0, n0, m0, chunk, tb, bb, t_valid):
    b, t, _ = z3.shape
    assert t % tb == 0 and tb % chunk == 0 and b % bb == 0
    assert t_valid == t or (tb == t and chunk == t)
    carried = t // tb > 1
    assert not (carried and bb > 1)
    m0 = jnp.broadcast_to(m0[:, :, None], (b, HEADS, LANES))
    bt, st4, st3 = _mixer_specs()
    scratch = [pltpu.VMEM((HEADS, DV, DK), F32), pltpu.VMEM((HEADS, DK), F32),
               pltpu.VMEM((HEADS, LANES), F32)] if carried else []
    h, c1, n1, m1 = pl.pallas_call(
        functools.partial(_mlstm_kernel, chunk=chunk, t_valid=t_valid, carried=carried),
        grid=(b // bb, t // tb),
        in_specs=[
            pl.BlockSpec((bb, tb, QK), bt(Z_MQ // QK)),
            pl.BlockSpec((bb, tb, QK), bt(Z_MK // QK)),
            pl.BlockSpec((bb, tb, VW), bt(Z_MV // VW)),
            pl.BlockSpec((bb, tb, LANES), bt(0)),
            pl.BlockSpec((1, LANES), lambda i, j: (0, 0)),
            pl.BlockSpec((bb, HEADS, DV, DK), st4),
            pl.BlockSpec((bb, HEADS, DK), st3),
            pl.BlockSpec((bb, HEADS, LANES), st3),
        ],
        out_specs=[
            pl.BlockSpec((bb, tb, VW), lambda i, j: (i, j, 0)),
            pl.BlockSpec((bb, HEADS, DV, DK), st4),
            pl.BlockSpec((bb, HEADS, DK), st3),
            pl.BlockSpec((bb, HEADS, LANES), st3),
        ],
        out_shape=[
            jax.ShapeDtypeStruct((b, t, VW), BF16),
            jax.ShapeDtypeStruct((b, HEADS, DV, DK), F32),
            jax.ShapeDtypeStruct((b, HEADS, DK), F32),
            jax.ShapeDtypeStruct((b, HEADS, LANES), F32),
        ],
        scratch_shapes=scratch,
        compiler_params=_cparams(("parallel", "arbitrary")),
        name="mlstm",
    )(z3, z3, z3, sm3, bias_row, c0, n0, m0)
    return h, c1, n1, m1[:, :, 0]


def _gla_kernel(q_ref, k_ref, v_ref, b_ref, s0_ref, h_ref, s1_ref, *scratch, chunk, sub, carried):
    L = chunk
    nb = L // sub
    bb, tb = q_ref.shape[0], q_ref.shape[1]
    scale = DK ** -0.5

    if carried:
        s_scr, kf_scr = scratch

        @pl.when(pl.program_id(1) == 0)
        def _():
            s_scr[...] = s0_ref[0]
    else:
        (kf_scr,) = scratch

    row = lax.broadcasted_iota(jnp.int32, (L, L), 0)
    col = lax.broadcasted_iota(jnp.int32, (L, L), 1)
    in_blk = (col // sub) == (row // sub)
    trow = lax.broadcasted_iota(jnp.int32, (sub, DK), 0)
    eye = lax.broadcasted_iota(jnp.int32, (DK, DK), 0) == lax.broadcasted_iota(jnp.int32, (DK, DK), 1)
    place = jnp.where(lax.broadcasted_iota(jnp.int32, (sub * DK, LANES), 0) // DK
                      == lax.broadcasted_iota(jnp.int32, (sub * DK, LANES), 1) % sub, 1.0, 0.0).astype(BF16)

    def chunk_group(bis, r):
        rows = pl.ds(r, L)
        pairs = [(bi, h) for bi in bis for h in range(HEADS)]

        def state(bi):
            return (s_scr, s_scr) if carried else (s0_ref.at[bi], s1_ref.at[bi])

        slabs = []
        for bi, h in pairs:
            ks = slice(h * DK, (h + 1) * DK)
            qf = q_ref[bi, rows, ks].astype(F32) * scale
            for i in range(nb):
                blk = pl.ds(pl.multiple_of(r + i * sub, sub), sub)
                bs = b_ref[bi, blk, ks]
                kblk = kf_scr[bi, blk, ks]
                qs = qf[i * sub:(i + 1) * sub, :]
                row_slabs = []
                for s in range(sub):
                    e = jnp.exp2(jnp.where(trow >= s, bs - bs[s:s + 1, :], -jnp.inf))
                    row_slabs.append(e * qs * kblk[s:s + 1, :])
                slabs.append(jnp.concatenate(row_slabs, axis=1))
        own = _dot(jnp.concatenate(slabs, axis=0).astype(BF16), place)

        st = {}
        for n, (bi, h) in enumerate(pairs):
            ks = slice(h * DK, (h + 1) * DK)
            vs = slice(h * DV, (h + 1) * DV)
            bh = b_ref[bi, rows, ks]
            qf = q_ref[bi, rows, ks].astype(F32) * scale
            kf = kf_scr[bi, rows, ks]
            s0 = state(bi)[0][h]
            d = dict(vb=v_ref[bi, rows, vs], s0=s0, s0b=s0.astype(BF16),
                     qhat=(qf * jnp.exp2(bh)).astype(BF16), a_own=own[n * L:(n + 1) * L, 0:L])
            d["qk"] = []
            for i in range(1, nb):
                r0 = i * sub
                anchor = bh[r0:r0 + 1, :]
                qt = qf[r0:r0 + sub, :] * jnp.exp2(bh[r0:r0 + sub, :] - anchor)
                kt = kf[0:r0, :] * jnp.exp2(anchor - bh[0:r0, :])
                kt = jnp.concatenate([kt, jnp.zeros((L - r0, DK), F32)], axis=0)
                d["qk"].append((qt.astype(BF16), kt.astype(BF16)))
            b_end = bh[L - 1:L, :]
            d["e_col"] = jnp.sum(jnp.where(eye, jnp.exp2(b_end), 0.0), axis=1, keepdims=True)
            d["ke_t"] = (kf * jnp.exp2(b_end - bh)).T.astype(BF16)
            st[bi, h] = d
        for key in pairs:
            d = st[key]
            d["o"] = _dot(d["qhat"], d["s0b"])
        for key in pairs:
            d = st[key]
            d["blocks"] = [_dot_nt(qt, kt) for qt, kt in d["qk"]]
        for key in pairs:
            d = st[key]
            d["s_new"] = d["e_col"] * d["s0"] + _dot(d["ke_t"], d["vb"])
        for key in pairs:
            d = st[key]
            if nb > 1:
                below = jnp.concatenate([jnp.zeros((sub, L), F32)] + d["blocks"], axis=0)
                a_intra = jnp.where(in_blk, d["a_own"], below)
                d["o"] = d["o"] + _dot(a_intra.astype(BF16), d["vb"])
            else:
                d["o"] = d["o"] + _dot(d["a_own"], d["vb"].astype(F32))
        for bi, h in pairs:
            d = st[bi, h]
            d["hn"] = d["o"].astype(h_ref.dtype)
        for bi, h in pairs:
            d = st[bi, h]
            state(bi)[1][h] = d["s_new"]
            h_ref[bi, rows, h * DV:(h + 1) * DV] = d["hn"]

    for bi in range(bb):
        kf_scr[bi] = k_ref[bi].astype(F32)
    if tb == L:
        chunk_group(list(range(bb)), 0)
    else:
        assert bb == 1

        def loop_body(ci, carry):
            chunk_group([0], pl.multiple_of(ci * L, L))
            return carry

        lax.fori_loop(0, tb // L, loop_body, 0, unroll=2)

    if carried:
        @pl.when(pl.program_id(1) == pl.num_programs(1) - 1)
        def _():
            s1_ref[0] = s_scr[...]


def _gla(z3, b3, s0, chunk, sub, tb, bb):
    b, t, _ = z3.shape
    assert t % tb == 0 and tb % chunk == 0 and chunk % sub == 0 and b % bb == 0
    carried = t // tb > 1
    assert not (carried and bb > 1)
    bt, st4, _ = _mixer_specs()
    scratch = [pltpu.VMEM((bb, tb, QK), F32)]
    if carried:
        scratch = [pltpu.VMEM((HEADS, DK, DV), F32)] + scratch
    h, s1 = pl.pallas_call(
        functools.partial(_gla_kernel, chunk=chunk, sub=sub, carried=carried),
        grid=(b // bb, t // tb),
        in_specs=[
            pl.BlockSpec((bb, tb, QK), bt(Z_GQ // QK)),
            pl.BlockSpec((bb, tb, QK), bt(Z_GK // QK)),
            pl.BlockSpec((bb, tb, VW), bt(Z_GV // VW)),
            pl.BlockSpec((bb, tb, QK), bt(0)),
            pl.BlockSpec((bb, HEADS, DK, DV), st4),
        ],
        out_specs=[
            pl.BlockSpec((bb, tb, VW), lambda i, j: (i, j, 0)),
            pl.BlockSpec((bb, HEADS, DK, DV), st4),
        ],
        out_shape=[
            jax.ShapeDtypeStruct((b, t, VW), BF16),
            jax.ShapeDtypeStruct((b, HEADS, DK, DV), F32),
        ],
        scratch_shapes=scratch,
        compiler_params=_cparams(("parallel", "arbitrary")),
        name="gla",
    )(z3, z3, z3, b3, s0)
    return h, s1


def _layer(x3, mem_k, mem_v, c0, n0, m0, s0, p, final_norm, m_chunk, g_chunk, g_sub, tb, bb, tq):
    b, t, _ = x3.shape
    n = b * t
    tmix = -(-t // tb) * tb
    x = x3.reshape(n, D_MODEL)
    x = _ffn(x, p["ffn1_norm"], p["ffn1_wg"], p["ffn1_wu"], p["ffn1_wd"])
    z, sm, bdec = _inproj(x, p["mix_norm"], p["w_in"], p["w_in_side"], p["wa_pad"], p["gla_ba"],
                          chunk=min(g_chunk, t))
    pad_t = lambda a, mode="constant": jnp.pad(a, ((0, 0), (0, tmix - t), (0, 0)), mode=mode)
    z3 = pad_t(z.reshape(b, t, Z_WIDTH))
    sm3 = pad_t(sm.reshape(b, t, LANES))
    b3 = pad_t(bdec.reshape(b, t, QK), "edge")
    hm, c1, n1, m1 = _mlstm(z3, sm3, p["bias_row"], c0, n0, m0, m_chunk, tb, bb, t)
    hg, s1 = _gla(z3, b3, s0, g_chunk, g_sub, tb, bb)
    x, q = _merge(z, hm[:, :t].reshape(n, VW), hg[:, :t].reshape(n, VW), x, p["mlstm_norm"], p["gla_norm"],
                  p["w_br_m"], p["w_br_g"], p["w_out"], p["ca_norm"], p["ca_wq"])
    if mem_k.ndim == 4:
        o = _attn_cache(q.reshape(b, t, D_MODEL), mem_k, mem_v, bb=4)
    else:
        o = _attn(q.reshape(b, t, D_MODEL), mem_k, mem_v, tq)
    y = _ffn(x, p["ffn2_norm"], p["ffn2_wg"], p["ffn2_wu"], p["ffn2_wd"], final_g=final_norm,
             res_a=o.reshape(n, D_MODEL), res_w=p["ca_wo"])
    return y.reshape(b, t, D_MODEL), (c1, n1, m1, s1)


def _permute_w_in(w_in):
    sizes = (QK, QK, VW, VW, HEADS, HEADS, QK, QK, VW, VW, G_RANK, D_MODEL, D_MODEL)
    offs = [0]
    for s in sizes:
        offs.append(offs[-1] + s)
    part = lambda i: w_in[:, offs[i]:offs[i + 1]]
    (mq, mk, mv, mo, mi, mf, gq, gk, gv, gr, ga, gate_m, gate_g) = [part(i) for i in range(len(sizes))]
    pad = jnp.zeros((D_MODEL, LANES - 2 * HEADS - G_RANK), w_in.dtype)
    w_main = jnp.concatenate([mq, mk, mv, gq, gk, gv, 0.5 * mo, 0.5 * gr, 0.5 * gate_m, 0.5 * gate_g], axis=1)
    w_side = jnp.concatenate([mi, mf, ga, pad], axis=1)
    return w_main.astype(BF16), w_side.astype(BF16)


def kernel(x_prompt, x_sample, mem_prompt, state_mlstm_C, state_mlstm_n, state_mlstm_m, state_gla_S, cache_mem_k, cache_mem_v, ffn1_norm, ffn1_wg, ffn1_wu, ffn1_wd, mix_norm, w_in, b_if, gla_wa2, gla_ba, mlstm_norm, gla_norm, w_br_m, w_br_g, w_out, ca_norm, mem_norm, ca_wq, ca_wk, ca_wv, ca_wo, ffn2_norm, ffn2_wg, ffn2_wu, ffn2_wd, final_norm):
    depth = ffn1_norm.shape[0]
    assert depth == 1
    l = 0
    bp, tp, _ = x_prompt.shape
    bs, ts, _ = x_sample.shape
    row = lambda v: v.reshape(1, -1).astype(F32)
    bias_row = jnp.zeros((1, LANES), F32).at[0, :2 * HEADS].set(b_if[l])
    wa_pad = jnp.zeros((LANES, QK), F32).at[SM_A:SM_A + G_RANK, :].set(gla_wa2[l])
    w_in_main, w_in_side = _permute_w_in(w_in[l])
    p = {
        "w_in": w_in_main, "w_in_side": w_in_side,
        "ffn1_norm": row(ffn1_norm[l]), "ffn1_wg": (0.5 * ffn1_wg[l]).astype(BF16),
        "ffn1_wu": ffn1_wu[l].astype(BF16),
        "ffn1_wd": ffn1_wd[l].astype(BF16),
        "mix_norm": row(mix_norm[l]),
        "bias_row": bias_row, "wa_pad": wa_pad, "gla_ba": row(gla_ba[l]),
        "mlstm_norm": row(mlstm_norm[l]), "gla_norm": row(gla_norm[l]),
        "w_br_m": w_br_m[l].astype(BF16), "w_br_g": w_br_g[l].astype(BF16), "w_out": w_out[l].astype(BF16),
        "ca_norm": row(ca_norm[l]), "ca_wq": ca_wq[l].astype(BF16), "ca_wo": ca_wo[l].astype(BF16),
        "ffn2_norm": row(ffn2_norm[l]), "ffn2_wg": (0.5 * ffn2_wg[l]).astype(BF16),
        "ffn2_wu": ffn2_wu[l].astype(BF16),
        "ffn2_wd": ffn2_wd[l].astype(BF16),
    }
    fin = row(final_norm)

    mem2 = mem_prompt.reshape(bp * N_MEM, D_MODEL)
    mk_p = _normproj(mem2, row(mem_norm[l]), ca_wk[l].astype(BF16), F32, name="memk")
    mv_p = _normproj(mem2, row(mem_norm[l]), ca_wv[l].astype(BF16), F32, name="memv")
    mk_p = mk_p.reshape(bp, N_MEM, D_MODEL)
    mv_p = mv_p.reshape(bp, N_MEM, D_MODEL)
    zc = jnp.zeros((bp, HEADS, DV, DK), F32)
    zn = jnp.zeros((bp, HEADS, DK), F32)
    zm = jnp.zeros((bp, HEADS), F32)
    zs = jnp.zeros((bp, HEADS, DK, DV), F32)
    yp, (cp, np_, mp, sp) = _layer(x_prompt, mk_p, mv_p, zc, zn, zm, zs, p, fin,
                                   m_chunk=128, g_chunk=64, g_sub=8, tb=512, bb=1, tq=512)

    tpad = -(-ts // SUBLANES) * SUBLANES
    ys, (cs, ns, ms, ss) = _layer(x_sample, cache_mem_k[l], cache_mem_v[l],
                                  state_mlstm_C[l], state_mlstm_n[l], state_mlstm_m[l], state_gla_S[l],
                                  p, fin, m_chunk=tpad, g_chunk=tpad, g_sub=tpad, tb=tpad, bb=16, tq=None)

    st = lambda a: a[None]
    return (yp, ys, st(cp), st(np_), st(mp), st(sp),
            st(mk_p.reshape(bp, N_MEM, HEADS, C_HD)), st(mv_p.reshape(bp, N_MEM, HEADS, C_HD)),
            st(cs), st(ns), st(ms), st(ss))
```

```python
import functools
import math

import jax
import jax.numpy as jnp
from jax import lax
from jax.experimental import pallas as pl
from jax.experimental.pallas import tpu as pltpu

F32 = jnp.float32
BF16 = jnp.bfloat16

D_MODEL = 1024
D_FF = 2816
HEADS = 4
DK = 128
DV = 256
QK = HEADS * DK
VW = HEADS * DV
G_RANK = 16
G_TAU = 16.0
N_MEM = 256
C_HD = D_MODEL // HEADS
EPS = 1e-6
LOG2E = math.log2(math.e)
LANES = 128
SUBLANES = 8

Z_MQ = 0
Z_MK = Z_MQ + QK
Z_MV = Z_MK + QK
Z_GQ = Z_MV + VW
Z_GK = Z_GQ + QK
Z_GV = Z_GK + QK
Z_MO = Z_GV + VW
Z_GR = Z_MO + VW
Z_GATE_M = Z_GR + VW
Z_GATE_G = Z_GATE_M + D_MODEL
Z_WIDTH = Z_GATE_G + D_MODEL
SM_I = 0
SM_F = HEADS
SM_A = 2 * HEADS

VMEM_LIMIT = 56 * 1024 * 1024


def _cparams(sem):
    return pltpu.CompilerParams(dimension_semantics=sem, vmem_limit_bytes=VMEM_LIMIT)


def _rms(x, g):
    return x * lax.rsqrt(jnp.mean(x * x, axis=-1, keepdims=True) + EPS) * g


def _two_sigmoid_of_twice(xh):
    return jnp.tanh(xh) + 1.0


def _log_sigmoid(x):
    return jnp.minimum(x, 0.0) - jnp.log(1.0 + jnp.exp(-jnp.abs(x)))


def _dot(a, b):
    return jnp.dot(a, b, preferred_element_type=F32)


def _dot_nt(a, b):
    return lax.dot_general(a, b, (((1,), (1,)), ((), ())), preferred_element_type=F32)


def _split3(x):
    hi = x.astype(BF16)
    r1 = x - hi.astype(F32)
    mid = r1.astype(BF16)
    lo = (r1 - mid.astype(F32)).astype(BF16)
    return hi, mid, lo


def _dot_exact_left(m_bf16, x, terms=3):
    return sum(_dot(m_bf16, p) for p in _split3(x)[:terms])


def _resident(shape):
    zeros = (0,) * len(shape)
    return pl.BlockSpec(shape, lambda *_: zeros, pipeline_mode=pl.Buffered(1))


def _ffn_kernel(*refs, final, fused_res):
    refs = list(refs)
    if fused_res:
        a_ref, wr_ref = refs[:2]
        refs = refs[2:]
    x_ref, g_ref, wg_ref, wu_ref, wd_ref = refs[:5]
    refs = refs[5:]
    if final:
        fg_ref = refs.pop(0)
    (o_ref,) = refs
    x = x_ref[...]
    if fused_res:
        x = x + _dot(a_ref[...], wr_ref[...])
    hn = _rms(x, g_ref[...]).astype(BF16)
    a = _dot(hn, wg_ref[...])
    u = _dot(hn, wu_ref[...])
    act = (a * _two_sigmoid_of_twice(a) * u).astype(BF16)
    y = x + 0.5 * _dot(act, wd_ref[...])
    if final:
        y = _rms(y, fg_ref[...])
    o_ref[...] = y


def _ffn(x, g, wg, wu, wd, final_g=None, res_a=None, res_w=None, tm=512):
    n = x.shape[0]
    tm = min(tm, n)
    assert n % tm == 0
    final = final_g is not None
    fused_res = res_a is not None
    row = pl.BlockSpec((tm, D_MODEL), lambda i: (i, 0))
    in_specs, args = [], []
    if fused_res:
        in_specs += [row, _resident((D_MODEL, D_MODEL))]
        args += [res_a, res_w]
    in_specs += [row, _resident((1, D_MODEL)), _resident((D_MODEL, D_FF)), _resident((D_MODEL, D_FF)),
                 _resident((D_FF, D_MODEL))]
    args += [x, g, wg, wu, wd]
    if final:
        in_specs.append(_resident((1, D_MODEL)))
        args.append(final_g)
    return pl.pallas_call(
        functools.partial(_ffn_kernel, final=final, fused_res=fused_res),
        grid=(n // tm,),
        in_specs=in_specs,
        out_specs=row,
        out_shape=jax.ShapeDtypeStruct((n, D_MODEL), F32),
        compiler_params=_cparams(("parallel",)),
        name="ffn_final" if final else "ffn",
    )(*args)


def _normproj_kernel(x_ref, g_ref, w_ref, o_ref):
    o_ref[...] = _dot(_rms(x_ref[...], g_ref[...]).astype(BF16), w_ref[...]).astype(o_ref.dtype)


def _normproj(x, g, w, out_dtype, tm=512, name="normproj"):
    n = x.shape[0]
    width = w.shape[1]
    tm = min(tm, n)
    assert n % tm == 0
    return pl.pallas_call(
        _normproj_kernel,
        grid=(n // tm,),
        in_specs=[pl.BlockSpec((tm, D_MODEL), lambda i: (i, 0)), _resident((1, D_MODEL)),
                  _resident((D_MODEL, width))],
        out_specs=pl.BlockSpec((tm, width), lambda i: (i, 0)),
        out_shape=jax.ShapeDtypeStruct((n, width), out_dtype),
        compiler_params=_cparams(("parallel",)),
        name=name,
    )(x, g, w)


def _inproj_kernel(x_ref, g_ref, w_ref, ws_ref, wa_ref, ba_ref, z_ref, s_ref, b_ref, *, chunk):
    tm = x_ref.shape[0]
    hn = _rms(x_ref[...], g_ref[...]).astype(BF16)
    sm = _dot(hn, ws_ref[...])
    s_ref[...] = sm
    wa_hi = wa_ref[...].astype(BF16)
    wa_lo = (wa_ref[...] - wa_hi.astype(F32)).astype(BF16)
    sm_hi = sm.astype(BF16)
    sm_lo = (sm - sm_hi.astype(F32)).astype(BF16)
    a_raw = _dot(sm_hi, wa_hi) + _dot(sm_lo, wa_hi) + _dot(sm_hi, wa_lo) + ba_ref[...]
    z_ref[...] = _dot(hn, w_ref[...]).astype(z_ref.dtype)
    la = _log_sigmoid(a_raw) * (LOG2E / G_TAU)
    span = min(tm, 2 * LANES)
    assert span % chunk == 0 and tm % span == 0
    row = lax.broadcasted_iota(jnp.int32, (span, span), 0)
    col = lax.broadcasted_iota(jnp.int32, (span, span), 1)
    tri = jnp.where((col <= row) & (col // chunk == row // chunk), 1.0, 0.0).astype(BF16)
    for c in range(tm // span):
        b_ref[c * span:(c + 1) * span, :] = _dot_exact_left(tri, la[c * span:(c + 1) * span, :], terms=2)


def _inproj(x, g, w, w_side, wa_pad, ba, chunk, tm=512):
    n = x.shape[0]
    tm = min(tm, n)
    assert n % tm == 0 and tm % chunk == 0
    row = lambda width: pl.BlockSpec((tm, width), lambda i: (i, 0))
    return pl.pallas_call(
        functools.partial(_inproj_kernel, chunk=chunk),
        grid=(n // tm,),
        in_specs=[row(D_MODEL), _resident((1, D_MODEL)), _resident((D_MODEL, Z_WIDTH)),
                  _resident((D_MODEL, LANES)), _resident((LANES, QK)), _resident((1, QK))],
        out_specs=[row(Z_WIDTH), row(LANES), row(QK)],
        out_shape=[jax.ShapeDtypeStruct((n, Z_WIDTH), BF16), jax.ShapeDtypeStruct((n, LANES), F32),
                   jax.ShapeDtypeStruct((n, QK), F32)],
        compiler_params=_cparams(("parallel",)),
        name="inproj",
    )(x, g, w, w_side, wa_pad, ba)


def _head_norm(h, gn, gate):
    outs = []
    for hd in range(HEADS):
        cs = slice(hd * DV, (hd + 1) * DV)
        o = h[:, cs].astype(F32)
        outs.append(o * lax.rsqrt(jnp.mean(o * o, axis=-1, keepdims=True) + EPS) * gn[:, cs] * gate[:, cs])
    return jnp.concatenate(outs, axis=1).astype(BF16)


def _merge_kernel(og_ref, rg_ref, gm_ref, gg_ref, hm_ref, hg_ref, x_ref, nm_ref, ng_ref, wm_ref, wg_ref, wo_ref,
                  gq_ref, wq_ref, o_ref, q_ref):
    half_nm = 0.5 * nm_ref[...]
    hm = _head_norm(hm_ref[...], half_nm, _two_sigmoid_of_twice(og_ref[...].astype(F32)))
    ym = _dot(hm, wm_ref[...])
    rg = rg_ref[...].astype(F32)
    hg = _head_norm(hg_ref[...], ng_ref[...], rg * _two_sigmoid_of_twice(rg))
    yg = _dot(hg, wg_ref[...])
    y = 0.5 * (_two_sigmoid_of_twice(gm_ref[...].astype(F32)) * ym
               + _two_sigmoid_of_twice(gg_ref[...].astype(F32)) * yg)
    x = x_ref[...] + _dot(y.astype(BF16), wo_ref[...])
    o_ref[...] = x
    q_ref[...] = _dot(_rms(x, gq_ref[...]).astype(BF16), wq_ref[...]).astype(q_ref.dtype)


def _merge(z, hm, hg, x, nm, ng, wm, wg, wo, gq, wq, tm=512):
    n = x.shape[0]
    tm = min(tm, n)
    assert n % tm == 0
    row = lambda i: (i, 0)
    zcol = lambda off, width: pl.BlockSpec((tm, width), lambda i: (i, off // width))
    return pl.pallas_call(
        _merge_kernel,
        grid=(n // tm,),
        in_specs=[
            zcol(Z_MO, VW),
            zcol(Z_GR, VW),
            zcol(Z_GATE_M, D_MODEL),
            zcol(Z_GATE_G, D_MODEL),
            pl.BlockSpec((tm, VW), row),
            pl.BlockSpec((tm, VW), row),
            pl.BlockSpec((tm, D_MODEL), row),
            _resident((1, VW)),
            _resident((1, VW)),
            _resident((VW, D_MODEL)),
            _resident((VW, D_MODEL)),
            _resident((D_MODEL, D_MODEL)),
            _resident((1, D_MODEL)),
            _resident((D_MODEL, D_MODEL)),
        ],
        out_specs=[pl.BlockSpec((tm, D_MODEL), row), pl.BlockSpec((tm, D_MODEL), row)],
        out_shape=[jax.ShapeDtypeStruct((n, D_MODEL), F32), jax.ShapeDtypeStruct((n, D_MODEL), BF16)],
        compiler_params=_cparams(("parallel",)),
        name="merge",
    )(z, z, z, z, hm, hg, x, nm, ng, wm, wg, wo, gq, wq)


def _attn_kernel(q_ref, k_ref, v_ref, o_ref):
    scale = C_HD ** -0.5
    cols = [slice(h * C_HD, (h + 1) * C_HD) for h in range(HEADS)]
    scores = [_dot_nt(q_ref[0, :, cs], k_ref[0, :, cs].astype(BF16)) for cs in cols]
    probs = []
    for s in scores:
        s = s * (scale * LOG2E)
        p = jnp.exp2(s - jnp.max(s, axis=-1, keepdims=True))
        probs.append((p / jnp.sum(p, axis=-1, keepdims=True)).astype(BF16))
    outs = [_dot(p, v_ref[0, :, cs].astype(BF16)) for p, cs in zip(probs, cols)]
    for o, cs in zip(outs, cols):
        o_ref[0, :, cs] = o.astype(o_ref.dtype)


def _attn(q, k, v, tq):
    b, t, _ = q.shape
    assert t % tq == 0
    return pl.pallas_call(
        _attn_kernel,
        grid=(b, t // tq),
        in_specs=[
            pl.BlockSpec((1, tq, D_MODEL), lambda i, j: (i, j, 0)),
            pl.BlockSpec((1, N_MEM, D_MODEL), lambda i, j: (i, 0, 0)),
            pl.BlockSpec((1, N_MEM, D_MODEL), lambda i, j: (i, 0, 0)),
        ],
        out_specs=pl.BlockSpec((1, tq, D_MODEL), lambda i, j: (i, j, 0)),
        out_shape=jax.ShapeDtypeStruct((b, t, D_MODEL), BF16),
        compiler_params=_cparams(("parallel", "arbitrary")),
        name="attn",
    )(q, k, v)


CACHE_HALVES = C_HD // LANES
CACHE_ROWS = HEADS * CACHE_HALVES


def _attn_cache_kernel(q_ref, k_ref, v_ref, o_ref, *, t):
    scale = C_HD ** -0.5
    nq = t * HEADS
    lane = lax.broadcasted_iota(jnp.int32, (nq, LANES), 1)
    rowi = lax.broadcasted_iota(jnp.int32, (nq, LANES), 0)
    valid = ((lane % CACHE_ROWS) // HEADS == 0) & (lane % HEADS == rowi % HEADS)
    n_tiles = N_MEM * CACHE_ROWS // LANES
    seqs = range(q_ref.shape[0])
    scores = [_dot_nt(q_ref[bi], k_ref[bi].astype(BF16)) for bi in seqs]
    probs = []
    for s in scores:
        tiles = []
        for j in range(n_tiles):
            cs = slice(j * LANES, (j + 1) * LANES)
            sj = s[0:nq, cs] + pltpu.roll(s[nq:2 * nq, cs], LANES - HEADS, 1)
            tiles.append(jnp.where(valid, sj * (scale * LOG2E), -jnp.inf))
        mx = functools.reduce(jnp.maximum, [jnp.max(x, axis=1, keepdims=True) for x in tiles])
        ps = [jnp.exp2(x - mx) for x in tiles]
        den = functools.reduce(jnp.add, [jnp.sum(x, axis=1, keepdims=True) for x in ps])
        inv = 1.0 / den
        p0 = jnp.concatenate([x * inv for x in ps], axis=1)
        p1 = jnp.concatenate([pltpu.roll(x * inv, HEADS, 1) for x in ps], axis=1)
        probs.append(jnp.concatenate([p0, p1], axis=0).astype(BF16))
    outs = [_dot(p, v_ref[bi].astype(BF16)) for p, bi in zip(probs, seqs)]
    for o, bi in zip(outs, seqs):
        o_ref[bi] = o.astype(o_ref.dtype)


def _attn_cache(q, k, v, bb):
    assert CACHE_HALVES == 2
    b, t, _ = q.shape
    assert b % bb == 0
    nq = t * HEADS
    qv = q.reshape(b, t, HEADS, CACHE_HALVES, LANES).transpose(0, 3, 1, 2, 4)
    qv = qv.reshape(b, CACHE_HALVES * nq, LANES)

    def cache_view(a):
        a = a.reshape(b, N_MEM, HEADS, CACHE_HALVES, LANES).transpose(0, 1, 3, 2, 4)
        return a.reshape(b, N_MEM * CACHE_ROWS, LANES)

    o = pl.pallas_call(
        functools.partial(_attn_cache_kernel, t=t),
        grid=(b // bb,),
        in_specs=[
            pl.BlockSpec((bb, CACHE_HALVES * nq, LANES), lambda i: (i, 0, 0)),
            pl.BlockSpec((bb, N_MEM * CACHE_ROWS, LANES), lambda i: (i, 0, 0)),
            pl.BlockSpec((bb, N_MEM * CACHE_ROWS, LANES), lambda i: (i, 0, 0)),
        ],
        out_specs=pl.BlockSpec((bb, CACHE_HALVES * nq, LANES), lambda i: (i, 0, 0)),
        out_shape=jax.ShapeDtypeStruct((b, CACHE_HALVES * nq, LANES), BF16),
        compiler_params=_cparams(("parallel",)),
        name="attn_cache",
    )(qv, cache_view(k), cache_view(v))
    return o.reshape(b, CACHE_HALVES, t, HEADS, LANES).transpose(0, 2, 3, 1, 4).reshape(b, t, D_MODEL)


def _tile_lanes(rep, width):
    if width <= LANES:
        return rep[:, :width]
    return jnp.concatenate([rep] * (width // LANES), axis=1)


def _mlstm_kernel(q_ref, k_ref, v_ref, sm_ref, bias_ref, c0_ref, n0_ref, m0_ref,
                  h_ref, c1_ref, n1_ref, m1_ref, *scratch, chunk, t_valid, carried):
    L = chunk
    bb, tb = q_ref.shape[0], q_ref.shape[1]
    scale = DK ** -0.5

    if carried:
        c_scr, n_scr, m_scr = scratch

        @pl.when(pl.program_id(1) == 0)
        def _():
            c_scr[...] = c0_ref[0]
            n_scr[...] = n0_ref[0]
            m_scr[...] = m0_ref[0]

    row = lax.broadcasted_iota(jnp.int32, (L, L), 0)
    col = lax.broadcasted_iota(jnp.int32, (L, L), 1)
    tri = col <= row
    tri_lo = jnp.where(tri, 1.0, 0.0).astype(BF16)
    lane = lax.broadcasted_iota(jnp.int32, (L, LANES), 1)
    spread = jnp.where(lax.broadcasted_iota(jnp.int32, (LANES, 2 * HEADS * LANES), 0)
                       == lax.broadcasted_iota(jnp.int32, (LANES, 2 * HEADS * LANES), 1) // LANES,
                       1.0, 0.0).astype(BF16)
    pick = jnp.where(lax.broadcasted_iota(jnp.int32, (SUBLANES, LANES), 0)
                     == lax.broadcasted_iota(jnp.int32, (SUBLANES, LANES), 1), 1.0, 0.0).astype(BF16)

    def chunk_group(bis, r):
        rows = pl.ds(r, L)
        cols_of, rows_of = {}, {}
        for bi in bis:
            g = sm_ref[bi, rows, :] + bias_ref[...]
            lf_all = _log_sigmoid(g)
            ig_all = g
            if t_valid < L:
                valid = lax.broadcasted_iota(jnp.int32, (L, 1), 0) < t_valid
                lf_all = jnp.where(valid, lf_all, 0.0)
                ig_all = jnp.where(valid, g, -1e30)
            f_all = _dot_exact_left(tri_lo, lf_all)
            x = jnp.where(lane < SM_F, ig_all, f_all) * LOG2E
            parts = _split3(x)
            cols_of[bi] = sum(_dot(p, spread) for p in parts)
            rows_of[bi] = sum(_dot_nt(pick, p) for p in parts)
        pairs = [(bi, h) for bi in bis for h in range(HEADS)]

        def state(bi):
            if carried:
                return c_scr, n_scr, m_scr, c_scr, n_scr, m_scr
            return (c0_ref.at[bi], n0_ref.at[bi], m0_ref.at[bi], c1_ref.at[bi], n1_ref.at[bi], m1_ref.at[bi])

        st = {}
        for bi, h in pairs:
            c_in, n_in, m_in = state(bi)[:3]
            slab = lambda c, bi=bi: cols_of[bi][:, c * LANES:(c + 1) * LANES]
            f_rep = slab(SM_F + h)
            ig_rep = slab(SM_I + h)
            f_row = rows_of[bi][SM_F + h:SM_F + h + 1, :]
            ig_row = rows_of[bi][SM_I + h:SM_I + h + 1, :]
            dmat = jnp.where(tri, _tile_lanes(f_rep, L) - f_row + ig_row, -jnp.inf)
            m_inter = m_in[h:h + 1, :] * LOG2E + f_rep
            m = jnp.maximum(m_inter, jnp.max(dmat, axis=1, keepdims=True))
            w = jnp.exp2(dmat - _tile_lanes(m - math.log2(scale), L))
            a = jnp.exp2(m_inter - m)
            m_end = m[L - 1:L, :]
            w_rep = jnp.exp2(f_rep[L - 1:L, :] - f_rep + ig_rep - (m_end - math.log2(scale)))
            st[bi, h] = dict(m=m, w=w, a=a, m_end=m_end, w_rep=w_rep, c0=c_in[h], n0=n_in[h:h + 1, :],
                             qb=q_ref[bi, rows, h * DK:(h + 1) * DK], kb=k_ref[bi, rows, h * DK:(h + 1) * DK],
                             vb=v_ref[bi, rows, h * DV:(h + 1) * DV])
        for key in pairs:
            d = st[key]
            d["s"] = _dot_nt(d["qb"], d["kb"]) * d["w"]
        for key in pairs:
            d = st[key]
            d["inter"] = _dot_nt(d["qb"], d["c0"].astype(BF16))
        for key in pairs:
            d = st[key]
            d["sv"] = _dot(d["s"].astype(BF16), d["vb"])
        for key in pairs:
            d = st[key]
            vw_t = (d["vb"].astype(F32) * _tile_lanes(d["w_rep"], DV)).T.astype(BF16)
            d["c_new"] = d["a"][L - 1:L, :] * d["c0"] + _dot(vw_t, d["kb"])
        for bi, h in pairs:
            d = st[bi, h]
            a, m = d["a"], d["m"]
            qf = d["qb"].astype(F32)
            kf = d["kb"].astype(F32)
            num = _tile_lanes(a, DV) * d["inter"] + d["sv"]
            den = a * jnp.sum(qf * d["n0"], axis=1, keepdims=True) + jnp.sum(d["s"], axis=1, keepdims=True)
            hh = num / _tile_lanes(jnp.maximum(jnp.abs(den), jnp.exp2(-m)), DV)
            d["n_new"] = a[L - 1:L, :] * d["n0"] + jnp.sum(kf * d["w_rep"], axis=0, keepdims=True)
            d["hn"] = hh.astype(h_ref.dtype)
        for bi, h in pairs:
            d = st[bi, h]
            c_out, n_out, m_out = state(bi)[3:]
            c_out[h] = d["c_new"]
            n_out[h:h + 1, :] = d["n_new"]
            m_out[h:h + 1, :] = d["m_end"] * (1.0 / LOG2E)
            h_ref[bi, rows, h * DV:(h + 1) * DV] = d["hn"]

    if tb == L:
        chunk_group(list(range(bb)), 0)
    else:
        assert bb == 1

        def loop_body(ci, carry):
            chunk_group([0], pl.multiple_of(ci * L, L))
            return carry

        lax.fori_loop(0, tb // L, loop_body, 0, unroll=2)

    if carried:
        @pl.when(pl.program_id(1) == pl.num_programs(1) - 1)
        def _():
            c1_ref[0] = c_scr[...]
            n1_ref[0] = n_scr[...]
            m1_ref[0] = m_scr[...]


def _mixer_specs():
    bt = lambda col: (lambda i, j: (i, j, col))
    return bt, (lambda i, j: (i, 0, 0, 0)), (lambda i, j: (i, 0, 0))


def _mlstm(z3, sm3, bias_row, c0, n0, m0, chunk, tb, bb, t_valid):
    b, t, _ = z3.shape
    assert t % tb == 0 and tb % chunk == 0 and b % bb == 0
    assert t_valid == t or (tb == t and chunk == t)
    carried = t // tb > 1
    assert not (carried and bb > 1)
    m0 = jnp.broadcast_to(m0[:, :, None], (b, HEADS, LANES))
    bt, st4, st3 = _mixer_specs()
    scratch = [pltpu.VMEM((HEADS, DV, DK), F32), pltpu.VMEM((HEADS, DK), F32),
               pltpu.VMEM((HEADS, LANES), F32)] if carried else []
    h, c1, n1, m1 = pl.pallas_call(
        functools.partial(_mlstm_kernel, chunk=chunk, t_valid=t_valid, carried=carried),
        grid=(b // bb, t // tb),
        in_specs=[
            pl.BlockSpec((bb, tb, QK), bt(Z_MQ // QK)),
            pl.BlockSpec((bb, tb, QK), bt(Z_MK // QK)),
            pl.BlockSpec((bb, tb, VW), bt(Z_MV // VW)),
            pl.BlockSpec((bb, tb, LANES), bt(0)),
            pl.BlockSpec((1, LANES), lambda i, j: (0, 0)),
            pl.BlockSpec((bb, HEADS, DV, DK), st4),
            pl.BlockSpec((bb, HEADS, DK), st3),
            pl.BlockSpec((bb, HEADS, LANES), st3),
        ],
        out_specs=[
            pl.BlockSpec((bb, tb, VW), lambda i, j: (i, j, 0)),
            pl.BlockSpec((bb, HEADS, DV, DK), st4),
            pl.BlockSpec((bb, HEADS, DK), st3),
            pl.BlockSpec((bb, HEADS, LANES), st3),
        ],
        out_shape=[
            jax.ShapeDtypeStruct((b, t, VW), BF16),
            jax.ShapeDtypeStruct((b, HEADS, DV, DK), F32),
            jax.ShapeDtypeStruct((b, HEADS, DK), F32),
            jax.ShapeDtypeStruct((b, HEADS, LANES), F32),
        ],
        scratch_shapes=scratch,
        compiler_params=_cparams(("parallel", "arbitrary")),
        name="mlstm",
    )(z3, z3, z3, sm3, bias_row, c0, n0, m0)
    return h, c1, n1, m1[:, :, 0]


def _gla_kernel(q_ref, k_ref, v_ref, b_ref, s0_ref, h_ref, s1_ref, *scratch, chunk, sub, carried):
    L = chunk
    nb = L // sub
    bb, tb = q_ref.shape[0], q_ref.shape[1]
    scale = DK ** -0.5

    if carried:
        s_scr, kf_scr = scratch

        @pl.when(pl.program_id(1) == 0)
        def _():
            s_scr[...] = s0_ref[0]
    else:
        (kf_scr,) = scratch

    row = lax.broadcasted_iota(jnp.int32, (L, L), 0)
    col = lax.broadcasted_iota(jnp.int32, (L, L), 1)
    in_blk = (col // sub) == (row // sub)
    trow = lax.broadcasted_iota(jnp.int32, (sub, DK), 0)
    eye = lax.broadcasted_iota(jnp.int32, (DK, DK), 0) == lax.broadcasted_iota(jnp.int32, (DK, DK), 1)
    place = jnp.where(lax.broadcasted_iota(jnp.int32, (sub * DK, LANES), 0) // DK
                      == lax.broadcasted_iota(jnp.int32, (sub * DK, LANES), 1) % sub, 1.0, 0.0).astype(BF16)

    def chunk_group(bis, r):
        rows = pl.ds(r, L)
        pairs = [(bi, h) for bi in bis for h in range(HEADS)]

        def state(bi):
            return (s_scr, s_scr) if carried else (s0_ref.at[bi], s1_ref.at[bi])

        slabs = []
        for bi, h in pairs:
            ks = slice(h * DK, (h + 1) * DK)
            qf = q_ref[bi, rows, ks].astype(F32) * scale
            for i in range(nb):
                blk = pl.ds(pl.multiple_of(r + i * sub, sub), sub)
                bs = b_ref[bi, blk, ks]
                kblk = kf_scr[bi, blk, ks]
                qs = qf[i * sub:(i + 1) * sub, :]
                row_slabs = []
                for s in range(sub):
                    e = jnp.exp2(jnp.where(trow >= s, bs - bs[s:s + 1, :], -jnp.inf))
                    row_slabs.append(e * qs * kblk[s:s + 1, :])
                slabs.append(jnp.concatenate(row_slabs, axis=1))
        own = _dot(jnp.concatenate(slabs, axis=0).astype(BF16), place)

        st = {}
        for n, (bi, h) in enumerate(pairs):
            ks = slice(h * DK, (h + 1) * DK)
            vs = slice(h * DV, (h + 1) * DV)
            bh = b_ref[bi, rows, ks]
            qf = q_ref[bi, rows, ks].astype(F32) * scale
            kf = kf_scr[bi, rows, ks]
            s0 = state(bi)[0][h]
            d = dict(vb=v_ref[bi, rows, vs], s0=s0, s0b=s0.astype(BF16),
                     qhat=(qf * jnp.exp2(bh)).astype(BF16), a_own=own[n * L:(n + 1) * L, 0:L])
            d["qk"] = []
            for i in range(1, nb):
                r0 = i * sub
                anchor = bh[r0:r0 + 1, :]
                qt = qf[r0:r0 + sub, :] * jnp.exp2(bh[r0:r0 + sub, :] - anchor)
                kt = kf[0:r0, :] * jnp.exp2(anchor - bh[0:r0, :])
                kt = jnp.concatenate([kt, jnp.zeros((L - r0, DK), F32)], axis=0)
                d["qk"].append((qt.astype(BF16), kt.astype(BF16)))
            b_end = bh[L - 1:L, :]
            d["e_col"] = jnp.sum(jnp.where(eye, jnp.exp2(b_end), 0.0), axis=1, keepdims=True)
            d["ke_t"] = (kf * jnp.exp2(b_end - bh)).T.astype(BF16)
            st[bi, h] = d
        for key in pairs:
            d = st[key]
            d["o"] = _dot(d["qhat"], d["s0b"])
        for key in pairs:
            d = st[key]
            d["blocks"] = [_dot_nt(qt, kt) for qt, kt in d["qk"]]
        for key in pairs:
            d = st[key]
            d["s_new"] = d["e_col"] * d["s0"] + _dot(d["ke_t"], d["vb"])
        for key in pairs:
            d = st[key]
            if nb > 1:
                below = jnp.concatenate([jnp.zeros((sub, L), F32)] + d["blocks"], axis=0)
                a_intra = jnp.where(in_blk, d["a_own"], below)
                d["o"] = d["o"] + _dot(a_intra.astype(BF16), d["vb"])
            else:
                d["o"] = d["o"] + _dot(d["a_own"], d["vb"].astype(F32))
        for bi, h in pairs:
            d = st[bi, h]
            d["hn"] = d["o"].astype(h_ref.dtype)
        for bi, h in pairs:
            d = st[bi, h]
            state(bi)[1][h] = d["s_new"]
            h_ref[bi, rows, h * DV:(h + 1) * DV] = d["hn"]

    for bi in range(bb):
        kf_scr[bi] = k_ref[bi].astype(F32)
    if tb == L:
        chunk_group(list(range(bb)), 0)
    else:
        assert bb == 1

        def loop_body(ci, carry):
            chunk_group([0], pl.multiple_of(ci * L, L))
            return carry

        lax.fori_loop(0, tb // L, loop_body, 0, unroll=2)

    if carried:
        @pl.when(pl.program_id(1) == pl.num_programs(1) - 1)
        def _():
            s1_ref[0] = s_scr[...]


def _gla(z3, b3, s0, chunk, sub, tb, bb):
    b, t, _ = z3.shape
    assert t % tb == 0 and tb % chunk == 0 and chunk % sub == 0 and b % bb == 0
    carried = t // tb > 1
    assert not (carried and bb > 1)
    bt, st4, _ = _mixer_specs()
    scratch = [pltpu.VMEM((bb, tb, QK), F32)]
    if carried:
        scratch = [pltpu.VMEM((HEADS, DK, DV), F32)] + scratch
    h, s1 = pl.pallas_call(
        functools.partial(_gla_kernel, chunk=chunk, sub=sub, carried=carried),
        grid=(b // bb, t // tb),
        in_specs=[
            pl.BlockSpec((bb, tb, QK), bt(Z_GQ // QK)),
            pl.BlockSpec((bb, tb, QK), bt(Z_GK // QK)),
            pl.BlockSpec((bb, tb, VW), bt(Z_GV // VW)),
            pl.BlockSpec((bb, tb, QK), bt(0)),
            pl.BlockSpec((bb, HEADS, DK, DV), st4),
        ],
        out_specs=[
            pl.BlockSpec((bb, tb, VW), lambda i, j: (i, j, 0)),
            pl.BlockSpec((bb, HEADS, DK, DV), st4),
        ],
        out_shape=[
            jax.ShapeDtypeStruct((b, t, VW), BF16),
            jax.ShapeDtypeStruct((b, HEADS, DK, DV), F32),
        ],
        scratch_shapes=scratch,
        compiler_params=_cparams(("parallel", "arbitrary")),
        name="gla",
    )(z3, z3, z3, b3, s0)
    return h, s1


def _layer(x3, mem_k, mem_v, c0, n0, m0, s0, p, final_norm, m_chunk, g_chunk, g_sub, tb, bb, tq):
    b, t, _ = x3.shape
    n = b * t
    tmix = -(-t // tb) * tb
    x = x3.reshape(n, D_MODEL)
    x = _ffn(x, p["ffn1_norm"], p["ffn1_wg"], p["ffn1_wu"], p["ffn1_wd"])
    z, sm, bdec = _inproj(x, p["mix_norm"], p["w_in"], p["w_in_side"], p["wa_pad"], p["gla_ba"],
                          chunk=min(g_chunk, t))
    pad_t = lambda a, mode="constant": jnp.pad(a, ((0, 0), (0, tmix - t), (0, 0)), mode=mode)
    z3 = pad_t(z.reshape(b, t, Z_WIDTH))
    sm3 = pad_t(sm.reshape(b, t, LANES))
    b3 = pad_t(bdec.reshape(b, t, QK), "edge")
    hm, c1, n1, m1 = _mlstm(z3, sm3, p["bias_row"], c0, n0, m0, m_chunk, tb, bb, t)
    hg, s1 = _gla(z3, b3, s0, g_chunk, g_sub, tb, bb)
    x, q = _merge(z, hm[:, :t].reshape(n, VW), hg[:, :t].reshape(n, VW), x, p["mlstm_norm"], p["gla_norm"],
                  p["w_br_m"], p["w_br_g"], p["w_out"], p["ca_norm"], p["ca_wq"])
    if mem_k.ndim == 4:
        o = _attn_cache(q.reshape(b, t, D_MODEL), mem_k, mem_v, bb=4)
    else:
        o = _attn(q.reshape(b, t, D_MODEL), mem_k, mem_v, tq)
    y = _ffn(x, p["ffn2_norm"], p["ffn2_wg"], p["ffn2_wu"], p["ffn2_wd"], final_g=final_norm,
             res_a=o.reshape(n, D_MODEL), res_w=p["ca_wo"])
    return y.reshape(b, t, D_MODEL), (c1, n1, m1, s1)


def _permute_w_in(w_in):
    sizes = (QK, QK, VW, VW, HEADS, HEADS, QK, QK, VW, VW, G_RANK, D_MODEL, D_MODEL)
    offs = [0]
    for s in sizes:
        offs.append(offs[-1] + s)
    part = lambda i: w_in[:, offs[i]:offs[i + 1]]
    (mq, mk, mv, mo, mi, mf, gq, gk, gv, gr, ga, gate_m, gate_g) = [part(i) for i in range(len(sizes))]
    pad = jnp.zeros((D_MODEL, LANES - 2 * HEADS - G_RANK), w_in.dtype)
    w_main = jnp.concatenate([mq, mk, mv, gq, gk, gv, 0.5 * mo, 0.5 * gr, 0.5 * gate_m, 0.5 * gate_g], axis=1)
    w_side = jnp.concatenate([mi, mf, ga, pad], axis=1)
    return w_main.astype(BF16), w_side.astype(BF16)


def kernel(x_prompt, x_sample, mem_prompt, state_mlstm_C, state_mlstm_n, state_mlstm_m, state_gla_S, cache_mem_k, cache_mem_v, ffn1_norm, ffn1_wg, ffn1_wu, ffn1_wd, mix_norm, w_in, b_if, gla_wa2, gla_ba, mlstm_norm, gla_norm, w_br_m, w_br_g, w_out, ca_norm, mem_norm, ca_wq, ca_wk, ca_wv, ca_wo, ffn2_norm, ffn2_wg, ffn2_wu, ffn2_wd, final_norm):
    depth = ffn1_norm.shape[0]
    assert depth == 1
    l = 0
    bp, tp, _ = x_prompt.shape
    bs, ts, _ = x_sample.shape
    row = lambda v: v.reshape(1, -1).astype(F32)
    bias_row = jnp.pad(b_if[l].astype(F32)[None, :], ((0, 0), (0, LANES - 2 * HEADS)))
    wa_pad = jnp.pad(gla_wa2[l].astype(F32), ((SM_A, LANES - SM_A - G_RANK), (0, 0)))
    w_in_main, w_in_side = _permute_w_in(w_in[l])
    p = {
        "w_in": w_in_main, "w_in_side": w_in_side,
        "ffn1_norm": row(ffn1_norm[l]), "ffn1_wg": (0.5 * ffn1_wg[l]).astype(BF16),
        "ffn1_wu": ffn1_wu[l].astype(BF16),
        "ffn1_wd": ffn1_wd[l].astype(BF16),
        "mix_norm": row(mix_norm[l]),
        "bias_row": bias_row, "wa_pad": wa_pad, "gla_ba": row(gla_ba[l]),
        "mlstm_norm": row(mlstm_norm[l]), "gla_norm": row(gla_norm[l]),
        "w_br_m": w_br_m[l].astype(BF16), "w_br_g": w_br_g[l].astype(BF16), "w_out": w_out[l].astype(BF16),
        "ca_norm": row(ca_norm[l]), "ca_wq": ca_wq[l].astype(BF16), "ca_wo": ca_wo[l].astype(BF16),
        "ffn2_norm": row(ffn2_norm[l]), "ffn2_wg": (0.5 * ffn2_wg[l]).astype(BF16),
        "ffn2_wu": ffn2_wu[l].astype(BF16),
        "ffn2_wd": ffn2_wd[l].astype(BF16),
    }
    fin = row(final_norm)

    mem2 = mem_prompt.reshape(bp * N_MEM, D_MODEL)
    mk_p = _normproj(mem2, row(mem_norm[l]), ca_wk[l].astype(BF16), F32, name="memk")
    mv_p = _normproj(mem2, row(mem_norm[l]), ca_wv[l].astype(BF16), F32, name="memv")
    mk_p = mk_p.reshape(bp, N_MEM, D_MODEL)
    mv_p = mv_p.reshape(bp, N_MEM, D_MODEL)
    zc = jnp.zeros((bp, HEADS, DV, DK), F32)
    zn = jnp.zeros((bp, HEADS, DK), F32)
    zm = jnp.zeros((bp, HEADS), F32)
    zs = jnp.zeros((bp, HEADS, DK, DV), F32)
    yp, (cp, np_, mp, sp) = _layer(x_prompt, mk_p, mv_p, zc, zn, zm, zs, p, fin,
                                   m_chunk=128, g_chunk=64, g_sub=8, tb=512, bb=1, tq=1024)

    tpad = -(-ts // SUBLANES) * SUBLANES
    ys, (cs, ns, ms, ss) = _layer(x_sample, cache_mem_k[l], cache_mem_v[l],
                                  state_mlstm_C[l], state_mlstm_n[l], state_mlstm_m[l], state_gla_S[l],
                                  p, fin, m_chunk=tpad, g_chunk=tpad, g_sub=tpad, tb=tpad, bb=16, tq=None)

    st = lambda a: a[None]
    return (yp, ys, st(cp), st(np_), st(mp), st(sp),
            st(mk_p.reshape(bp, N_MEM, HEADS, C_HD)), st(mv_p.reshape(bp, N_MEM, HEADS, C_HD)),
            st(cs), st(ns), st(ms), st(ss))
```

```python
import functools
import math

import jax
import jax.numpy as jnp
from jax import lax
from jax.experimental import pallas as pl
from jax.experimental.pallas import tpu as pltpu

F32 = jnp.float32
BF16 = jnp.bfloat16

D_MODEL = 1024
D_FF = 2816
HEADS = 4
DK = 128
DV = 256
QK = HEADS * DK
VW = HEADS * DV
G_RANK = 16
G_TAU = 16.0
N_MEM = 256
C_HD = D_MODEL // HEADS
EPS = 1e-6
LOG2E = math.log2(math.e)
LANES = 128
SUBLANES = 8

Z_MQ = 0
Z_MK = Z_MQ + QK
Z_MV = Z_MK + QK
Z_GQ = Z_MV + VW
Z_GK = Z_GQ + QK
Z_GV = Z_GK + QK
Z_MO = Z_GV + VW
Z_GR = Z_MO + VW
Z_GATE_M = Z_GR + VW
Z_GATE_G = Z_GATE_M + D_MODEL
Z_WIDTH = Z_GATE_G + D_MODEL
SM_I = 0
SM_F = HEADS
SM_A = 2 * HEADS

VMEM_LIMIT = 56 * 1024 * 1024


def _cparams(sem):
    return pltpu.CompilerParams(dimension_semantics=sem, vmem_limit_bytes=VMEM_LIMIT)


def _rms(x, g):
    return x * lax.rsqrt(jnp.mean(x * x, axis=-1, keepdims=True) + EPS) * g


def _two_sigmoid_of_twice(xh):
    return jnp.tanh(xh) + 1.0


def _log_sigmoid(x):
    return jnp.minimum(x, 0.0) - jnp.log(1.0 + jnp.exp(-jnp.abs(x)))


def _dot(a, b):
    return jnp.dot(a, b, preferred_element_type=F32)


def _dot_nt(a, b):
    return lax.dot_general(a, b, (((1,), (1,)), ((), ())), preferred_element_type=F32)


def _split3(x):
    hi = x.astype(BF16)
    r1 = x - hi.astype(F32)
    mid = r1.astype(BF16)
    lo = (r1 - mid.astype(F32)).astype(BF16)
    return hi, mid, lo


def _dot_exact_left(m_bf16, x, terms=3):
    return sum(_dot(m_bf16, p) for p in _split3(x)[:terms])


def _resident(shape):
    zeros = (0,) * len(shape)
    return pl.BlockSpec(shape, lambda *_: zeros, pipeline_mode=pl.Buffered(1))


def _ffn_kernel(*refs, final, fused_res):
    refs = list(refs)
    if fused_res:
        a_ref, wr_ref = refs[:2]
        refs = refs[2:]
    x_ref, g_ref, wg_ref, wu_ref, wd_ref = refs[:5]
    refs = refs[5:]
    if final:
        fg_ref = refs.pop(0)
    (o_ref,) = refs
    x = x_ref[...]
    if fused_res:
        x = x + _dot(a_ref[...], wr_ref[...])
    hn = _rms(x, g_ref[...]).astype(BF16)
    a = _dot(hn, wg_ref[...])
    u = _dot(hn, wu_ref[...])
    act = (a * _two_sigmoid_of_twice(a) * u).astype(BF16)
    y = x + 0.5 * _dot(act, wd_ref[...])
    if final:
        y = _rms(y, fg_ref[...])
    o_ref[...] = y


def _ffn(x, g, wg, wu, wd, final_g=None, res_a=None, res_w=None, tm=512):
    n = x.shape[0]
    tm = min(tm, n)
    assert n % tm == 0
    final = final_g is not None
    fused_res = res_a is not None
    row = pl.BlockSpec((tm, D_MODEL), lambda i: (i, 0))
    in_specs, args = [], []
    if fused_res:
        in_specs += [row, _resident((D_MODEL, D_MODEL))]
        args += [res_a, res_w]
    in_specs += [row, _resident((1, D_MODEL)), _resident((D_MODEL, D_FF)), _resident((D_MODEL, D_FF)),
                 _resident((D_FF, D_MODEL))]
    args += [x, g, wg, wu, wd]
    if final:
        in_specs.append(_resident((1, D_MODEL)))
        args.append(final_g)
    return pl.pallas_call(
        functools.partial(_ffn_kernel, final=final, fused_res=fused_res),
        grid=(n // tm,),
        in_specs=in_specs,
        out_specs=row,
        out_shape=jax.ShapeDtypeStruct((n, D_MODEL), F32),
        compiler_params=_cparams(("parallel",)),
        name="ffn_final" if final else "ffn",
    )(*args)


def _normproj_kernel(x_ref, g_ref, w_ref, o_ref):
    o_ref[...] = _dot(_rms(x_ref[...], g_ref[...]).astype(BF16), w_ref[...]).astype(o_ref.dtype)


def _normproj(x, g, w, out_dtype, tm=512, name="normproj"):
    n = x.shape[0]
    width = w.shape[1]
    tm = min(tm, n)
    assert n % tm == 0
    return pl.pallas_call(
        _normproj_kernel,
        grid=(n // tm,),
        in_specs=[pl.BlockSpec((tm, D_MODEL), lambda i: (i, 0)), _resident((1, D_MODEL)),
                  _resident((D_MODEL, width))],
        out_specs=pl.BlockSpec((tm, width), lambda i: (i, 0)),
        out_shape=jax.ShapeDtypeStruct((n, width), out_dtype),
        compiler_params=_cparams(("parallel",)),
        name=name,
    )(x, g, w)


def _inproj_kernel(x_ref, g_ref, w_ref, ws_ref, wa_ref, ba_ref, z_ref, s_ref, b_ref, *, chunk):
    tm = x_ref.shape[0]
    hn = _rms(x_ref[...], g_ref[...]).astype(BF16)
    sm = _dot(hn, ws_ref[...])
    s_ref[...] = sm
    wa_hi = wa_ref[...].astype(BF16)
    wa_lo = (wa_ref[...] - wa_hi.astype(F32)).astype(BF16)
    sm_hi = sm.astype(BF16)
    sm_lo = (sm - sm_hi.astype(F32)).astype(BF16)
    a_raw = _dot(sm_hi, wa_hi) + _dot(sm_lo, wa_hi) + _dot(sm_hi, wa_lo) + ba_ref[...]
    z_ref[...] = _dot(hn, w_ref[...]).astype(z_ref.dtype)
    la = _log_sigmoid(a_raw) * (LOG2E / G_TAU)
    span = min(tm, 2 * LANES)
    assert span % chunk == 0 and tm % span == 0
    row = lax.broadcasted_iota(jnp.int32, (span, span), 0)
    col = lax.broadcasted_iota(jnp.int32, (span, span), 1)
    tri = jnp.where((col <= row) & (col // chunk == row // chunk), 1.0, 0.0).astype(BF16)
    for c in range(tm // span):
        b_ref[c * span:(c + 1) * span, :] = _dot_exact_left(tri, la[c * span:(c + 1) * span, :], terms=2)


def _inproj(x, g, w, w_side, wa_pad, ba, chunk, tm=512):
    n = x.shape[0]
    tm = min(tm, n)
    assert n % tm == 0 and tm % chunk == 0
    row = lambda width: pl.BlockSpec((tm, width), lambda i: (i, 0))
    return pl.pallas_call(
        functools.partial(_inproj_kernel, chunk=chunk),
        grid=(n // tm,),
        in_specs=[row(D_MODEL), _resident((1, D_MODEL)), _resident((D_MODEL, Z_WIDTH)),
                  _resident((D_MODEL, LANES)), _resident((LANES, QK)), _resident((1, QK))],
        out_specs=[row(Z_WIDTH), row(LANES), row(QK)],
        out_shape=[jax.ShapeDtypeStruct((n, Z_WIDTH), BF16), jax.ShapeDtypeStruct((n, LANES), F32),
                   jax.ShapeDtypeStruct((n, QK), F32)],
        compiler_params=_cparams(("parallel",)),
        name="inproj",
    )(x, g, w, w_side, wa_pad, ba)


def _head_norm(h, gn, gate):
    outs = []
    for hd in range(HEADS):
        cs = slice(hd * DV, (hd + 1) * DV)
        o = h[:, cs].astype(F32)
        outs.append(o * lax.rsqrt(jnp.mean(o * o, axis=-1, keepdims=True) + EPS) * gn[:, cs] * gate[:, cs])
    return jnp.concatenate(outs, axis=1).astype(BF16)


def _merge_kernel(og_ref, rg_ref, gm_ref, gg_ref, hm_ref, hg_ref, x_ref, nm_ref, ng_ref, wm_ref, wg_ref, wo_ref,
                  gq_ref, wq_ref, o_ref, q_ref):
    half_nm = 0.5 * nm_ref[...]
    hm = _head_norm(hm_ref[...], half_nm, _two_sigmoid_of_twice(og_ref[...].astype(F32)))
    ym = _dot(hm, wm_ref[...])
    rg = rg_ref[...].astype(F32)
    hg = _head_norm(hg_ref[...], ng_ref[...], rg * _two_sigmoid_of_twice(rg))
    yg = _dot(hg, wg_ref[...])
    y = 0.5 * (_two_sigmoid_of_twice(gm_ref[...].astype(F32)) * ym
               + _two_sigmoid_of_twice(gg_ref[...].astype(F32)) * yg)
    x = x_ref[...] + _dot(y.astype(BF16), wo_ref[...])
    o_ref[...] = x
    q_ref[...] = _dot(_rms(x, gq_ref[...]).astype(BF16), wq_ref[...]).astype(q_ref.dtype)


def _merge(z, hm, hg, x, nm, ng, wm, wg, wo, gq, wq, tm=512):
    n = x.shape[0]
    tm = min(tm, n)
    assert n % tm == 0
    row = lambda i: (i, 0)
    zcol = lambda off, width: pl.BlockSpec((tm, width), lambda i: (i, off // width))
    return pl.pallas_call(
        _merge_kernel,
        grid=(n // tm,),
        in_specs=[
            zcol(Z_MO, VW),
            zcol(Z_GR, VW),
            zcol(Z_GATE_M, D_MODEL),
            zcol(Z_GATE_G, D_MODEL),
            pl.BlockSpec((tm, VW), row),
            pl.BlockSpec((tm, VW), row),
            pl.BlockSpec((tm, D_MODEL), row),
            _resident((1, VW)),
            _resident((1, VW)),
            _resident((VW, D_MODEL)),
            _resident((VW, D_MODEL)),
            _resident((D_MODEL, D_MODEL)),
            _resident((1, D_MODEL)),
            _resident((D_MODEL, D_MODEL)),
        ],
        out_specs=[pl.BlockSpec((tm, D_MODEL), row), pl.BlockSpec((tm, D_MODEL), row)],
        out_shape=[jax.ShapeDtypeStruct((n, D_MODEL), F32), jax.ShapeDtypeStruct((n, D_MODEL), BF16)],
        compiler_params=_cparams(("parallel",)),
        name="merge",
    )(z, z, z, z, hm, hg, x, nm, ng, wm, wg, wo, gq, wq)


def _attn_kernel(q_ref, k_ref, v_ref, o_ref):
    scale = C_HD ** -0.5
    cols = [slice(h * C_HD, (h + 1) * C_HD) for h in range(HEADS)]
    scores = [_dot_nt(q_ref[0, :, cs], k_ref[0, :, cs].astype(BF16)) for cs in cols]
    probs = []
    for s in scores:
        s = s * (scale * LOG2E)
        p = jnp.exp2(s - jnp.max(s, axis=-1, keepdims=True))
        probs.append((p / jnp.sum(p, axis=-1, keepdims=True)).astype(BF16))
    outs = [_dot(p, v_ref[0, :, cs].astype(BF16)) for p, cs in zip(probs, cols)]
    for o, cs in zip(outs, cols):
        o_ref[0, :, cs] = o.astype(o_ref.dtype)


def _attn(q, k, v, tq):
    b, t, _ = q.shape
    assert t % tq == 0
    return pl.pallas_call(
        _attn_kernel,
        grid=(b, t // tq),
        in_specs=[
            pl.BlockSpec((1, tq, D_MODEL), lambda i, j: (i, j, 0)),
            pl.BlockSpec((1, N_MEM, D_MODEL), lambda i, j: (i, 0, 0)),
            pl.BlockSpec((1, N_MEM, D_MODEL), lambda i, j: (i, 0, 0)),
        ],
        out_specs=pl.BlockSpec((1, tq, D_MODEL), lambda i, j: (i, j, 0)),
        out_shape=jax.ShapeDtypeStruct((b, t, D_MODEL), BF16),
        compiler_params=_cparams(("parallel", "arbitrary")),
        name="attn",
    )(q, k, v)


CACHE_HALVES = C_HD // LANES
CACHE_ROWS = HEADS * CACHE_HALVES


def _attn_cache_kernel(q_ref, k_ref, v_ref, o_ref, *, t):
    scale = C_HD ** -0.5
    nq = t * HEADS
    lane = lax.broadcasted_iota(jnp.int32, (nq, LANES), 1)
    rowi = lax.broadcasted_iota(jnp.int32, (nq, LANES), 0)
    valid = ((lane % CACHE_ROWS) // HEADS == 0) & (lane % HEADS == rowi % HEADS)
    n_tiles = N_MEM * CACHE_ROWS // LANES
    seqs = range(q_ref.shape[0])
    scores = [_dot_nt(q_ref[bi], k_ref[bi].astype(BF16)) for bi in seqs]
    probs = []
    for s in scores:
        tiles = []
        for j in range(n_tiles):
            cs = slice(j * LANES, (j + 1) * LANES)
            sj = s[0:nq, cs] + pltpu.roll(s[nq:2 * nq, cs], LANES - HEADS, 1)
            tiles.append(jnp.where(valid, sj * (scale * LOG2E), -jnp.inf))
        mx = functools.reduce(jnp.maximum, [jnp.max(x, axis=1, keepdims=True) for x in tiles])
        ps = [jnp.exp2(x - mx) for x in tiles]
        den = functools.reduce(jnp.add, [jnp.sum(x, axis=1, keepdims=True) for x in ps])
        inv = 1.0 / den
        p0 = jnp.concatenate([x * inv for x in ps], axis=1)
        p1 = jnp.concatenate([pltpu.roll(x * inv, HEADS, 1) for x in ps], axis=1)
        probs.append(jnp.concatenate([p0, p1], axis=0).astype(BF16))
    outs = [_dot(p, v_ref[bi].astype(BF16)) for p, bi in zip(probs, seqs)]
    for o, bi in zip(outs, seqs):
        o_ref[bi] = o.astype(o_ref.dtype)


def _attn_cache(q, k, v, bb):
    assert CACHE_HALVES == 2
    b, t, _ = q.shape
    assert b % bb == 0
    nq = t * HEADS
    qv = q.reshape(b, t, HEADS, CACHE_HALVES, LANES).transpose(0, 3, 1, 2, 4)
    qv = qv.reshape(b, CACHE_HALVES * nq, LANES)

    def cache_view(a):
        a = a.reshape(b, N_MEM, HEADS, CACHE_HALVES, LANES).transpose(0, 1, 3, 2, 4)
        return a.reshape(b, N_MEM * CACHE_ROWS, LANES)

    o = pl.pallas_call(
        functools.partial(_attn_cache_kernel, t=t),
        grid=(b // bb,),
        in_specs=[
            pl.BlockSpec((bb, CACHE_HALVES * nq, LANES), lambda i: (i, 0, 0)),
            pl.BlockSpec((bb, N_MEM * CACHE_ROWS, LANES), lambda i: (i, 0, 0)),
            pl.BlockSpec((bb, N_MEM * CACHE_ROWS, LANES), lambda i: (i, 0, 0)),
        ],
        out_specs=pl.BlockSpec((bb, CACHE_HALVES * nq, LANES), lambda i: (i, 0, 0)),
        out_shape=jax.ShapeDtypeStruct((b, CACHE_HALVES * nq, LANES), BF16),
        compiler_params=_cparams(("parallel",)),
        name="attn_cache",
    )(qv, cache_view(k), cache_view(v))
    return o.reshape(b, CACHE_HALVES, t, HEADS, LANES).transpose(0, 2, 3, 1, 4).reshape(b, t, D_MODEL)


def _tile_lanes(rep, width):
    if width <= LANES:
        return rep[:, :width]
    return jnp.concatenate([rep] * (width // LANES), axis=1)


def _mlstm_kernel(q_ref, k_ref, v_ref, sm_ref, bias_ref, c0_ref, n0_ref, m0_ref,
                  h_ref, c1_ref, n1_ref, m1_ref, *scratch, chunk, t_valid, carried):
    L = chunk
    bb, tb = q_ref.shape[0], q_ref.shape[1]
    scale = DK ** -0.5

    if carried:
        c_scr, n_scr, m_scr = scratch

        @pl.when(pl.program_id(1) == 0)
        def _():
            c_scr[...] = c0_ref[0]
            n_scr[...] = n0_ref[0]
            m_scr[...] = m0_ref[0]

    row = lax.broadcasted_iota(jnp.int32, (L, L), 0)
    col = lax.broadcasted_iota(jnp.int32, (L, L), 1)
    tri = col <= row
    tri_lo = jnp.where(tri, 1.0, 0.0).astype(BF16)
    lane = lax.broadcasted_iota(jnp.int32, (L, LANES), 1)
    spread = jnp.where(lax.broadcasted_iota(jnp.int32, (LANES, 2 * HEADS * LANES), 0)
                       == lax.broadcasted_iota(jnp.int32, (LANES, 2 * HEADS * LANES), 1) // LANES,
                       1.0, 0.0).astype(BF16)
    pick = jnp.where(lax.broadcasted_iota(jnp.int32, (SUBLANES, LANES), 0)
                     == lax.broadcasted_iota(jnp.int32, (SUBLANES, LANES), 1), 1.0, 0.0).astype(BF16)

    def chunk_group(bis, r):
        rows = pl.ds(r, L)
        cols_of, rows_of = {}, {}
        for bi in bis:
            g = sm_ref[bi, rows, :] + bias_ref[...]
            lf_all = _log_sigmoid(g)
            ig_all = g
            if t_valid < L:
                valid = lax.broadcasted_iota(jnp.int32, (L, 1), 0) < t_valid
                lf_all = jnp.where(valid, lf_all, 0.0)
                ig_all = jnp.where(valid, g, -1e30)
            f_all = _dot_exact_left(tri_lo, lf_all)
            x = jnp.where(lane < SM_F, ig_all, f_all) * LOG2E
            parts = _split3(x)
            cols_of[bi] = sum(_dot(p, spread) for p in parts)
            rows_of[bi] = sum(_dot_nt(pick, p) for p in parts)
        pairs = [(bi, h) for bi in bis for h in range(HEADS)]

        def state(bi):
            if carried:
                return c_scr, n_scr, m_scr, c_scr, n_scr, m_scr
            return (c0_ref.at[bi], n0_ref.at[bi], m0_ref.at[bi], c1_ref.at[bi], n1_ref.at[bi], m1_ref.at[bi])

        st = {}
        for bi, h in pairs:
            c_in, n_in, m_in = state(bi)[:3]
            slab = lambda c, bi=bi: cols_of[bi][:, c * LANES:(c + 1) * LANES]
            f_rep = slab(SM_F + h)
            ig_rep = slab(SM_I + h)
            f_row = rows_of[bi][SM_F + h:SM_F + h + 1, :]
            ig_row = rows_of[bi][SM_I + h:SM_I + h + 1, :]
            dmat = jnp.where(tri, _tile_lanes(f_rep, L) - f_row + ig_row, -jnp.inf)
            m_inter = m_in[h:h + 1, :] * LOG2E + f_rep
            m = jnp.maximum(m_inter, jnp.max(dmat, axis=1, keepdims=True))
            w = jnp.exp2(dmat - _tile_lanes(m - math.log2(scale), L))
            a = jnp.exp2(m_inter - m)
            m_end = m[L - 1:L, :]
            w_rep = jnp.exp2(f_rep[L - 1:L, :] - f_rep + ig_rep - (m_end - math.log2(scale)))
            st[bi, h] = dict(m=m, w=w, a=a, m_end=m_end, w_rep=w_rep, c0=c_in[h], n0=n_in[h:h + 1, :],
                             qb=q_ref[bi, rows, h * DK:(h + 1) * DK], kb=k_ref[bi, rows, h * DK:(h + 1) * DK],
                             vb=v_ref[bi, rows, h * DV:(h + 1) * DV])
        for key in pairs:
            d = st[key]
            d["s"] = _dot_nt(d["qb"], d["kb"]) * d["w"]
        for key in pairs:
            d = st[key]
            d["inter"] = _dot_nt(d["qb"], d["c0"].astype(BF16))
        for key in pairs:
            d = st[key]
            d["sv"] = _dot(d["s"].astype(BF16), d["vb"])
        for key in pairs:
            d = st[key]
            vw_t = (d["vb"].astype(F32) * _tile_lanes(d["w_rep"], DV)).T.astype(BF16)
            d["c_new"] = d["a"][L - 1:L, :] * d["c0"] + _dot(vw_t, d["kb"])
        for bi, h in pairs:
            d = st[bi, h]
            a, m = d["a"], d["m"]
            qf = d["qb"].astype(F32)
            kf = d["kb"].astype(F32)
            num = _tile_lanes(a, DV) * d["inter"] + d["sv"]
            den = a * jnp.sum(qf * d["n0"], axis=1, keepdims=True) + jnp.sum(d["s"], axis=1, keepdims=True)
            hh = num / _tile_lanes(jnp.maximum(jnp.abs(den), jnp.exp2(-m)), DV)
            d["n_new"] = a[L - 1:L, :] * d["n0"] + jnp.sum(kf * d["w_rep"], axis=0, keepdims=True)
            d["hn"] = hh.astype(h_ref.dtype)
        for bi, h in pairs:
            d = st[bi, h]
            c_out, n_out, m_out = state(bi)[3:]
            c_out[h] = d["c_new"]
            n_out[h:h + 1, :] = d["n_new"]
            m_out[h:h + 1, :] = d["m_end"] * (1.0 / LOG2E)
            h_ref[bi, rows, h * DV:(h + 1) * DV] = d["hn"]

    if tb == L:
        chunk_group(list(range(bb)), 0)
    else:
        assert bb == 1

        def loop_body(ci, carry):
            chunk_group([0], pl.multiple_of(ci * L, L))
            return carry

        lax.fori_loop(0, tb // L, loop_body, 0, unroll=2)

    if carried:
        @pl.when(pl.program_id(1) == pl.num_programs(1) - 1)
        def _():
            c1_ref[0] = c_scr[...]
            n1_ref[0] = n_scr[...]
            m1_ref[0] = m_scr[...]


def _mixer_specs():
    bt = lambda col: (lambda i, j: (i, j, col))
    return bt, (lambda i, j: (i, 0, 0, 0)), (lambda i, j: (i, 0, 0))


def _mlstm(z3, sm3, bias_row, c0, n0, m0, chunk, tb, bb, t_valid):
    b, t, _ = z3.shape
    assert t % tb == 0 and tb % chunk == 0 and b % bb == 0
    assert t_valid == t or (tb == t and chunk == t)
    carried = t // tb > 1
    assert not (carried and bb > 1)
    m0 = jnp.broadcast_to(m0[:, :, None], (b, HEADS, LANES))
    bt, st4, st3 = _mixer_specs()
    scratch = [pltpu.VMEM((HEADS, DV, DK), F32), pltpu.VMEM((HEADS, DK), F32),
               pltpu.VMEM((HEADS, LANES), F32)] if carried else []
    h, c1, n1, m1 = pl.pallas_call(
        functools.partial(_mlstm_kernel, chunk=chunk, t_valid=t_valid, carried=carried),
        grid=(b // bb, t // tb),
        in_specs=[
            pl.BlockSpec((bb, tb, QK), bt(Z_MQ // QK)),
            pl.BlockSpec((bb, tb, QK), bt(Z_MK // QK)),
            pl.BlockSpec((bb, tb, VW), bt(Z_MV // VW)),
            pl.BlockSpec((bb, tb, LANES), bt(0)),
            pl.BlockSpec((1, LANES), lambda i, j: (0, 0)),
            pl.BlockSpec((bb, HEADS, DV, DK), st4),
            pl.BlockSpec((bb, HEADS, DK), st3),
            pl.BlockSpec((bb, HEADS, LANES), st3),
        ],
        out_specs=[
            pl.BlockSpec((bb, tb, VW), lambda i, j: (i, j, 0)),
            pl.BlockSpec((bb, HEADS, DV, DK), st4),
            pl.BlockSpec((bb, HEADS, DK), st3),
            pl.BlockSpec((bb, HEADS, LANES), st3),
        ],
        out_shape=[
            jax.ShapeDtypeStruct((b, t, VW), BF16),
            jax.ShapeDtypeStruct((b, HEADS, DV, DK), F32),
            jax.ShapeDtypeStruct((b, HEADS, DK), F32),
            jax.ShapeDtypeStruct((b, HEADS, LANES), F32),
        ],
        scratch_shapes=scratch,
        compiler_params=_cparams(("parallel", "arbitrary")),
        name="mlstm",
    )(z3, z3, z3, sm3, bias_row, c0, n0, m0)
    return h, c1, n1, m1[:, :, 0]


def _gla_kernel(q_ref, k_ref, v_ref, b_ref, s0_ref, h_ref, s1_ref, *scratch, chunk, sub, carried):
    L = chunk
    nb = L // sub
    bb, tb = q_ref.shape[0], q_ref.shape[1]
    scale = DK ** -0.5

    if carried:
        s_scr, kf_scr = scratch

        @pl.when(pl.program_id(1) == 0)
        def _():
            s_scr[...] = s0_ref[0]
    else:
        (kf_scr,) = scratch

    row = lax.broadcasted_iota(jnp.int32, (L, L), 0)
    col = lax.broadcasted_iota(jnp.int32, (L, L), 1)
    in_blk = (col // sub) == (row // sub)
    trow = lax.broadcasted_iota(jnp.int32, (sub, DK), 0)
    eye = lax.broadcasted_iota(jnp.int32, (DK, DK), 0) == lax.broadcasted_iota(jnp.int32, (DK, DK), 1)
    place = jnp.where(lax.broadcasted_iota(jnp.int32, (sub * DK, LANES), 0) // DK
                      == lax.broadcasted_iota(jnp.int32, (sub * DK, LANES), 1) % sub, 1.0, 0.0).astype(BF16)

    def chunk_group(bis, r):
        rows = pl.ds(r, L)
        pairs = [(bi, h) for bi in bis for h in range(HEADS)]

        def state(bi):
            return (s_scr, s_scr) if carried else (s0_ref.at[bi], s1_ref.at[bi])

        slabs = []
        for bi, h in pairs:
            ks = slice(h * DK, (h + 1) * DK)
            qf = q_ref[bi, rows, ks].astype(F32) * scale
            for i in range(nb):
                blk = pl.ds(pl.multiple_of(r + i * sub, sub), sub)
                bs = b_ref[bi, blk, ks]
                kblk = kf_scr[bi, blk, ks]
                qs = qf[i * sub:(i + 1) * sub, :]
                row_slabs = []
                for s in range(sub):
                    e = jnp.exp2(jnp.where(trow >= s, bs - bs[s:s + 1, :], -jnp.inf))
                    row_slabs.append(e * qs * kblk[s:s + 1, :])
                slabs.append(jnp.concatenate(row_slabs, axis=1))
        own = _dot(jnp.concatenate(slabs, axis=0).astype(BF16), place)

        st = {}
        for n, (bi, h) in enumerate(pairs):
            ks = slice(h * DK, (h + 1) * DK)
            vs = slice(h * DV, (h + 1) * DV)
            bh = b_ref[bi, rows, ks]
            qf = q_ref[bi, rows, ks].astype(F32) * scale
            kf = kf_scr[bi, rows, ks]
            s0 = state(bi)[0][h]
            d = dict(vb=v_ref[bi, rows, vs], s0=s0, s0b=s0.astype(BF16),
                     qhat=(qf * jnp.exp2(bh)).astype(BF16), a_own=own[n * L:(n + 1) * L, 0:L])
            d["qk"] = []
            for i in range(1, nb):
                r0 = i * sub
                anchor = bh[r0:r0 + 1, :]
                qt = qf[r0:r0 + sub, :] * jnp.exp2(bh[r0:r0 + sub, :] - anchor)
                kt = kf[0:r0, :] * jnp.exp2(anchor - bh[0:r0, :])
                kt = jnp.concatenate([kt, jnp.zeros((L - r0, DK), F32)], axis=0)
                d["qk"].append((qt.astype(BF16), kt.astype(BF16)))
            b_end = bh[L - 1:L, :]
            d["e_col"] = jnp.sum(jnp.where(eye, jnp.exp2(b_end), 0.0), axis=1, keepdims=True)
            d["ke_t"] = (kf * jnp.exp2(b_end - bh)).T.astype(BF16)
            st[bi, h] = d
        for key in pairs:
            d = st[key]
            d["o"] = _dot(d["qhat"], d["s0b"])
        for key in pairs:
            d = st[key]
            d["blocks"] = [_dot_nt(qt, kt) for qt, kt in d["qk"]]
        for key in pairs:
            d = st[key]
            d["s_new"] = d["e_col"] * d["s0"] + _dot(d["ke_t"], d["vb"])
        for key in pairs:
            d = st[key]
            if nb > 1:
                below = jnp.concatenate([jnp.zeros((sub, L), F32)] + d["blocks"], axis=0)
                a_intra = jnp.where(in_blk, d["a_own"], below)
                d["o"] = d["o"] + _dot(a_intra.astype(BF16), d["vb"])
            else:
                d["o"] = d["o"] + _dot(d["a_own"], d["vb"].astype(F32))
        for bi, h in pairs:
            d = st[bi, h]
            d["hn"] = d["o"].astype(h_ref.dtype)
        for bi, h in pairs:
            d = st[bi, h]
            state(bi)[1][h] = d["s_new"]
            h_ref[bi, rows, h * DV:(h + 1) * DV] = d["hn"]

    for bi in range(bb):
        kf_scr[bi] = k_ref[bi].astype(F32)
    if tb == L:
        chunk_group(list(range(bb)), 0)
    else:
        assert bb == 1

        def loop_body(ci, carry):
            chunk_group([0], pl.multiple_of(ci * L, L))
            return carry

        lax.fori_loop(0, tb // L, loop_body, 0, unroll=2)

    if carried:
        @pl.when(pl.program_id(1) == pl.num_programs(1) - 1)
        def _():
            s1_ref[0] = s_scr[...]


def _gla(z3, b3, s0, chunk, sub, tb, bb):
    b, t, _ = z3.shape
    assert t % tb == 0 and tb % chunk == 0 and chunk % sub == 0 and b % bb == 0
    carried = t // tb > 1
    assert not (carried and bb > 1)
    bt, st4, _ = _mixer_specs()
    scratch = [pltpu.VMEM((bb, tb, QK), F32)]
    if carried:
        scratch = [pltpu.VMEM((HEADS, DK, DV), F32)] + scratch
    h, s1 = pl.pallas_call(
        functools.partial(_gla_kernel, chunk=chunk, sub=sub, carried=carried),
        grid=(b // bb, t // tb),
        in_specs=[
            pl.BlockSpec((bb, tb, QK), bt(Z_GQ // QK)),
            pl.BlockSpec((bb, tb, QK), bt(Z_GK // QK)),
            pl.BlockSpec((bb, tb, VW), bt(Z_GV // VW)),
            pl.BlockSpec((bb, tb, QK), bt(0)),
            pl.BlockSpec((bb, HEADS, DK, DV), st4),
        ],
        out_specs=[
            pl.BlockSpec((bb, tb, VW), lambda i, j: (i, j, 0)),
            pl.BlockSpec((bb, HEADS, DK, DV), st4),
        ],
        out_shape=[
            jax.ShapeDtypeStruct((b, t, VW), BF16),
            jax.ShapeDtypeStruct((b, HEADS, DK, DV), F32),
        ],
        scratch_shapes=scratch,
        compiler_params=_cparams(("parallel", "arbitrary")),
        name="gla",
    )(z3, z3, z3, b3, s0)
    return h, s1


def _layer(x3, mem_k, mem_v, c0, n0, m0, s0, p, final_norm, m_chunk, g_chunk, g_sub, tb, bb, tq):
    b, t, _ = x3.shape
    n = b * t
    tmix = -(-t // tb) * tb
    x = x3.reshape(n, D_MODEL)
    x = _ffn(x, p["ffn1_norm"], p["ffn1_wg"], p["ffn1_wu"], p["ffn1_wd"])
    z, sm, bdec = _inproj(x, p["mix_norm"], p["w_in"], p["w_in_side"], p["wa_pad"], p["gla_ba"],
                          chunk=min(g_chunk, t))
    pad_t = lambda a, mode="constant": jnp.pad(a, ((0, 0), (0, tmix - t), (0, 0)), mode=mode)
    z3 = pad_t(z.reshape(b, t, Z_WIDTH))
    sm3 = pad_t(sm.reshape(b, t, LANES))
    b3 = pad_t(bdec.reshape(b, t, QK), "edge")
    hm, c1, n1, m1 = _mlstm(z3, sm3, p["bias_row"], c0, n0, m0, m_chunk, tb, bb, t)
    hg, s1 = _gla(z3, b3, s0, g_chunk, g_sub, tb, bb)
    x, q = _merge(z, hm[:, :t].reshape(n, VW), hg[:, :t].reshape(n, VW), x, p["mlstm_norm"], p["gla_norm"],
                  p["w_br_m"], p["w_br_g"], p["w_out"], p["ca_norm"], p["ca_wq"])
    if mem_k.ndim == 4:
        o = _attn_cache(q.reshape(b, t, D_MODEL), mem_k, mem_v, bb=4)
    else:
        o = _attn(q.reshape(b, t, D_MODEL), mem_k, mem_v, tq)
    y = _ffn(x, p["ffn2_norm"], p["ffn2_wg"], p["ffn2_wu"], p["ffn2_wd"], final_g=final_norm,
             res_a=o.reshape(n, D_MODEL), res_w=p["ca_wo"])
    return y.reshape(b, t, D_MODEL), (c1, n1, m1, s1)


def _permute_w_in(w_in):
    sizes = (QK, QK, VW, VW, HEADS, HEADS, QK, QK, VW, VW, G_RANK, D_MODEL, D_MODEL)
    w_in = w_in.astype(BF16)
    offs = [0]
    for s in sizes:
        offs.append(offs[-1] + s)
    part = lambda i: w_in[:, offs[i]:offs[i + 1]]
    (mq, mk, mv, mo, mi, mf, gq, gk, gv, gr, ga, gate_m, gate_g) = [part(i) for i in range(len(sizes))]
    pad = jnp.zeros((D_MODEL, LANES - 2 * HEADS - G_RANK), w_in.dtype)
    w_main = jnp.concatenate([mq, mk, mv, gq, gk, gv, 0.5 * mo, 0.5 * gr, 0.5 * gate_m, 0.5 * gate_g], axis=1)
    w_side = jnp.concatenate([mi, mf, ga, pad], axis=1)
    return w_main, w_side


def kernel(x_prompt, x_sample, mem_prompt, state_mlstm_C, state_mlstm_n, state_mlstm_m, state_gla_S, cache_mem_k, cache_mem_v, ffn1_norm, ffn1_wg, ffn1_wu, ffn1_wd, mix_norm, w_in, b_if, gla_wa2, gla_ba, mlstm_norm, gla_norm, w_br_m, w_br_g, w_out, ca_norm, mem_norm, ca_wq, ca_wk, ca_wv, ca_wo, ffn2_norm, ffn2_wg, ffn2_wu, ffn2_wd, final_norm):
    depth = ffn1_norm.shape[0]
    assert depth == 1
    l = 0
    bp, tp, _ = x_prompt.shape
    bs, ts, _ = x_sample.shape
    row = lambda v: v.reshape(1, -1).astype(F32)
    bias_row = jnp.pad(b_if[l].astype(F32)[None, :], ((0, 0), (0, LANES - 2 * HEADS)))
    wa_pad = jnp.pad(gla_wa2[l].astype(F32), ((SM_A, LANES - SM_A - G_RANK), (0, 0)))
    w_in_main, w_in_side = _permute_w_in(w_in[l])
    p = {
        "w_in": w_in_main, "w_in_side": w_in_side,
        "ffn1_norm": row(ffn1_norm[l]), "ffn1_wg": (0.5 * ffn1_wg[l]).astype(BF16),
        "ffn1_wu": ffn1_wu[l].astype(BF16),
        "ffn1_wd": ffn1_wd[l].astype(BF16),
        "mix_norm": row(mix_norm[l]),
        "bias_row": bias_row, "wa_pad": wa_pad, "gla_ba": row(gla_ba[l]),
        "mlstm_norm": row(mlstm_norm[l]), "gla_norm": row(gla_norm[l]),
        "w_br_m": w_br_m[l].astype(BF16), "w_br_g": w_br_g[l].astype(BF16), "w_out": w_out[l].astype(BF16),
        "ca_norm": row(ca_norm[l]), "ca_wq": ca_wq[l].astype(BF16), "ca_wo": ca_wo[l].astype(BF16),
        "ffn2_norm": row(ffn2_norm[l]), "ffn2_wg": (0.5 * ffn2_wg[l]).astype(BF16),
        "ffn2_wu": ffn2_wu[l].astype(BF16),
        "ffn2_wd": ffn2_wd[l].astype(BF16),
    }
    fin = row(final_norm)

    mem2 = mem_prompt.reshape(bp * N_MEM, D_MODEL)
    mk_p = _normproj(mem2, row(mem_norm[l]), ca_wk[l].astype(BF16), F32, name="memk")
    mv_p = _normproj(mem2, row(mem_norm[l]), ca_wv[l].astype(BF16), F32, name="memv")
    mk_p = mk_p.reshape(bp, N_MEM, D_MODEL)
    mv_p = mv_p.reshape(bp, N_MEM, D_MODEL)
    zc = jnp.zeros((bp, HEADS, DV, DK), F32)
    zn = jnp.zeros((bp, HEADS, DK), F32)
    zm = jnp.zeros((bp, HEADS), F32)
    zs = jnp.zeros((bp, HEADS, DK, DV), F32)
    yp, (cp, np_, mp, sp) = _layer(x_prompt, mk_p, mv_p, zc, zn, zm, zs, p, fin,
                                   m_chunk=128, g_chunk=64, g_sub=8, tb=512, bb=1, tq=2048)

    tpad = -(-ts // SUBLANES) * SUBLANES
    ys, (cs, ns, ms, ss) = _layer(x_sample, cache_mem_k[l], cache_mem_v[l],
                                  state_mlstm_C[l], state_mlstm_n[l], state_mlstm_m[l], state_gla_S[l],
                                  p, fin, m_chunk=tpad, g_chunk=tpad, g_sub=tpad, tb=tpad, bb=16, tq=None)

    st = lambda a: a[None]
    return (yp, ys, st(cp), st(np_), st(mp), st(sp),
            st(mk_p.reshape(bp, N_MEM, HEADS, C_HD)), st(mv_p.reshape(bp, N_MEM, HEADS, C_HD)),
            st(cs), st(ns), st(ms), st(ss))
```

```python
import functools
import math

import jax
import jax.numpy as jnp
from jax import lax
from jax.experimental import pallas as pl
from jax.experimental.pallas import tpu as pltpu

F32 = jnp.float32
BF16 = jnp.bfloat16

D_MODEL = 1024
D_FF = 2816
HEADS = 4
DK = 128
DV = 256
QK = HEADS * DK
VW = HEADS * DV
G_RANK = 16
G_TAU = 16.0
N_MEM = 256
C_HD = D_MODEL // HEADS
EPS = 1e-6
LOG2E = math.log2(math.e)
LANES = 128
SUBLANES = 8

Z_MQ = 0
Z_MK = Z_MQ + QK
Z_MV = Z_MK + QK
Z_GQ = Z_MV + VW
Z_GK = Z_GQ + QK
Z_GV = Z_GK + QK
Z_MO = Z_GV + VW
Z_GR = Z_MO + VW
Z_GATE_M = Z_GR + VW
Z_GATE_G = Z_GATE_M + D_MODEL
Z_WIDTH = Z_GATE_G + D_MODEL
SM_I = 0
SM_F = HEADS
SM_A = 2 * HEADS

VMEM_LIMIT = 56 * 1024 * 1024


def _cparams(sem):
    return pltpu.CompilerParams(dimension_semantics=sem, vmem_limit_bytes=VMEM_LIMIT)


def _rms(x, g):
    return x * lax.rsqrt(jnp.mean(x * x, axis=-1, keepdims=True) + EPS) * g


def _two_sigmoid_of_twice(xh):
    return jnp.tanh(xh) + 1.0


def _log_sigmoid(x):
    return jnp.minimum(x, 0.0) - jnp.log(1.0 + jnp.exp(-jnp.abs(x)))


def _dot(a, b):
    return jnp.dot(a, b, preferred_element_type=F32)


def _dot_nt(a, b):
    return lax.dot_general(a, b, (((1,), (1,)), ((), ())), preferred_element_type=F32)


def _split3(x):
    hi = x.astype(BF16)
    r1 = x - hi.astype(F32)
    mid = r1.astype(BF16)
    lo = (r1 - mid.astype(F32)).astype(BF16)
    return hi, mid, lo


def _dot_exact_left(m_bf16, x, terms=3):
    return sum(_dot(m_bf16, p) for p in _split3(x)[:terms])


def _resident(shape):
    zeros = (0,) * len(shape)
    return pl.BlockSpec(shape, lambda *_: zeros, pipeline_mode=pl.Buffered(1))


def _ffn_kernel(*refs, final, fused_res):
    refs = list(refs)
    if fused_res:
        a_ref, wr_ref = refs[:2]
        refs = refs[2:]
    x_ref, g_ref, wg_ref, wu_ref, wd_ref = refs[:5]
    refs = refs[5:]
    if final:
        fg_ref = refs.pop(0)
    (o_ref,) = refs
    x = x_ref[...]
    if fused_res:
        x = x + _dot(a_ref[...], wr_ref[...])
    hn = _rms(x, g_ref[...]).astype(BF16)
    a = _dot(hn, wg_ref[...])
    u = _dot(hn, wu_ref[...])
    act = (a * _two_sigmoid_of_twice(a) * u).astype(BF16)
    y = x + 0.5 * _dot(act, wd_ref[...])
    if final:
        y = _rms(y, fg_ref[...])
    o_ref[...] = y


def _ffn(x, g, wg, wu, wd, final_g=None, res_a=None, res_w=None, tm=512):
    n = x.shape[0]
    tm = min(tm, n)
    assert n % tm == 0
    final = final_g is not None
    fused_res = res_a is not None
    row = pl.BlockSpec((tm, D_MODEL), lambda i: (i, 0))
    in_specs, args = [], []
    if fused_res:
        in_specs += [row, _resident((D_MODEL, D_MODEL))]
        args += [res_a, res_w]
    in_specs += [row, _resident((1, D_MODEL)), _resident((D_MODEL, D_FF)), _resident((D_MODEL, D_FF)),
                 _resident((D_FF, D_MODEL))]
    args += [x, g, wg, wu, wd]
    if final:
        in_specs.append(_resident((1, D_MODEL)))
        args.append(final_g)
    return pl.pallas_call(
        functools.partial(_ffn_kernel, final=final, fused_res=fused_res),
        grid=(n // tm,),
        in_specs=in_specs,
        out_specs=row,
        out_shape=jax.ShapeDtypeStruct((n, D_MODEL), F32),
        compiler_params=_cparams(("parallel",)),
        name="ffn_final" if final else "ffn",
    )(*args)


def _memkv_kernel(x_ref, g_ref, wk_ref, wv_ref, k_ref, v_ref):
    hn = _rms(x_ref[...], g_ref[...]).astype(BF16)
    k_ref[...] = _dot(hn, wk_ref[...])
    v_ref[...] = _dot(hn, wv_ref[...])


def _memkv(x, g, wk, wv, tm=512):
    n = x.shape[0]
    tm = min(tm, n)
    assert n % tm == 0
    row = pl.BlockSpec((tm, D_MODEL), lambda i: (i, 0))
    return pl.pallas_call(
        _memkv_kernel,
        grid=(n // tm,),
        in_specs=[row, _resident((1, D_MODEL)), _resident((D_MODEL, D_MODEL)), _resident((D_MODEL, D_MODEL))],
        out_specs=[row, row],
        out_shape=[jax.ShapeDtypeStruct((n, D_MODEL), F32)] * 2,
        compiler_params=_cparams(("parallel",)),
        name="memkv",
    )(x, g, wk, wv)


def _inproj_kernel(x_ref, g_ref, w_ref, ws_ref, wa_ref, ba_ref, z_ref, s_ref, b_ref, *, chunk):
    tm = x_ref.shape[0]
    hn = _rms(x_ref[...], g_ref[...]).astype(BF16)
    sm = _dot(hn, ws_ref[...])
    s_ref[...] = sm
    wa_hi = wa_ref[...].astype(BF16)
    wa_lo = (wa_ref[...] - wa_hi.astype(F32)).astype(BF16)
    sm_hi = sm.astype(BF16)
    sm_lo = (sm - sm_hi.astype(F32)).astype(BF16)
    a_raw = _dot(sm_hi, wa_hi) + _dot(sm_lo, wa_hi) + _dot(sm_hi, wa_lo) + ba_ref[...]
    z_ref[...] = _dot(hn, w_ref[...]).astype(z_ref.dtype)
    la = _log_sigmoid(a_raw) * (LOG2E / G_TAU)
    span = min(tm, 2 * LANES)
    assert span % chunk == 0 and tm % span == 0
    row = lax.broadcasted_iota(jnp.int32, (span, span), 0)
    col = lax.broadcasted_iota(jnp.int32, (span, span), 1)
    tri = jnp.where((col <= row) & (col // chunk == row // chunk), 1.0, 0.0).astype(BF16)
    for c in range(tm // span):
        b_ref[c * span:(c + 1) * span, :] = _dot_exact_left(tri, la[c * span:(c + 1) * span, :], terms=2)


def _inproj(x, g, w, w_side, wa_pad, ba, chunk, tm=512):
    n = x.shape[0]
    tm = min(tm, n)
    assert n % tm == 0 and tm % chunk == 0
    row = lambda width: pl.BlockSpec((tm, width), lambda i: (i, 0))
    return pl.pallas_call(
        functools.partial(_inproj_kernel, chunk=chunk),
        grid=(n // tm,),
        in_specs=[row(D_MODEL), _resident((1, D_MODEL)), _resident((D_MODEL, Z_WIDTH)),
                  _resident((D_MODEL, LANES)), _resident((LANES, QK)), _resident((1, QK))],
        out_specs=[row(Z_WIDTH), row(LANES), row(QK)],
        out_shape=[jax.ShapeDtypeStruct((n, Z_WIDTH), BF16), jax.ShapeDtypeStruct((n, LANES), F32),
                   jax.ShapeDtypeStruct((n, QK), F32)],
        compiler_params=_cparams(("parallel",)),
        name="inproj",
    )(x, g, w, w_side, wa_pad, ba)


def _head_norm(h, gn, gate):
    outs = []
    for hd in range(HEADS):
        cs = slice(hd * DV, (hd + 1) * DV)
        o = h[:, cs].astype(F32)
        outs.append(o * lax.rsqrt(jnp.mean(o * o, axis=-1, keepdims=True) + EPS) * gn[:, cs] * gate[:, cs])
    return jnp.concatenate(outs, axis=1).astype(BF16)


def _merge_kernel(og_ref, rg_ref, gm_ref, gg_ref, hm_ref, hg_ref, x_ref, nm_ref, ng_ref, wm_ref, wg_ref, wo_ref,
                  gq_ref, wq_ref, o_ref, q_ref):
    half_nm = 0.5 * nm_ref[...]
    hm = _head_norm(hm_ref[...], half_nm, _two_sigmoid_of_twice(og_ref[...].astype(F32)))
    ym = _dot(hm, wm_ref[...])
    rg = rg_ref[...].astype(F32)
    hg = _head_norm(hg_ref[...], ng_ref[...], rg * _two_sigmoid_of_twice(rg))
    yg = _dot(hg, wg_ref[...])
    y = 0.5 * (_two_sigmoid_of_twice(gm_ref[...].astype(F32)) * ym
               + _two_sigmoid_of_twice(gg_ref[...].astype(F32)) * yg)
    x = x_ref[...] + _dot(y.astype(BF16), wo_ref[...])
    o_ref[...] = x
    q_ref[...] = _dot(_rms(x, gq_ref[...]).astype(BF16), wq_ref[...]).astype(q_ref.dtype)


def _merge(z, hm, hg, x, nm, ng, wm, wg, wo, gq, wq, tm=512):
    n = x.shape[0]
    tm = min(tm, n)
    assert n % tm == 0
    row = lambda i: (i, 0)
    zcol = lambda off, width: pl.BlockSpec((tm, width), lambda i: (i, off // width))
    return pl.pallas_call(
        _merge_kernel,
        grid=(n // tm,),
        in_specs=[
            zcol(Z_MO, VW),
            zcol(Z_GR, VW),
            zcol(Z_GATE_M, D_MODEL),
            zcol(Z_GATE_G, D_MODEL),
            pl.BlockSpec((tm, VW), row),
            pl.BlockSpec((tm, VW), row),
            pl.BlockSpec((tm, D_MODEL), row),
            _resident((1, VW)),
            _resident((1, VW)),
            _resident((VW, D_MODEL)),
            _resident((VW, D_MODEL)),
            _resident((D_MODEL, D_MODEL)),
            _resident((1, D_MODEL)),
            _resident((D_MODEL, D_MODEL)),
        ],
        out_specs=[pl.BlockSpec((tm, D_MODEL), row), pl.BlockSpec((tm, D_MODEL), row)],
        out_shape=[jax.ShapeDtypeStruct((n, D_MODEL), F32), jax.ShapeDtypeStruct((n, D_MODEL), BF16)],
        compiler_params=_cparams(("parallel",)),
        name="merge",
    )(z, z, z, z, hm, hg, x, nm, ng, wm, wg, wo, gq, wq)


def _attn_kernel(q_ref, k_ref, v_ref, o_ref):
    scale = C_HD ** -0.5
    cols = [slice(h * C_HD, (h + 1) * C_HD) for h in range(HEADS)]
    scores = [_dot_nt(q_ref[0, :, cs], k_ref[0, :, cs].astype(BF16)) for cs in cols]
    probs = []
    for s in scores:
        s = s * (scale * LOG2E)
        p = jnp.exp2(s - jnp.max(s, axis=-1, keepdims=True))
        probs.append((p / jnp.sum(p, axis=-1, keepdims=True)).astype(BF16))
    outs = [_dot(p, v_ref[0, :, cs].astype(BF16)) for p, cs in zip(probs, cols)]
    for o, cs in zip(outs, cols):
        o_ref[0, :, cs] = o.astype(o_ref.dtype)


def _attn(q, k, v, tq):
    b, t, _ = q.shape
    assert t % tq == 0
    return pl.pallas_call(
        _attn_kernel,
        grid=(b, t // tq),
        in_specs=[
            pl.BlockSpec((1, tq, D_MODEL), lambda i, j: (i, j, 0)),
            pl.BlockSpec((1, N_MEM, D_MODEL), lambda i, j: (i, 0, 0)),
            pl.BlockSpec((1, N_MEM, D_MODEL), lambda i, j: (i, 0, 0)),
        ],
        out_specs=pl.BlockSpec((1, tq, D_MODEL), lambda i, j: (i, j, 0)),
        out_shape=jax.ShapeDtypeStruct((b, t, D_MODEL), BF16),
        compiler_params=_cparams(("parallel", "arbitrary")),
        name="attn",
    )(q, k, v)


CACHE_HALVES = C_HD // LANES
CACHE_ROWS = HEADS * CACHE_HALVES


def _attn_cache_kernel(q_ref, k_ref, v_ref, o_ref, *, t):
    scale = C_HD ** -0.5
    nq = t * HEADS
    lane = lax.broadcasted_iota(jnp.int32, (nq, LANES), 1)
    rowi = lax.broadcasted_iota(jnp.int32, (nq, LANES), 0)
    valid = ((lane % CACHE_ROWS) // HEADS == 0) & (lane % HEADS == rowi % HEADS)
    n_tiles = N_MEM * CACHE_ROWS // LANES
    seqs = range(q_ref.shape[0])
    scores = [_dot_nt(q_ref[bi], k_ref[bi].astype(BF16)) for bi in seqs]
    probs = []
    for s in scores:
        tiles = []
        for j in range(n_tiles):
            cs = slice(j * LANES, (j + 1) * LANES)
            sj = s[0:nq, cs] + pltpu.roll(s[nq:2 * nq, cs], LANES - HEADS, 1)
            tiles.append(jnp.where(valid, sj * (scale * LOG2E), -jnp.inf))
        mx = functools.reduce(jnp.maximum, [jnp.max(x, axis=1, keepdims=True) for x in tiles])
        ps = [jnp.exp2(x - mx) for x in tiles]
        den = functools.reduce(jnp.add, [jnp.sum(x, axis=1, keepdims=True) for x in ps])
        inv = 1.0 / den
        p0 = jnp.concatenate([x * inv for x in ps], axis=1)
        p1 = jnp.concatenate([pltpu.roll(x * inv, HEADS, 1) for x in ps], axis=1)
        probs.append(jnp.concatenate([p0, p1], axis=0).astype(BF16))
    outs = [_dot(p, v_ref[bi].astype(BF16)) for p, bi in zip(probs, seqs)]
    for o, bi in zip(outs, seqs):
        o_ref[bi] = o.astype(o_ref.dtype)


def _attn_cache(q, k, v, bb):
    assert CACHE_HALVES == 2
    b, t, _ = q.shape
    assert b % bb == 0
    nq = t * HEADS
    qv = q.reshape(b, t, HEADS, CACHE_HALVES, LANES).transpose(0, 3, 1, 2, 4)
    qv = qv.reshape(b, CACHE_HALVES * nq, LANES)

    def cache_view(a):
        a = a.reshape(b, N_MEM, HEADS, CACHE_HALVES, LANES).transpose(0, 1, 3, 2, 4)
        return a.reshape(b, N_MEM * CACHE_ROWS, LANES)

    o = pl.pallas_call(
        functools.partial(_attn_cache_kernel, t=t),
        grid=(b // bb,),
        in_specs=[
            pl.BlockSpec((bb, CACHE_HALVES * nq, LANES), lambda i: (i, 0, 0)),
            pl.BlockSpec((bb, N_MEM * CACHE_ROWS, LANES), lambda i: (i, 0, 0)),
            pl.BlockSpec((bb, N_MEM * CACHE_ROWS, LANES), lambda i: (i, 0, 0)),
        ],
        out_specs=pl.BlockSpec((bb, CACHE_HALVES * nq, LANES), lambda i: (i, 0, 0)),
        out_shape=jax.ShapeDtypeStruct((b, CACHE_HALVES * nq, LANES), BF16),
        compiler_params=_cparams(("parallel",)),
        name="attn_cache",
    )(qv, cache_view(k), cache_view(v))
    return o.reshape(b, CACHE_HALVES, t, HEADS, LANES).transpose(0, 2, 3, 1, 4).reshape(b, t, D_MODEL)


def _tile_lanes(rep, width):
    if width <= LANES:
        return rep[:, :width]
    return jnp.concatenate([rep] * (width // LANES), axis=1)


def _mlstm_kernel(q_ref, k_ref, v_ref, sm_ref, bias_ref, c0_ref, n0_ref, m0_ref,
                  h_ref, c1_ref, n1_ref, m1_ref, *scratch, chunk, t_valid, carried):
    L = chunk
    bb, tb = q_ref.shape[0], q_ref.shape[1]
    scale = DK ** -0.5

    if carried:
        c_scr, n_scr, m_scr = scratch

        @pl.when(pl.program_id(1) == 0)
        def _():
            c_scr[...] = c0_ref[0]
            n_scr[...] = n0_ref[0]
            m_scr[...] = m0_ref[0]

    row = lax.broadcasted_iota(jnp.int32, (L, L), 0)
    col = lax.broadcasted_iota(jnp.int32, (L, L), 1)
    tri = col <= row
    tri_lo = jnp.where(tri, 1.0, 0.0).astype(BF16)
    lane = lax.broadcasted_iota(jnp.int32, (L, LANES), 1)
    spread = jnp.where(lax.broadcasted_iota(jnp.int32, (LANES, 2 * HEADS * LANES), 0)
                       == lax.broadcasted_iota(jnp.int32, (LANES, 2 * HEADS * LANES), 1) // LANES,
                       1.0, 0.0).astype(BF16)
    pick = jnp.where(lax.broadcasted_iota(jnp.int32, (SUBLANES, LANES), 0)
                     == lax.broadcasted_iota(jnp.int32, (SUBLANES, LANES), 1), 1.0, 0.0).astype(BF16)

    def chunk_group(bis, r):
        rows = pl.ds(r, L)
        cols_of, rows_of = {}, {}
        for bi in bis:
            g = sm_ref[bi, rows, :] + bias_ref[...]
            lf_all = _log_sigmoid(g)
            ig_all = g
            if t_valid < L:
                valid = lax.broadcasted_iota(jnp.int32, (L, 1), 0) < t_valid
                lf_all = jnp.where(valid, lf_all, 0.0)
                ig_all = jnp.where(valid, g, -1e30)
            f_all = _dot_exact_left(tri_lo, lf_all)
            x = jnp.where(lane < SM_F, ig_all, f_all) * LOG2E
            parts = _split3(x)
            cols_of[bi] = sum(_dot(p, spread) for p in parts)
            rows_of[bi] = sum(_dot_nt(pick, p) for p in parts)
        pairs = [(bi, h) for bi in bis for h in range(HEADS)]

        def state(bi):
            if carried:
                return c_scr, n_scr, m_scr, c_scr, n_scr, m_scr
            return (c0_ref.at[bi], n0_ref.at[bi], m0_ref.at[bi], c1_ref.at[bi], n1_ref.at[bi], m1_ref.at[bi])

        st = {}
        for bi, h in pairs:
            c_in, n_in, m_in = state(bi)[:3]
            slab = lambda c, bi=bi: cols_of[bi][:, c * LANES:(c + 1) * LANES]
            f_rep = slab(SM_F + h)
            ig_rep = slab(SM_I + h)
            f_row = rows_of[bi][SM_F + h:SM_F + h + 1, :]
            ig_row = rows_of[bi][SM_I + h:SM_I + h + 1, :]
            dmat = jnp.where(tri, _tile_lanes(f_rep, L) - f_row + ig_row, -jnp.inf)
            m_inter = m_in[h:h + 1, :] * LOG2E + f_rep
            m = jnp.maximum(m_inter, jnp.max(dmat, axis=1, keepdims=True))
            w = jnp.exp2(dmat - _tile_lanes(m - math.log2(scale), L))
            a = jnp.exp2(m_inter - m)
            m_end = m[L - 1:L, :]
            w_rep = jnp.exp2(f_rep[L - 1:L, :] - f_rep + ig_rep - (m_end - math.log2(scale)))
            st[bi, h] = dict(m=m, w=w, a=a, m_end=m_end, w_rep=w_rep, c0=c_in[h], n0=n_in[h:h + 1, :],
                             qb=q_ref[bi, rows, h * DK:(h + 1) * DK], kb=k_ref[bi, rows, h * DK:(h + 1) * DK],
                             vb=v_ref[bi, rows, h * DV:(h + 1) * DV])
        for key in pairs:
            d = st[key]
            d["s"] = _dot_nt(d["qb"], d["kb"]) * d["w"]
        for key in pairs:
            d = st[key]
            d["inter"] = _dot_nt(d["qb"], d["c0"].astype(BF16))
        for key in pairs:
            d = st[key]
            d["sv"] = _dot(d["s"].astype(BF16), d["vb"])
        for key in pairs:
            d = st[key]
            vw_t = (d["vb"].astype(F32) * _tile_lanes(d["w_rep"], DV)).T.astype(BF16)
            d["c_new"] = d["a"][L - 1:L, :] * d["c0"] + _dot(vw_t, d["kb"])
        for bi, h in pairs:
            d = st[bi, h]
            a, m = d["a"], d["m"]
            qf = d["qb"].astype(F32)
            kf = d["kb"].astype(F32)
            num = _tile_lanes(a, DV) * d["inter"] + d["sv"]
            den = a * jnp.sum(qf * d["n0"], axis=1, keepdims=True) + jnp.sum(d["s"], axis=1, keepdims=True)
            hh = num / _tile_lanes(jnp.maximum(jnp.abs(den), jnp.exp2(-m)), DV)
            d["n_new"] = a[L - 1:L, :] * d["n0"] + jnp.sum(kf * d["w_rep"], axis=0, keepdims=True)
            d["hn"] = hh.astype(h_ref.dtype)
        for bi, h in pairs:
            d = st[bi, h]
            c_out, n_out, m_out = state(bi)[3:]
            c_out[h] = d["c_new"]
            n_out[h:h + 1, :] = d["n_new"]
            m_out[h:h + 1, :] = d["m_end"] * (1.0 / LOG2E)
            h_ref[bi, rows, h * DV:(h + 1) * DV] = d["hn"]

    if tb == L:
        chunk_group(list(range(bb)), 0)
    else:
        assert bb == 1

        def loop_body(ci, carry):
            chunk_group([0], pl.multiple_of(ci * L, L))
            return carry

        lax.fori_loop(0, tb // L, loop_body, 0, unroll=2)

    if carried:
        @pl.when(pl.program_id(1) == pl.num_programs(1) - 1)
        def _():
            c1_ref[0] = c_scr[...]
            n1_ref[0] = n_scr[...]
            m1_ref[0] = m_scr[...]


def _mixer_specs():
    bt = lambda col: (lambda i, j: (i, j, col))
    return bt, (lambda i, j: (i, 0, 0, 0)), (lambda i, j: (i, 0, 0))


def _mlstm(z3, sm3, bias_row, c0, n0, m0, chunk, tb, bb, t_valid):
    b, t, _ = z3.shape
    assert t % tb == 0 and tb % chunk == 0 and b % bb == 0
    assert t_valid == t or (tb == t and chunk == t)
    carried = t // tb > 1
    assert not (carried and bb > 1)
    m0 = jnp.broadcast_to(m0[:, :, None], (b, HEADS, LANES))
    bt, st4, st3 = _mixer_specs()
    scratch = [pltpu.VMEM((HEADS, DV, DK), F32), pltpu.VMEM((HEADS, DK), F32),
               pltpu.VMEM((HEADS, LANES), F32)] if carried else []
    h, c1, n1, m1 = pl.pallas_call(
        functools.partial(_mlstm_kernel, chunk=chunk, t_valid=t_valid, carried=carried),
        grid=(b // bb, t // tb),
        in_specs=[
            pl.BlockSpec((bb, tb, QK), bt(Z_MQ // QK)),
            pl.BlockSpec((bb, tb, QK), bt(Z_MK // QK)),
            pl.BlockSpec((bb, tb, VW), bt(Z_MV // VW)),
            pl.BlockSpec((bb, tb, LANES), bt(0)),
            pl.BlockSpec((1, LANES), lambda i, j: (0, 0)),
            pl.BlockSpec((bb, HEADS, DV, DK), st4),
            pl.BlockSpec((bb, HEADS, DK), st3),
            pl.BlockSpec((bb, HEADS, LANES), st3),
        ],
        out_specs=[
            pl.BlockSpec((bb, tb, VW), lambda i, j: (i, j, 0)),
            pl.BlockSpec((bb, HEADS, DV, DK), st4),
            pl.BlockSpec((bb, HEADS, DK), st3),
            pl.BlockSpec((bb, HEADS, LANES), st3),
        ],
        out_shape=[
            jax.ShapeDtypeStruct((b, t, VW), BF16),
            jax.ShapeDtypeStruct((b, HEADS, DV, DK), F32),
            jax.ShapeDtypeStruct((b, HEADS, DK), F32),
            jax.ShapeDtypeStruct((b, HEADS, LANES), F32),
        ],
        scratch_shapes=scratch,
        compiler_params=_cparams(("parallel", "arbitrary")),
        name="mlstm",
    )(z3, z3, z3, sm3, bias_row, c0, n0, m0)
    return h, c1, n1, m1[:, :, 0]


def _gla_kernel(q_ref, k_ref, v_ref, b_ref, s0_ref, h_ref, s1_ref, *scratch, chunk, sub, carried):
    L = chunk
    nb = L // sub
    bb, tb = q_ref.shape[0], q_ref.shape[1]
    scale = DK ** -0.5

    if carried:
        s_scr, kf_scr = scratch

        @pl.when(pl.program_id(1) == 0)
        def _():
            s_scr[...] = s0_ref[0]
    else:
        (kf_scr,) = scratch

    row = lax.broadcasted_iota(jnp.int32, (L, L), 0)
    col = lax.broadcasted_iota(jnp.int32, (L, L), 1)
    in_blk = (col // sub) == (row // sub)
    trow = lax.broadcasted_iota(jnp.int32, (sub, DK), 0)
    eye = lax.broadcasted_iota(jnp.int32, (DK, DK), 0) == lax.broadcasted_iota(jnp.int32, (DK, DK), 1)
    place = jnp.where(lax.broadcasted_iota(jnp.int32, (sub * DK, LANES), 0) // DK
                      == lax.broadcasted_iota(jnp.int32, (sub * DK, LANES), 1) % sub, 1.0, 0.0).astype(BF16)

    def chunk_group(bis, r):
        rows = pl.ds(r, L)
        pairs = [(bi, h) for bi in bis for h in range(HEADS)]

        def state(bi):
            return (s_scr, s_scr) if carried else (s0_ref.at[bi], s1_ref.at[bi])

        slabs = []
        for bi, h in pairs:
            ks = slice(h * DK, (h + 1) * DK)
            qf = q_ref[bi, rows, ks].astype(F32) * scale
            for i in range(nb):
                blk = pl.ds(pl.multiple_of(r + i * sub, sub), sub)
                bs = b_ref[bi, blk, ks]
                kblk = kf_scr[bi, blk, ks]
                qs = qf[i * sub:(i + 1) * sub, :]
                row_slabs = []
                for s in range(sub):
                    e = jnp.exp2(jnp.where(trow >= s, bs - bs[s:s + 1, :], -jnp.inf))
                    row_slabs.append(e * qs * kblk[s:s + 1, :])
                slabs.append(jnp.concatenate(row_slabs, axis=1))
        own = _dot(jnp.concatenate(slabs, axis=0).astype(BF16), place)

        st = {}
        for n, (bi, h) in enumerate(pairs):
            ks = slice(h * DK, (h + 1) * DK)
            vs = slice(h * DV, (h + 1) * DV)
            bh = b_ref[bi, rows, ks]
            qf = q_ref[bi, rows, ks].astype(F32) * scale
            kf = kf_scr[bi, rows, ks]
            s0 = state(bi)[0][h]
            d = dict(vb=v_ref[bi, rows, vs], s0=s0, s0b=s0.astype(BF16),
                     qhat=(qf * jnp.exp2(bh)).astype(BF16), a_own=own[n * L:(n + 1) * L, 0:L])
            d["qk"] = []
            for i in range(1, nb):
                r0 = i * sub
                anchor = bh[r0:r0 + 1, :]
                qt = qf[r0:r0 + sub, :] * jnp.exp2(bh[r0:r0 + sub, :] - anchor)
                kt = kf[0:r0, :] * jnp.exp2(anchor - bh[0:r0, :])
                kt = jnp.concatenate([kt, jnp.zeros((L - r0, DK), F32)], axis=0)
                d["qk"].append((qt.astype(BF16), kt.astype(BF16)))
            b_end = bh[L - 1:L, :]
            d["e_col"] = jnp.sum(jnp.where(eye, jnp.exp2(b_end), 0.0), axis=1, keepdims=True)
            d["ke_t"] = (kf * jnp.exp2(b_end - bh)).T.astype(BF16)
            st[bi, h] = d
        for key in pairs:
            d = st[key]
            d["o"] = _dot(d["qhat"], d["s0b"])
        for key in pairs:
            d = st[key]
            d["blocks"] = [_dot_nt(qt, kt) for qt, kt in d["qk"]]
        for key in pairs:
            d = st[key]
            d["s_new"] = d["e_col"] * d["s0"] + _dot(d["ke_t"], d["vb"])
        for key in pairs:
            d = st[key]
            if nb > 1:
                below = jnp.concatenate([jnp.zeros((sub, L), F32)] + d["blocks"], axis=0)
                a_intra = jnp.where(in_blk, d["a_own"], below)
                d["o"] = d["o"] + _dot(a_intra.astype(BF16), d["vb"])
            else:
                d["o"] = d["o"] + _dot(d["a_own"], d["vb"].astype(F32))
        for bi, h in pairs:
            d = st[bi, h]
            d["hn"] = d["o"].astype(h_ref.dtype)
        for bi, h in pairs:
            d = st[bi, h]
            state(bi)[1][h] = d["s_new"]
            h_ref[bi, rows, h * DV:(h + 1) * DV] = d["hn"]

    for bi in range(bb):
        kf_scr[bi] = k_ref[bi].astype(F32)
    if tb == L:
        chunk_group(list(range(bb)), 0)
    else:
        assert bb == 1

        def loop_body(ci, carry):
            chunk_group([0], pl.multiple_of(ci * L, L))
            return carry

        lax.fori_loop(0, tb // L, loop_body, 0, unroll=2)

    if carried:
        @pl.when(pl.program_id(1) == pl.num_programs(1) - 1)
        def _():
            s1_ref[0] = s_scr[...]


def _gla(z3, b3, s0, chunk, sub, tb, bb):
    b, t, _ = z3.shape
    assert t % tb == 0 and tb % chunk == 0 and chunk % sub == 0 and b % bb == 0
    carried = t // tb > 1
    assert not (carried and bb > 1)
    bt, st4, _ = _mixer_specs()
    scratch = [pltpu.VMEM((bb, tb, QK), F32)]
    if carried:
        scratch = [pltpu.VMEM((HEADS, DK, DV), F32)] + scratch
    h, s1 = pl.pallas_call(
        functools.partial(_gla_kernel, chunk=chunk, sub=sub, carried=carried),
        grid=(b // bb, t // tb),
        in_specs=[
            pl.BlockSpec((bb, tb, QK), bt(Z_GQ // QK)),
            pl.BlockSpec((bb, tb, QK), bt(Z_GK // QK)),
            pl.BlockSpec((bb, tb, VW), bt(Z_GV // VW)),
            pl.BlockSpec((bb, tb, QK), bt(0)),
            pl.BlockSpec((bb, HEADS, DK, DV), st4),
        ],
        out_specs=[
            pl.BlockSpec((bb, tb, VW), lambda i, j: (i, j, 0)),
            pl.BlockSpec((bb, HEADS, DK, DV), st4),
        ],
        out_shape=[
            jax.ShapeDtypeStruct((b, t, VW), BF16),
            jax.ShapeDtypeStruct((b, HEADS, DK, DV), F32),
        ],
        scratch_shapes=scratch,
        compiler_params=_cparams(("parallel", "arbitrary")),
        name="gla",
    )(z3, z3, z3, b3, s0)
    return h, s1


def _layer(x3, mem_k, mem_v, c0, n0, m0, s0, p, final_norm, m_chunk, g_chunk, g_sub, tb, bb, tq):
    b, t, _ = x3.shape
    n = b * t
    tmix = -(-t // tb) * tb
    x = x3.reshape(n, D_MODEL)
    x = _ffn(x, p["ffn1_norm"], p["ffn1_wg"], p["ffn1_wu"], p["ffn1_wd"])
    z, sm, bdec = _inproj(x, p["mix_norm"], p["w_in"], p["w_in_side"], p["wa_pad"], p["gla_ba"],
                          chunk=min(g_chunk, t))
    pad_t = lambda a, mode="constant": jnp.pad(a, ((0, 0), (0, tmix - t), (0, 0)), mode=mode)
    z3 = pad_t(z.reshape(b, t, Z_WIDTH))
    sm3 = pad_t(sm.reshape(b, t, LANES))
    b3 = pad_t(bdec.reshape(b, t, QK), "edge")
    hm, c1, n1, m1 = _mlstm(z3, sm3, p["bias_row"], c0, n0, m0, m_chunk, tb, bb, t)
    hg, s1 = _gla(z3, b3, s0, g_chunk, g_sub, tb, bb)
    x, q = _merge(z, hm[:, :t].reshape(n, VW), hg[:, :t].reshape(n, VW), x, p["mlstm_norm"], p["gla_norm"],
                  p["w_br_m"], p["w_br_g"], p["w_out"], p["ca_norm"], p["ca_wq"])
    if mem_k.ndim == 4:
        o = _attn_cache(q.reshape(b, t, D_MODEL), mem_k, mem_v, bb=4)
    else:
        o = _attn(q.reshape(b, t, D_MODEL), mem_k, mem_v, tq)
    y = _ffn(x, p["ffn2_norm"], p["ffn2_wg"], p["ffn2_wu"], p["ffn2_wd"], final_g=final_norm,
             res_a=o.reshape(n, D_MODEL), res_w=p["ca_wo"])
    return y.reshape(b, t, D_MODEL), (c1, n1, m1, s1)


def _permute_w_in(w_in):
    sizes = (QK, QK, VW, VW, HEADS, HEADS, QK, QK, VW, VW, G_RANK, D_MODEL, D_MODEL)
    w_in = w_in.astype(BF16)
    offs = [0]
    for s in sizes:
        offs.append(offs[-1] + s)
    part = lambda i: w_in[:, offs[i]:offs[i + 1]]
    (mq, mk, mv, mo, mi, mf, gq, gk, gv, gr, ga, gate_m, gate_g) = [part(i) for i in range(len(sizes))]
    pad = jnp.zeros((D_MODEL, LANES - 2 * HEADS - G_RANK), w_in.dtype)
    w_main = jnp.concatenate([mq, mk, mv, gq, gk, gv, 0.5 * mo, 0.5 * gr, 0.5 * gate_m, 0.5 * gate_g], axis=1)
    w_side = jnp.concatenate([mi, mf, ga, pad], axis=1)
    return w_main, w_side


def kernel(x_prompt, x_sample, mem_prompt, state_mlstm_C, state_mlstm_n, state_mlstm_m, state_gla_S, cache_mem_k, cache_mem_v, ffn1_norm, ffn1_wg, ffn1_wu, ffn1_wd, mix_norm, w_in, b_if, gla_wa2, gla_ba, mlstm_norm, gla_norm, w_br_m, w_br_g, w_out, ca_norm, mem_norm, ca_wq, ca_wk, ca_wv, ca_wo, ffn2_norm, ffn2_wg, ffn2_wu, ffn2_wd, final_norm):
    depth = ffn1_norm.shape[0]
    assert depth == 1
    l = 0
    bp, tp, _ = x_prompt.shape
    bs, ts, _ = x_sample.shape
    row = lambda v: v.reshape(1, -1).astype(F32)
    bias_row = jnp.pad(b_if[l].astype(F32)[None, :], ((0, 0), (0, LANES - 2 * HEADS)))
    wa_pad = jnp.pad(gla_wa2[l].astype(F32), ((SM_A, LANES - SM_A - G_RANK), (0, 0)))
    w_in_main, w_in_side = _permute_w_in(w_in[l])
    p = {
        "w_in": w_in_main, "w_in_side": w_in_side,
        "ffn1_norm": row(ffn1_norm[l]), "ffn1_wg": (0.5 * ffn1_wg[l]).astype(BF16),
        "ffn1_wu": ffn1_wu[l].astype(BF16),
        "ffn1_wd": ffn1_wd[l].astype(BF16),
        "mix_norm": row(mix_norm[l]),
        "bias_row": bias_row, "wa_pad": wa_pad, "gla_ba": row(gla_ba[l]),
        "mlstm_norm": row(mlstm_norm[l]), "gla_norm": row(gla_norm[l]),
        "w_br_m": w_br_m[l].astype(BF16), "w_br_g": w_br_g[l].astype(BF16), "w_out": w_out[l].astype(BF16),
        "ca_norm": row(ca_norm[l]), "ca_wq": ca_wq[l].astype(BF16), "ca_wo": ca_wo[l].astype(BF16),
        "ffn2_norm": row(ffn2_norm[l]), "ffn2_wg": (0.5 * ffn2_wg[l]).astype(BF16),
        "ffn2_wu": ffn2_wu[l].astype(BF16),
        "ffn2_wd": ffn2_wd[l].astype(BF16),
    }
    fin = row(final_norm)

    mem2 = mem_prompt.reshape(bp * N_MEM, D_MODEL)
    mk_p, mv_p = _memkv(mem2, row(mem_norm[l]), ca_wk[l].astype(BF16), ca_wv[l].astype(BF16))
    mk_p = mk_p.reshape(bp, N_MEM, D_MODEL)
    mv_p = mv_p.reshape(bp, N_MEM, D_MODEL)
    zc = jnp.zeros((bp, HEADS, DV, DK), F32)
    zn = jnp.zeros((bp, HEADS, DK), F32)
    zm = jnp.zeros((bp, HEADS), F32)
    zs = jnp.zeros((bp, HEADS, DK, DV), F32)
    yp, (cp, np_, mp, sp) = _layer(x_prompt, mk_p, mv_p, zc, zn, zm, zs, p, fin,
                                   m_chunk=128, g_chunk=64, g_sub=8, tb=512, bb=1, tq=2048)

    tpad = -(-ts // SUBLANES) * SUBLANES
    ys, (cs, ns, ms, ss) = _layer(x_sample, cache_mem_k[l], cache_mem_v[l],
                                  state_mlstm_C[l], state_mlstm_n[l], state_mlstm_m[l], state_gla_S[l],
                                  p, fin, m_chunk=tpad, g_chunk=tpad, g_sub=tpad, tb=tpad, bb=16, tq=None)

    st = lambda a: a[None]
    return (yp, ys, st(cp), st(np_), st(mp), st(sp),
            st(mk_p.reshape(bp, N_MEM, HEADS, C_HD)), st(mv_p.reshape(bp, N_MEM, HEADS, C_HD)),
            st(cs), st(ns), st(ms), st(ss))
```
